```python
import math
import jax, jax.numpy as jnp
from jax import lax
import numpy as np

D_MODEL = 1024
BATCH = 2
SEQ = 8192
DEPTH = 2

PLE_DIM = 256
ROPE_THETA = 10000.0
EPS = 1e-6
NEG_INF = -1e30
HEAD_DIM = 64

A_HEADS = 8
A_KV_GROUPS = 2
A_CMP_LEN = 32
A_CMP_STRIDE = 16
A_CMP_HIDDEN = 256
A_SEL_LEN = 64
A_TOPK = 16
A_WINDOW = 512
A_Q_CHUNK = 128
A_FORCE_BONUS = 1e4

B_HEADS = 8
B_KV_HEADS = 2
B_WINDOW = 128
B_BLOCK = 128

C_HEADS = 8
C_Q_RANK = 256
C_KV_RANK = 256
C_NOPE = 64
C_ROPE = 32
C_V = 64
C_Q_BLOCK = 128

D_FF = int(math.ceil(8 * D_MODEL / 3 / 256)) * 256

A_Q = A_HEADS * HEAD_DIM
A_KV = A_KV_GROUPS * HEAD_DIM
A_GATES = A_HEADS * 3
B_Q = B_HEADS * HEAD_DIM
B_KV = B_KV_HEADS * HEAD_DIM
IN_SIZES = (A_Q, A_KV, A_KV, A_KV, A_KV, A_KV, A_KV, A_GATES, B_Q, B_KV, B_KV, C_Q_RANK, C_KV_RANK, C_ROPE)
IN_COLS = sum(IN_SIZES)
A_OUT = A_HEADS * HEAD_DIM
B_OUT = B_HEADS * HEAD_DIM
C_OUT = C_HEADS * C_V

kernel_name = "hybrid_nsa_swa_sink_mla_gated_block"


def rmsnorm(x, g):
    xf = x.astype(jnp.float32)
    y = xf * lax.rsqrt(jnp.mean(xf * xf, axis=-1, keepdims=True) + EPS)
    return (y * g.astype(jnp.float32)).astype(x.dtype)


def rope_tables(positions, dim):
    inv_freq = jnp.power(jnp.float32(ROPE_THETA), -jnp.arange(0, dim, 2, dtype=jnp.float32) / dim)
    ang = positions.astype(jnp.float32)[..., None] * inv_freq
    return jnp.cos(ang), jnp.sin(ang)


def apply_rope(x, cos, sin):
    half = x.shape[-1] // 2
    xf = x.astype(jnp.float32)
    x1, x2 = xf[..., :half], xf[..., half:]
    c, s = cos[:, :, None, :], sin[:, :, None, :]
    return jnp.concatenate([x1 * c - x2 * s, x2 * c + x1 * s], axis=-1).astype(x.dtype)


def masked_softmax(scores, mask):
    p = jax.nn.softmax(jnp.where(mask, scores, NEG_INF), axis=-1)
    return jnp.where(mask, p, 0.0)


def compress_blocks(t, tok, pos, w1, w2):
    B, _, G, dh = t.shape
    n_cmp, L = tok.shape
    blocks = t[:, tok] + pos[:, None, :].astype(t.dtype)
    flat = blocks.transpose(0, 1, 3, 2, 4).reshape(B, n_cmp, G, L * dh)
    return jax.nn.gelu(flat @ w1) @ w2


def nsa_attention(q, kc, vc, ks, vs, kw, vw, gate_logits, pos_k, w1_k, w2_k, pos_v, w1_v, w2_v):
    B, S = q.shape[0], q.shape[1]
    G, R, dh = A_KV_GROUPS, A_HEADS // A_KV_GROUPS, HEAD_DIM
    QC = A_Q_CHUNK
    scale = dh ** -0.5
    dt = q.dtype
    qg = q.reshape(B, S, G, R, dh)
    gates = jax.nn.sigmoid(gate_logits.astype(jnp.float32)).reshape(B, S, G, R, 3).astype(dt)
    n_cmp = (S - A_CMP_LEN) // A_CMP_STRIDE + 1
    tok = np.arange(n_cmp)[:, None] * A_CMP_STRIDE + np.arange(A_CMP_LEN)[None, :]
    k_cmp = compress_blocks(kc, tok, pos_k, w1_k, w2_k)
    v_cmp = compress_blocks(vc, tok, pos_v, w1_v, w2_v)
    cmp_end = jnp.asarray(tok[:, -1], jnp.int32)
    n_sel = S // A_SEL_LEN
    sel_map = np.zeros((n_cmp, n_sel), np.float32)
    np.add.at(sel_map, (np.repeat(np.arange(n_cmp), A_CMP_LEN), (tok // A_SEL_LEN).reshape(-1)), 1.0 / A_CMP_LEN)
    sel_map = jnp.asarray(sel_map)
    top_k = min(A_TOPK, n_sel)
    ks_blk = ks.transpose(0, 2, 1, 3).reshape(B, G, n_sel, A_SEL_LEN * dh)
    vs_blk = vs.transpose(0, 2, 1, 3).reshape(B, G, n_sel, A_SEL_LEN * dh)
    gather_blocks = jax.vmap(jax.vmap(lambda t, i: t[i]))
    kw_pad = jnp.pad(kw, ((0, 0), (A_WINDOW, 0), (0, 0), (0, 0)))
    vw_pad = jnp.pad(vw, ((0, 0), (A_WINDOW, 0), (0, 0), (0, 0)))
    blk_ids = jnp.arange(n_sel)

    def chunk(c):
        t0 = c * QC
        tq = t0 + jnp.arange(QC)
        qc = lax.dynamic_slice_in_dim(qg, t0, QC, axis=1)
        gc = lax.dynamic_slice_in_dim(gates, t0, QC, axis=1)
        s = jnp.einsum('bqgrd,bngd->bgrqn', qc, k_cmp).astype(jnp.float32) * scale
        p_cmp = masked_softmax(s, cmp_end[None, :] <= tq[:, None])
        o_cmp = jnp.einsum('bgrqn,bngd->bqgrd', p_cmp.astype(dt), v_cmp)
        imp = jnp.einsum('bgrqn,nj->bgqj', p_cmp, sel_map)
        cur = tq // A_SEL_LEN
        valid = blk_ids[None, :] <= cur[:, None]
        forced = (blk_ids[None, :] == 0) | (blk_ids[None, :] == cur[:, None]) | (blk_ids[None, :] == cur[:, None] - 1)
        imp = jnp.where(valid, imp + jnp.where(forced, A_FORCE_BONUS, 0.0), NEG_INF)
        _, top_idx = lax.top_k(imp, top_k)
        idx_flat = top_idx.reshape(B, G, QC * top_k)
        kb = gather_blocks(ks_blk, idx_flat).reshape(B, G, QC, top_k * A_SEL_LEN, dh)
        vb = gather_blocks(vs_blk, idx_flat).reshape(B, G, QC, top_k * A_SEL_LEN, dh)
        key_pos = (top_idx[..., None] * A_SEL_LEN + jnp.arange(A_SEL_LEN)).reshape(B, G, QC, top_k * A_SEL_LEN)
        m_sel = key_pos <= tq[None, None, :, None]
        s = jnp.einsum('bqgrd,bgqkd->bgrqk', qc, kb).astype(jnp.float32) * scale
        p_sel = masked_softmax(s, m_sel[:, :, None])
        o_sel = jnp.einsum('bgrqk,bgqkd->bqgrd', p_sel.astype(dt), vb)
        kwc = lax.dynamic_slice_in_dim(kw_pad, t0, QC + A_WINDOW, axis=1)
        vwc = lax.dynamic_slice_in_dim(vw_pad, t0, QC + A_WINDOW, axis=1)
        kp = t0 - A_WINDOW + jnp.arange(QC + A_WINDOW)
        m_win = (kp[None, :] <= tq[:, None]) & (kp[None, :] > tq[:, None] - A_WINDOW) & (kp[None, :] >= 0)
        s = jnp.einsum('bqgrd,bkgd->bgrqk', qc, kwc).astype(jnp.float32) * scale
        p_win = masked_softmax(s, m_win)
        o_win = jnp.einsum('bgrqk,bkgd->bqgrd', p_win.astype(dt), vwc)
        return gc[..., 0:1] * o_cmp + gc[..., 1:2] * o_sel + gc[..., 2:3] * o_win

    out = lax.map(chunk, jnp.arange(S // QC))
    return out.transpose(1, 0, 2, 3, 4, 5).reshape(B, S, A_HEADS * dh)


def swa_sink_attention(q, k, v, sinks):
    B, S = q.shape[0], q.shape[1]
    G, R, dh, BLK = B_KV_HEADS, B_HEADS // B_KV_HEADS, HEAD_DIM, B_BLOCK
    nb = S // BLK
    qb = q.reshape(B, nb, BLK, G, R, dh)

    def band(t):
        tb = t.reshape(B, nb, BLK, G, dh)
        prev = jnp.pad(tb, ((0, 0), (1, 0), (0, 0), (0, 0), (0, 0)))[:, :-1]
        return jnp.concatenate([prev, tb], axis=2)

    kb, vb = band(k), band(v)
    s = jnp.einsum('bnqgrd,bnkgd->bngrqk', qb, kb).astype(jnp.float32) * (dh ** -0.5)
    qi = jnp.arange(BLK)[:, None]
    kj = jnp.arange(2 * BLK)[None, :] - BLK
    rel = (kj <= qi) & (kj > qi - B_WINDOW)
    mask = rel[None] & ((jnp.arange(nb)[:, None, None] > 0) | (kj[None] >= 0))
    s = jnp.where(mask[None, :, None, None], s, NEG_INF)
    sink = jnp.broadcast_to(sinks.astype(jnp.float32).reshape(G, R)[None, None, :, :, None, None], s.shape[:-1] + (1,))
    p = jax.nn.softmax(jnp.concatenate([s, sink], axis=-1), axis=-1)[..., :-1]
    o = jnp.einsum('bngrqk,bnkgd->bnqgrd', p.astype(v.dtype), vb)
    return o.reshape(B, S, B_HEADS * dh)


def causal_block_attention(q, k, v, scale):
    B, S, H, _ = q.shape
    dv = v.shape[-1]
    key_pos = jnp.arange(S)

    def body(c):
        t0 = c * C_Q_BLOCK
        qb = lax.dynamic_slice_in_dim(q, t0, C_Q_BLOCK, axis=1)
        s = jnp.einsum('bqhd,bkhd->bhqk', qb, k).astype(jnp.float32) * scale
        mask = key_pos[None, :] <= (t0 + jnp.arange(C_Q_BLOCK))[:, None]
        p = jax.nn.softmax(jnp.where(mask, s, NEG_INF), axis=-1)
        return jnp.einsum('bhqk,bkhd->bqhd', p.astype(v.dtype), v)

    out = lax.map(body, jnp.arange(S // C_Q_BLOCK))
    return out.transpose(1, 0, 2, 3, 4).reshape(B, S, H * dv)


def mla_attention(cq, ckv, k_pe_raw, q_norm, w_q_up, kv_norm, w_kv_up, cos32, sin32):
    B, S = cq.shape[0], cq.shape[1]
    q = (rmsnorm(cq, q_norm) @ w_q_up).reshape(B, S, C_HEADS, C_NOPE + C_ROPE)
    q_nope, q_pe = q[..., :C_NOPE], apply_rope(q[..., C_NOPE:], cos32, sin32)
    kv = (rmsnorm(ckv, kv_norm) @ w_kv_up).reshape(B, S, C_HEADS, C_NOPE + C_V)
    k_nope, v = kv[..., :C_NOPE], kv[..., C_NOPE:]
    k_pe = apply_rope(k_pe_raw[:, :, None, :], cos32, sin32)
    qf = jnp.concatenate([q_nope, q_pe], axis=-1)
    kf = jnp.concatenate([k_nope, jnp.broadcast_to(k_pe, (B, S, C_HEADS, C_ROPE))], axis=-1)
    return causal_block_attention(qf, kf, v, (C_NOPE + C_ROPE) ** -0.5)


def hybrid_layer(x, p_i, rope64, rope32, mix_norm, w_in, a_cmp_pos_k, a_cmp_w1_k, a_cmp_w2_k,
                 a_cmp_pos_v, a_cmp_w1_v, a_cmp_w2_v, b_sinks, c_q_norm, c_w_q_up, c_kv_norm, c_w_kv_up,
                 w_branch_gate, w_branch_a, w_branch_b, w_branch_c, w_out, ffn_norm, w_ffn_gate, w_ffn_up,
                 w_ffn_down, ple_norm, w_ple_proj, w_ple_gate):
    B, S, _ = x.shape
    cos64, sin64 = rope64
    cos32, sin32 = rope32
    h = rmsnorm(x, mix_norm)
    z = h @ w_in
    split_points = [int(v) for v in np.cumsum(IN_SIZES)[:-1]]
    (a_q, a_kc, a_vc, a_ks, a_vs, a_kw, a_vw, a_g, b_q, b_k, b_v, c_cq, c_ckv, c_kpe) = jnp.split(z, split_points, axis=-1)

    def heads(t, n):
        return t.reshape(B, S, n, -1)

    def rot(t, n):
        return apply_rope(heads(t, n), cos64, sin64)

    G = A_KV_GROUPS
    o_a = nsa_attention(rot(a_q, A_HEADS), rot(a_kc, G), heads(a_vc, G), rot(a_ks, G), heads(a_vs, G),
                        rot(a_kw, G), heads(a_vw, G), a_g, a_cmp_pos_k, a_cmp_w1_k, a_cmp_w2_k,
                        a_cmp_pos_v, a_cmp_w1_v, a_cmp_w2_v)
    o_b = swa_sink_attention(rot(b_q, B_HEADS), rot(b_k, B_KV_HEADS), heads(b_v, B_KV_HEADS), b_sinks)
    o_c = mla_attention(c_cq, c_ckv, c_kpe, c_q_norm, c_w_q_up, c_kv_norm, c_w_kv_up, cos32, sin32)

    g = jax.nn.sigmoid((h @ w_branch_gate).astype(jnp.float32)).astype(x.dtype)
    g_a, g_b, g_c = jnp.split(g, 3, axis=-1)
    merged = g_a * (o_a @ w_branch_a) + g_b * (o_b @ w_branch_b) + g_c * (o_c @ w_branch_c)
    x = x + merged @ w_out

    h2 = rmsnorm(x, ffn_norm)
    x = x + (jax.nn.silu(h2 @ w_ffn_gate) * (h2 @ w_ffn_up)) @ w_ffn_down

    gate = jax.nn.sigmoid((rmsnorm(x, ple_norm) @ w_ple_gate).astype(jnp.float32)).astype(x.dtype)
    return x + gate * (p_i.astype(x.dtype) @ w_ple_proj)


def setup_inputs(seed: int = 0) -> dict:
    key = jax.random.key(seed)
    ks = jax.random.split(key, 32)

    def dense(k, shape, fan_in):
        return jax.random.normal(k, shape, jnp.float32) * (fan_in ** -0.5)

    def gain(k, shape):
        return 1.0 + 0.05 * jax.random.normal(k, shape, jnp.float32)

    L, dh = A_CMP_LEN, HEAD_DIM
    return {
        "x": jax.random.normal(ks[0], (BATCH, SEQ, D_MODEL), jnp.float32),
        "p": jax.random.normal(ks[1], (DEPTH, BATCH, SEQ, PLE_DIM), jnp.float32),
        "positions": jnp.broadcast_to(jnp.arange(SEQ, dtype=jnp.int32)[None, :], (BATCH, SEQ)),
        "mix_norm": gain(ks[2], (DEPTH, D_MODEL)),
        "w_in": dense(ks[3], (DEPTH, D_MODEL, IN_COLS), D_MODEL),
        "a_cmp_pos_k": 0.2 * jax.random.normal(ks[4], (DEPTH, L, dh), jnp.float32),
        "a_cmp_w1_k": dense(ks[5], (DEPTH, L * dh, A_CMP_HIDDEN), L * dh),
        "a_cmp_w2_k": dense(ks[6], (DEPTH, A_CMP_HIDDEN, dh), A_CMP_HIDDEN),
        "a_cmp_pos_v": 0.2 * jax.random.normal(ks[7], (DEPTH, L, dh), jnp.float32),
        "a_cmp_w1_v": dense(ks[8], (DEPTH, L * dh, A_CMP_HIDDEN), L * dh),
        "a_cmp_w2_v": dense(ks[9], (DEPTH, A_CMP_HIDDEN, dh), A_CMP_HIDDEN),
        "b_sinks": 0.5 * jax.random.normal(ks[10], (DEPTH, B_HEADS), jnp.float32),
        "c_q_norm": gain(ks[11], (DEPTH, C_Q_RANK)),
        "c_w_q_up": dense(ks[12], (DEPTH, C_Q_RANK, C_HEADS * (C_NOPE + C_ROPE)), C_Q_RANK),
        "c_kv_norm": gain(ks[13], (DEPTH, C_KV_RANK)),
        "c_w_kv_up": dense(ks[14], (DEPTH, C_KV_RANK, C_HEADS * (C_NOPE + C_V)), C_KV_RANK),
        "w_branch_gate": dense(ks[15], (DEPTH, D_MODEL, 3 * D_MODEL), D_MODEL),
        "w_branch_a": dense(ks[16], (DEPTH, A_OUT, D_MODEL), A_OUT),
        "w_branch_b": dense(ks[17], (DEPTH, B_OUT, D_MODEL), B_OUT),
        "w_branch_c": dense(ks[18], (DEPTH, C_OUT, D_MODEL), C_OUT),
        "w_out": dense(ks[19], (DEPTH, D_MODEL, D_MODEL), D_MODEL),
        "ffn_norm": gain(ks[20], (DEPTH, D_MODEL)),
        "w_ffn_gate": dense(ks[21], (DEPTH, D_MODEL, D_FF), D_MODEL),
        "w_ffn_up": dense(ks[22], (DEPTH, D_MODEL, D_FF), D_MODEL),
        "w_ffn_down": dense(ks[23], (DEPTH, D_FF, D_MODEL), D_FF),
        "ple_norm": gain(ks[24], (DEPTH, D_MODEL)),
        "w_ple_proj": dense(ks[25], (DEPTH, PLE_DIM, D_MODEL), PLE_DIM),
        "w_ple_gate": dense(ks[26], (DEPTH, D_MODEL, D_MODEL), D_MODEL),
        "final_norm": gain(ks[27], (D_MODEL,)),
    }


def reference(x, p, positions, mix_norm, w_in, a_cmp_pos_k, a_cmp_w1_k, a_cmp_w2_k, a_cmp_pos_v,
              a_cmp_w1_v, a_cmp_w2_v, b_sinks, c_q_norm, c_w_q_up, c_kv_norm, c_w_kv_up, w_branch_gate,
              w_branch_a, w_branch_b, w_branch_c, w_out, ffn_norm, w_ffn_gate, w_ffn_up, w_ffn_down,
              ple_norm, w_ple_proj, w_ple_gate, final_norm):
    rope64 = rope_tables(positions, HEAD_DIM)
    rope32 = rope_tables(positions, C_ROPE)
    for i in range(DEPTH):
        x = hybrid_layer(x, p[i], rope64, rope32, mix_norm[i], w_in[i], a_cmp_pos_k[i], a_cmp_w1_k[i],
                         a_cmp_w2_k[i], a_cmp_pos_v[i], a_cmp_w1_v[i], a_cmp_w2_v[i], b_sinks[i],
                         c_q_norm[i], c_w_q_up[i], c_kv_norm[i], c_w_kv_up[i], w_branch_gate[i],
                         w_branch_a[i], w_branch_b[i], w_branch_c[i], w_out[i], ffn_norm[i],
                         w_ffn_gate[i], w_ffn_up[i], w_ffn_down[i], ple_norm[i], w_ple_proj[i],
                         w_ple_gate[i])
    return rmsnorm(x, final_norm)
```

```python
import functools
import math

import numpy as np
import jax
import jax.numpy as jnp
from jax import lax
from jax.experimental import pallas as pl
from jax.experimental.pallas import tpu as pltpu

F32 = jnp.float32
BF16 = jnp.bfloat16

D_MODEL = 1024
PLE_DIM = 256
ROPE_THETA = 10000.0
EPS = 1e-6
NEG_INF = -1e30
HEAD_DIM = 64

A_HEADS = 8
A_KV_GROUPS = 2
A_GROUP_HEADS = A_HEADS // A_KV_GROUPS
A_CMP_LEN = 32
A_CMP_STRIDE = 16
A_CMP_HIDDEN = 256
A_SEL_LEN = 64
A_TOPK = 16
A_WINDOW = 512
A_Q_CHUNK = 128
A_FORCE_BONUS = 1e4

B_HEADS = 8
B_KV_HEADS = 2
B_WINDOW = 128
B_BLOCK = 128

C_HEADS = 8
C_Q_RANK = 256
C_KV_RANK = 256
C_NOPE = 64
C_ROPE = 32
C_V = 64
C_QK_PAD = 128

D_FF = int(math.ceil(8 * D_MODEL / 3 / 256)) * 256

LANES = 128
KV_TILE = 512
VMEM_LIMIT = 56 * 1024 * 1024

_ROW_K = 4 * 128
_ROW_COLS = 2 * _ROW_K + 128 + C_Q_RANK + C_KV_RANK
_T_AQ, _T_BQ = 0, 512
_T_VS, _T_VW, _T_BV = 1024, 1152, 1280
_T_G, _T_KPE = 1408, 1440
_T_ROWS = 1472
_GATE_ROWS = 16


def _cparams(sem):
    return pltpu.CompilerParams(dimension_semantics=sem, vmem_limit_bytes=VMEM_LIMIT)


def _rms(x, g):
    return x * lax.rsqrt(jnp.mean(x * x, axis=-1, keepdims=True) + EPS) * g


def _sigmoid(x):
    return 1.0 / (1.0 + jnp.exp(-x))


def _dot(a, b):
    return jnp.dot(a, b, preferred_element_type=F32)


def _dot_nt(a, b):
    return lax.dot_general(a, b, (((1,), (1,)), ((), ())), preferred_element_type=F32)


def _rope_kernel(pc_ref, pr_ref, i128_ref, i64_ref, i32_ref,
                 c128_ref, s128_ref, c64t_ref, s64t_ref, c32t_ref, s32t_ref):
    ang = pc_ref[...].astype(F32) * i128_ref[...]
    c128_ref[...] = jnp.cos(ang)
    s128_ref[...] = jnp.sin(ang)
    pr = pr_ref[...].astype(F32)
    a64 = i64_ref[...] * pr
    c64t_ref[...] = jnp.cos(a64)
    s64t_ref[...] = jnp.sin(a64)
    a32 = i32_ref[...] * pr
    c32t_ref[...] = jnp.cos(a32)
    s32t_ref[...] = jnp.sin(a32)


def _rope_tables(positions, tm):
    T = positions.size
    pos = positions.reshape(T)
    inv64 = jnp.power(F32(ROPE_THETA), -jnp.arange(0, HEAD_DIM, 2, dtype=F32) / HEAD_DIM)
    inv32 = jnp.power(F32(ROPE_THETA), -jnp.arange(0, C_ROPE, 2, dtype=F32) / C_ROPE)
    inv128 = jnp.tile(inv64, 4).reshape(1, 128)
    full = lambda shape: pl.BlockSpec(shape, lambda i: (0,) * len(shape))
    return pl.pallas_call(
        _rope_kernel,
        grid=(T // tm,),
        in_specs=[pl.BlockSpec((tm, 1), lambda i: (i, 0)),
                  pl.BlockSpec((1, tm), lambda i: (0, i)),
                  full((1, 128)), full((32, 1)), full((16, 1))],
        out_specs=[pl.BlockSpec((tm, 128), lambda i: (i, 0)),
                   pl.BlockSpec((tm, 128), lambda i: (i, 0)),
                   pl.BlockSpec((32, tm), lambda i: (0, i)),
                   pl.BlockSpec((32, tm), lambda i: (0, i)),
                   pl.BlockSpec((16, tm), lambda i: (0, i)),
                   pl.BlockSpec((16, tm), lambda i: (0, i))],
        out_shape=[jax.ShapeDtypeStruct((T, 128), F32), jax.ShapeDtypeStruct((T, 128), F32),
                   jax.ShapeDtypeStruct((32, T), F32), jax.ShapeDtypeStruct((32, T), F32),
                   jax.ShapeDtypeStruct((16, T), F32), jax.ShapeDtypeStruct((16, T), F32)],
        compiler_params=_cparams(("arbitrary",)),
        name="rope_tables",
    )(pos.reshape(T, 1), pos.reshape(1, T), inv128, inv64.reshape(32, 1), inv32.reshape(16, 1))


def _inproj_kernel(x_ref, nrm_ref, wr_ref, wt_ref, c128_ref, s128_ref, c64t_ref, s64t_ref,
                   c32t_ref, s32t_ref,
                   kc_ref, ks_ref, kw_ref, bk_ref, vc_ref, cq_ref, ckv_ref,
                   aqt_ref, bqt_ref, vst_ref, vwt_ref, bvt_ref, gt_ref, kpet_ref):
    h = _rms(x_ref[...], nrm_ref[...]).astype(BF16)
    zr = _dot(h, wr_ref[...])
    c = c128_ref[...]
    s = s128_ref[...]
    for i, ref in enumerate((kc_ref, ks_ref, kw_ref, bk_ref)):
        v = zr[:, 128 * i:128 * (i + 1)] * c + zr[:, _ROW_K + 128 * i:_ROW_K + 128 * (i + 1)] * s
        ref[...] = v.astype(ref.dtype)
    o = 2 * _ROW_K
    vc_ref[...] = zr[:, o:o + 128]
    cq_ref[...] = zr[:, o + 128:o + 128 + C_Q_RANK]
    ckv_ref[...] = zr[:, o + 128 + C_Q_RANK:o + 128 + C_Q_RANK + C_KV_RANK]

    zt = _dot_nt(wt_ref[...], h)
    ct = c64t_ref[...]
    st = s64t_ref[...]
    qscale = HEAD_DIM ** -0.5
    for q_ref, base in ((aqt_ref, _T_AQ), (bqt_ref, _T_BQ)):
        for hh in range(8):
            r0 = base + hh * HEAD_DIM
            x1 = zt[r0:r0 + 32]
            x2 = zt[r0 + 32:r0 + 64]
            q_ref[hh * 64:hh * 64 + 32, :] = ((x1 * ct - x2 * st) * qscale).astype(BF16)
            q_ref[hh * 64 + 32:hh * 64 + 64, :] = ((x2 * ct + x1 * st) * qscale).astype(BF16)
    vst_ref[...] = zt[_T_VS:_T_VS + 128].astype(BF16)
    vwt_ref[...] = zt[_T_VW:_T_VW + 128].astype(BF16)
    bvt_ref[...] = zt[_T_BV:_T_BV + 128].astype(BF16)
    gt_ref[...] = _sigmoid(zt[_T_G:_T_G + 2 * _GATE_ROWS])
    c3 = c32t_ref[...]
    s3 = s32t_ref[...]
    x1 = zt[_T_KPE:_T_KPE + 16]
    x2 = zt[_T_KPE + 16:_T_KPE + 32]
    kpet_ref[0:16, :] = x1 * c3 - x2 * s3
    kpet_ref[16:32, :] = x2 * c3 + x1 * s3


def _rot_cols(w, dim):
    k, n = w.shape
    w4 = w.reshape(k, n // dim, 2, dim // 2)
    return jnp.concatenate([-w4[:, :, 1:2], w4[:, :, 0:1]], axis=2).reshape(k, n)


def _inproj(x, mix_norm, w_in, tables, tm):
    T = x.shape[0]
    c128, s128, c64t, s64t, c32t, s32t = tables
    sizes = (512, 128, 128, 128, 128, 128, 128, 24, 512, 128, 128, 256, 256, 32)
    offs = np.concatenate([[0], np.cumsum(sizes)])
    (w_aq, w_kc, w_vc, w_ks, w_vs, w_kw, w_vw, w_g, w_bq, w_bk, w_bv, w_cq, w_ckv, w_kpe) = [
        w_in[:, int(offs[i]):int(offs[i + 1])] for i in range(len(sizes))]
    kmain = jnp.concatenate([w_kc, w_ks, w_kw, w_bk], axis=1)
    w_row = jnp.concatenate([kmain, _rot_cols(kmain, HEAD_DIM), w_vc, w_cq, w_ckv], axis=1).astype(BF16)
    wg = w_g.reshape(D_MODEL, A_KV_GROUPS, A_GROUP_HEADS * 3)
    wg = jnp.pad(wg, ((0, 0), (0, 0), (0, _GATE_ROWS - A_GROUP_HEADS * 3))).reshape(D_MODEL, 2 * _GATE_ROWS)
    w_t = jnp.concatenate([w_aq, w_bq, w_vs, w_vw, w_bv, wg, w_kpe], axis=1).T.astype(BF16)

    row = lambda n: pl.BlockSpec((tm, n), lambda i: (i, 0))
    colT = lambda n: pl.BlockSpec((n, tm), lambda i: (0, i))
    full = lambda a: pl.BlockSpec(a.shape, lambda i: (0,) * a.ndim)
    nrm = mix_norm.reshape(1, D_MODEL)
    outs = pl.pallas_call(
        _inproj_kernel,
        grid=(T // tm,),
        in_specs=[row(D_MODEL), full(nrm), full(w_row), full(w_t), row(128), row(128),
                  colT(32), colT(32), colT(16), colT(16)],
        out_specs=[row(128), row(128), row(128), row(128), row(128), row(C_Q_RANK), row(C_KV_RANK),
                   colT(512), colT(512), colT(128), colT(128), colT(128), colT(2 * _GATE_ROWS), colT(32)],
        out_shape=[jax.ShapeDtypeStruct((T, 128), F32), jax.ShapeDtypeStruct((T, 128), BF16),
                   jax.ShapeDtypeStruct((T, 128), BF16), jax.ShapeDtypeStruct((T, 128), BF16),
                   jax.ShapeDtypeStruct((T, 128), F32), jax.ShapeDtypeStruct((T, C_Q_RANK), F32),
                   jax.ShapeDtypeStruct((T, C_KV_RANK), F32),
                   jax.ShapeDtypeStruct((512, T), BF16), jax.ShapeDtypeStruct((512, T), BF16),
                   jax.ShapeDtypeStruct((128, T), BF16), jax.ShapeDtypeStruct((128, T), BF16),
                   jax.ShapeDtypeStruct((128, T), BF16), jax.ShapeDtypeStruct((2 * _GATE_ROWS, T), F32),
                   jax.ShapeDtypeStruct((32, T), F32)],
        compiler_params=_cparams(("arbitrary",)),
        name="in_projection",
    )(x, nrm, w_row, w_t, c128, s128, c64t, s64t, c32t, s32t)
    return outs


def _gelu_tanh(x):
    return 0.5 * x * (1.0 + jnp.tanh(math.sqrt(2.0 / math.pi) * (x + 0.044715 * (x * x * x))))


def _cmp_kernel(seg_ref, pos_ref, w1_ref, w2_ref, w2t_ref, o_ref, ot_ref):
    seg = seg_ref[0, 0]
    n_seg = seg.shape[0]
    half = seg.shape[1]
    a = (seg + pos_ref[0, 0:1, :]).astype(BF16)
    b = (seg + pos_ref[0, 1:2, :]).astype(BF16)
    u = _dot(a, w1_ref[0, 0:half, :])
    v = _dot(b, w1_ref[0, half:2 * half, :])
    pre = u + pltpu.roll(v, n_seg - 1, 0)
    g = _gelu_tanh(pre).astype(BF16)
    o_ref[0, 0] = _dot(g, w2_ref[0]).astype(BF16)
    ot_ref[0, 0] = _dot_nt(w2t_ref[0], g).astype(BF16)


def _compress(kc, vc, pos_k, w1_k, w2_k, pos_v, w1_v, w2_v, B, S):
    G, dh = A_KV_GROUPS, HEAD_DIM
    n_seg = S // A_CMP_STRIDE
    seg_w = A_CMP_STRIDE * dh

    def segs(t):
        return t.reshape(B, S, G, dh).transpose(0, 2, 1, 3).reshape(B * G, n_seg, seg_w)

    seg = jnp.stack([segs(kc), segs(vc)])
    pos = jnp.stack([pos_k.reshape(2, seg_w), pos_v.reshape(2, seg_w)])
    w1 = jnp.stack([w1_k, w1_v]).astype(BF16)
    w2 = jnp.stack([w2_k, w2_v]).astype(BF16)
    w2t = jnp.stack([w2_k.T, w2_v.T]).astype(BF16)
    per = lambda shape: pl.BlockSpec((1,) + shape, lambda w, i: (w,) + (0,) * len(shape))
    o, ot = pl.pallas_call(
        _cmp_kernel,
        grid=(2, B * G),
        in_specs=[pl.BlockSpec((1, 1, n_seg, seg_w), lambda w, i: (w, i, 0, 0)),
                  per((2, seg_w)), per((2 * seg_w, A_CMP_HIDDEN)), per((A_CMP_HIDDEN, dh)),
                  per((dh, A_CMP_HIDDEN))],
        out_specs=[pl.BlockSpec((1, 1, n_seg, dh), lambda w, i: (w, i, 0, 0)),
                   pl.BlockSpec((1, 1, dh, n_seg), lambda w, i: (w, i, 0, 0))],
        out_shape=[jax.ShapeDtypeStruct((2, B * G, n_seg, dh), BF16),
                   jax.ShapeDtypeStruct((2, B * G, dh, n_seg), BF16)],
        compiler_params=_cparams(("arbitrary", "arbitrary")),
        name="nsa_compress",
    )(seg, pos, w1, w2, w2t)
    return o[0], ot[1]


def _lane_heads(blk, dh, n):
    return jnp.concatenate([blk[r * dh:(r + 1) * dh] for r in range(n)], axis=1)


def _nsa_kernel(q_ref, kc_ref, vct_ref, smt_ref, ks_ref, vst_ref, kw_ref, vwt_ref, g_ref,
                o_ref, sel_ref):
    QC, R, dh = A_Q_CHUNK, A_GROUP_HEADS, HEAD_DIM
    NQ = R * QC
    c = pl.program_id(2)
    t0 = c * QC
    qT = _lane_heads(q_ref[...], dh, R)
    tq = t0 + (lax.broadcasted_iota(jnp.int32, (1, NQ), 1) & (QC - 1))
    tq1 = t0 + lax.broadcasted_iota(jnp.int32, (1, QC), 1)

    n_seg = kc_ref.shape[1]
    s = _dot(kc_ref[0], qT)
    cend = lax.broadcasted_iota(jnp.int32, (n_seg, 1), 0) * A_CMP_STRIDE + (A_CMP_LEN - 1)
    cmask = cend <= tq
    s = jnp.where(cmask, s, NEG_INF)
    m = jnp.max(s, axis=0, keepdims=True)
    e = jnp.where(cmask, jnp.exp(s - m), 0.0)
    l = jnp.sum(e, axis=0, keepdims=True)
    p = e / jnp.where(l > 0.0, l, 1.0)
    o_cmp = _dot(vct_ref[0], p.astype(BF16))

    psum = p[:, 0:QC]
    for r in range(1, R):
        psum = psum + p[:, r * QC:(r + 1) * QC]
    hi = psum.astype(BF16)
    lo = (psum - hi.astype(F32)).astype(BF16)
    smt = smt_ref[...]
    imp = _dot(smt, hi) + _dot(smt, lo)
    n_sel = imp.shape[0]
    jid = lax.broadcasted_iota(jnp.int32, (n_sel, QC), 0).astype(F32)
    cur = (tq1 // A_SEL_LEN).astype(F32)
    valid = jid <= cur
    forced = (jid == 0.0) | (jid == cur) | (jid == cur - 1.0)
    val = jnp.where(valid, imp + jnp.where(forced, A_FORCE_BONUS, 0.0), NEG_INF)

    removed = -3.0e38
    sel = jnp.zeros((n_sel, QC), F32)
    for _ in range(min(A_TOPK, n_sel)):
        vmax = jnp.max(val, axis=0, keepdims=True)
        first = jnp.min(jnp.where(val == vmax, jid, float(n_sel)), axis=0, keepdims=True)
        hit = jid == first
        sel = jnp.where(hit, 1.0, sel)
        val = jnp.where(hit, removed, val)
    sel_ref[...] = sel

    KT = min(KV_TILE, ks_ref.shape[1])
    blocks_per_tile = KT // A_SEL_LEN
    kiota = lax.broadcasted_iota(jnp.int32, (KT, 1), 0)

    def sel_step(i, carry):
        m_i, l_i, acc = carry
        k0 = pl.multiple_of(i * KT, KT)
        s_i = _dot(ks_ref[0, pl.ds(k0, KT), :], qT)
        rows = [jnp.broadcast_to(sel_ref[pl.ds(i * blocks_per_tile + j, 1), :], (A_SEL_LEN, QC))
                for j in range(blocks_per_tile)]
        selm = jnp.concatenate(rows, axis=0)
        ok = (selm > 0.0) & ((k0 + kiota) <= tq1)
        bias = jnp.where(ok, 0.0, NEG_INF)
        s_i = s_i + jnp.concatenate([bias] * R, axis=1)
        m_new = jnp.maximum(m_i, jnp.max(s_i, axis=0, keepdims=True))
        alpha = jnp.exp(m_i - m_new)
        p_i = jnp.exp(s_i - m_new)
        l_new = alpha * l_i + jnp.sum(p_i, axis=0, keepdims=True)
        acc = alpha * acc + _dot(vst_ref[:, pl.ds(k0, KT)], p_i.astype(BF16))
        return m_new, l_new, acc

    n_tiles = (t0 + QC + KT - 1) // KT
    init = (jnp.full((1, NQ), NEG_INF, F32), jnp.zeros((1, NQ), F32), jnp.zeros((dh, NQ), F32))
    _, l_s, acc_s = lax.fori_loop(0, n_tiles, sel_step, init)
    o_sel = acc_s / l_s

    WL = A_WINDOW + QC
    start = pl.multiple_of(jnp.maximum(t0 - A_WINDOW, 0), QC)
    s_w = _dot(kw_ref[0, pl.ds(start, WL), :], qT)
    kp = start + lax.broadcasted_iota(jnp.int32, (WL, 1), 0)
    ok = (kp <= tq) & (kp > tq - A_WINDOW)
    s_w = jnp.where(ok, s_w, NEG_INF)
    m_w = jnp.max(s_w, axis=0, keepdims=True)
    p_w = jnp.exp(s_w - m_w)
    l_w = jnp.sum(p_w, axis=0, keepdims=True)
    o_win = _dot(vwt_ref[:, pl.ds(start, WL)], p_w.astype(BF16)) / l_w

    gt = g_ref[...]
    outs = []
    for r in range(R):
        sl = slice(r * QC, (r + 1) * QC)
        outs.append(gt[3 * r:3 * r + 1] * o_cmp[:, sl] + gt[3 * r + 1:3 * r + 2] * o_sel[:, sl]
                    + gt[3 * r + 2:3 * r + 3] * o_win[:, sl])
    o_ref[...] = jnp.concatenate(outs, axis=0).astype(o_ref.dtype)


def _sel_map_t(S):
    n_cmp = (S - A_CMP_LEN) // A_CMP_STRIDE + 1
    n_seg = S // A_CMP_STRIDE
    n_sel = S // A_SEL_LEN
    tok = np.arange(n_cmp)[:, None] * A_CMP_STRIDE + np.arange(A_CMP_LEN)[None, :]
    sm = np.zeros((n_seg, n_sel), np.float32)
    np.add.at(sm, (np.repeat(np.arange(n_cmp), A_CMP_LEN), (tok // A_SEL_LEN).reshape(-1)), 1.0 / A_CMP_LEN)
    return jnp.asarray(sm.T, BF16)


def _group_rows(t, B, S):
    return t.reshape(B, S, A_KV_GROUPS, HEAD_DIM).transpose(0, 2, 1, 3).reshape(B * A_KV_GROUPS, S, HEAD_DIM)


def _nsa(aqt, k_cmp, v_cmp_t, ks, vst, kw, vwt, gt, B, S):
    G, R, dh, QC = A_KV_GROUPS, A_GROUP_HEADS, HEAD_DIM, A_Q_CHUNK
    T = B * S
    nq = S // QC
    n_seg = S // A_CMP_STRIDE
    n_sel = S // A_SEL_LEN
    smt = _sel_map_t(S)
    qspec = pl.BlockSpec((R * dh, QC), lambda b, g, c: (g, b * nq + c))
    return pl.pallas_call(
        _nsa_kernel,
        grid=(B, G, nq),
        in_specs=[qspec,
                  pl.BlockSpec((1, n_seg, dh), lambda b, g, c: (b * G + g, 0, 0)),
                  pl.BlockSpec((1, dh, n_seg), lambda b, g, c: (b * G + g, 0, 0)),
                  pl.BlockSpec((n_sel, n_seg), lambda b, g, c: (0, 0)),
                  pl.BlockSpec((1, S, dh), lambda b, g, c: (b * G + g, 0, 0)),
                  pl.BlockSpec((dh, S), lambda b, g, c: (g, b)),
                  pl.BlockSpec((1, S, dh), lambda b, g, c: (b * G + g, 0, 0)),
                  pl.BlockSpec((dh, S), lambda b, g, c: (g, b)),
                  pl.BlockSpec((_GATE_ROWS, QC), lambda b, g, c: (g, b * nq + c))],
        out_specs=qspec,
        out_shape=jax.ShapeDtypeStruct((G * R * dh, T), BF16),
        scratch_shapes=[pltpu.VMEM((n_sel, QC), F32)],
        compiler_params=_cparams(("arbitrary", "arbitrary", "arbitrary")),
        name="nsa_attention",
    )(aqt, k_cmp, v_cmp_t, smt, ks, vst, kw, vwt, gt)


def _swa_kernel(q_ref, kp_ref, kc_ref, vpt_ref, vct_ref, sink_ref, o_ref):
    BLK, R, dh = B_BLOCK, B_HEADS // B_KV_HEADS, HEAD_DIM
    NQ = R * BLK
    n = pl.program_id(2)
    qT = _lane_heads(q_ref[...], dh, R)
    k = jnp.concatenate([kp_ref[0], kc_ref[0]], axis=0)
    s = _dot(k, qT)
    kj = lax.broadcasted_iota(jnp.int32, (2 * BLK, 1), 0) - BLK
    qi = lax.broadcasted_iota(jnp.int32, (1, NQ), 1) & (BLK - 1)
    ok = (kj <= qi) & (kj > qi - B_WINDOW) & ((n > 0) | (kj >= 0))
    s = jnp.where(ok, s, NEG_INF)
    sink = sink_ref[0]
    m = jnp.maximum(jnp.max(s, axis=0, keepdims=True), sink)
    p = jnp.exp(s - m)
    l = jnp.sum(p, axis=0, keepdims=True) + jnp.exp(sink - m)
    vT = jnp.concatenate([vpt_ref[...], vct_ref[...]], axis=1)
    o = _dot(vT, p.astype(BF16)) / l
    o_ref[...] = jnp.concatenate([o[:, r * BLK:(r + 1) * BLK] for r in range(R)], axis=0).astype(o_ref.dtype)


def _swa(bqt, bk, bvt, sinks, B, S):
    G, R, dh, BLK = B_KV_HEADS, B_HEADS // B_KV_HEADS, HEAD_DIM, B_BLOCK
    T = B * S
    nb = S // BLK
    sink_cols = jnp.repeat(sinks.astype(F32).reshape(G, R), BLK, axis=1).reshape(G, 1, R * BLK)
    prev = lambda n: jnp.maximum(n - 1, 0)
    qspec = pl.BlockSpec((R * dh, BLK), lambda b, g, n: (g, b * nb + n))
    return pl.pallas_call(
        _swa_kernel,
        grid=(B, G, nb),
        in_specs=[qspec,
                  pl.BlockSpec((1, BLK, dh), lambda b, g, n: (b * G + g, prev(n), 0)),
                  pl.BlockSpec((1, BLK, dh), lambda b, g, n: (b * G + g, n, 0)),
                  pl.BlockSpec((dh, BLK), lambda b, g, n: (g, b * nb + prev(n))),
                  pl.BlockSpec((dh, BLK), lambda b, g, n: (g, b * nb + n)),
                  pl.BlockSpec((1, 1, R * BLK), lambda b, g, n: (g, 0, 0))],
        out_specs=qspec,
        out_shape=jax.ShapeDtypeStruct((G * R * dh, T), BF16),
        compiler_params=_cparams(("arbitrary", "arbitrary", "arbitrary")),
        name="swa_sink_attention",
    )(bqt, bk, bk, bvt, bvt, sink_cols)


def _mla_proj_kernel(cq_ref, ckv_ref, kpet_ref, qn_ref, kvn_ref, wqt_ref, wkvt_ref, c_ref, s_ref,
                     qf_ref, kf_ref, vt_ref):
    hq = _rms(cq_ref[...], qn_ref[...]).astype(BF16)
    hkv = _rms(ckv_ref[...], kvn_ref[...]).astype(BF16)
    qt = _dot_nt(wqt_ref[...], hq)
    kvt = _dot_nt(wkvt_ref[...], hkv)
    c = c_ref[...]
    s = s_ref[...]
    kpe = kpet_ref[...].astype(BF16)
    scale = (C_NOPE + C_ROPE) ** -0.5
    half = C_ROPE // 2
    tm = qt.shape[1]
    zpad = jnp.zeros((C_QK_PAD - C_NOPE - C_ROPE, tm), BF16)
    for h in range(C_HEADS):
        b = h * C_QK_PAD
        qf_ref[h, 0:C_NOPE, :] = (qt[b:b + C_NOPE] * scale).astype(BF16)
        x1 = qt[b + C_NOPE:b + C_NOPE + half]
        x2 = qt[b + C_NOPE + half:b + C_NOPE + C_ROPE]
        qf_ref[h, C_NOPE:C_NOPE + half, :] = ((x1 * c - x2 * s) * scale).astype(BF16)
        qf_ref[h, C_NOPE + half:C_NOPE + C_ROPE, :] = ((x2 * c + x1 * s) * scale).astype(BF16)
        qf_ref[h, C_NOPE + C_ROPE:C_QK_PAD, :] = zpad
        kf_ref[h, 0:C_NOPE, :] = kvt[b:b + C_NOPE].astype(BF16)
        kf_ref[h, C_NOPE:C_NOPE + C_ROPE, :] = kpe
        kf_ref[h, C_NOPE + C_ROPE:C_QK_PAD, :] = zpad
        vt_ref[h] = kvt[b + C_NOPE:b + C_NOPE + C_V].astype(BF16)


def _mla_proj(cq, ckv, kpet, q_norm, w_q_up, kv_norm, w_kv_up, c32t, s32t, tm):
    T = cq.shape[0]
    H = C_HEADS
    wq = w_q_up.reshape(C_Q_RANK, H, C_NOPE + C_ROPE)
    wq = jnp.pad(wq, ((0, 0), (0, 0), (0, C_QK_PAD - C_NOPE - C_ROPE))).reshape(C_Q_RANK, H * C_QK_PAD)
    wqt = wq.T.astype(BF16)
    wkvt = w_kv_up.T.astype(BF16)
    full = lambda a: pl.BlockSpec(a.shape, lambda i: (0,) * a.ndim)
    qn = q_norm.reshape(1, C_Q_RANK)
    kvn = kv_norm.reshape(1, C_KV_RANK)
    return pl.pallas_call(
        _mla_proj_kernel,
        grid=(T // tm,),
        in_specs=[pl.BlockSpec((tm, C_Q_RANK), lambda i: (i, 0)),
                  pl.BlockSpec((tm, C_KV_RANK), lambda i: (i, 0)),
                  pl.BlockSpec((C_ROPE, tm), lambda i: (0, i)),
                  full(qn), full(kvn), full(wqt), full(wkvt),
                  pl.BlockSpec((C_ROPE // 2, tm), lambda i: (0, i)),
                  pl.BlockSpec((C_ROPE // 2, tm), lambda i: (0, i))],
        out_specs=[pl.BlockSpec((H, C_QK_PAD, tm), lambda i: (0, 0, i)),
                   pl.BlockSpec((H, C_QK_PAD, tm), lambda i: (0, 0, i)),
                   pl.BlockSpec((H, C_V, tm), lambda i: (0, 0, i))],
        out_shape=[jax.ShapeDtypeStruct((H, C_QK_PAD, T), BF16),
                   jax.ShapeDtypeStruct((H, C_QK_PAD, T), BF16),
                   jax.ShapeDtypeStruct((H, C_V, T), BF16)],
        compiler_params=_cparams(("arbitrary",)),
        name="mla_projection",
    )(cq, ckv, kpet, qn, kvn, wqt, wkvt, c32t, s32t)


def _mla_kernel(q_ref, kf_ref, vt_ref, o_ref):
    QT = q_ref.shape[2]
    KT = QT
    qi = pl.program_id(2)
    qT = q_ref[0]
    tq = qi * QT + lax.broadcasted_iota(jnp.int32, (1, QT), 1)
    kiota = lax.broadcasted_iota(jnp.int32, (KT, 1), 0)

    def step(i, carry, masked):
        m_i, l_i, acc = carry
        k0 = pl.multiple_of(i * KT, KT)
        s = _dot(kf_ref[0, pl.ds(k0, KT), :], qT)
        if masked:
            s = jnp.where((k0 + kiota) <= tq, s, NEG_INF)
        m_new = jnp.maximum(m_i, jnp.max(s, axis=0, keepdims=True))
        alpha = jnp.exp(m_i - m_new)
        p = jnp.exp(s - m_new)
        l_new = alpha * l_i + jnp.sum(p, axis=0, keepdims=True)
        acc = alpha * acc + _dot(vt_ref[0, :, pl.ds(k0, KT)], p.astype(BF16))
        return m_new, l_new, acc

    init = (jnp.full((1, QT), NEG_INF, F32), jnp.zeros((1, QT), F32), jnp.zeros((C_V, QT), F32))
    carry = lax.fori_loop(0, qi, functools.partial(step, masked=False), init)
    _, l, acc = step(qi, carry, True)
    o_ref[...] = (acc / l).astype(o_ref.dtype)


def _mla(qft, kf, vt, B, S):
    H = C_HEADS
    T = B * S
    QT = min(KV_TILE, S)
    nq = S // QT
    return pl.pallas_call(
        _mla_kernel,
        grid=(B, H, nq),
        in_specs=[pl.BlockSpec((1, C_QK_PAD, QT), lambda b, h, q: (h, 0, b * nq + q)),
                  pl.BlockSpec((1, S, C_QK_PAD), lambda b, h, q: (h, b, 0)),
                  pl.BlockSpec((1, C_V, S), lambda b, h, q: (h, 0, b))],
        out_specs=pl.BlockSpec((C_V, QT), lambda b, h, q: (h, b * nq + q)),
        out_shape=jax.ShapeDtypeStruct((H * C_V, T), BF16),
        compiler_params=_cparams(("arbitrary", "arbitrary", "arbitrary")),
        name="mla_attention",
    )(qft, kf, vt)


def _merge_kernel(x_ref, oa_ref, ob_ref, oc_ref, nrm_ref, wg_ref, wa_ref, wb_ref, wc_ref, wo_ref, o_ref):
    x = x_ref[...]
    h = _rms(x, nrm_ref[...]).astype(BF16)
    g = _sigmoid(_dot(h, wg_ref[...]))
    D = D_MODEL
    m = (g[:, 0:D] * _dot(oa_ref[...], wa_ref[...]) + g[:, D:2 * D] * _dot(ob_ref[...], wb_ref[...])
         + g[:, 2 * D:3 * D] * _dot(oc_ref[...], wc_ref[...]))
    o_ref[...] = x + _dot(m.astype(BF16), wo_ref[...])


def _merge(x, oa, ob, oc, mix_norm, w_gate, w_a, w_b, w_c, w_out, tm):
    T = x.shape[0]
    row = lambda n: pl.BlockSpec((tm, n), lambda i: (i, 0))
    full = lambda a: pl.BlockSpec(a.shape, lambda i: (0,) * a.ndim)
    nrm = mix_norm.reshape(1, D_MODEL)
    ws = [w.astype(BF16) for w in (w_gate, w_a, w_b, w_c, w_out)]
    return pl.pallas_call(
        _merge_kernel,
        grid=(T // tm,),
        in_specs=[row(D_MODEL), row(512), row(512), row(512), full(nrm)] + [full(w) for w in ws],
        out_specs=row(D_MODEL),
        out_shape=jax.ShapeDtypeStruct((T, D_MODEL), F32),
        compiler_params=_cparams(("arbitrary",)),
        name="branch_merge",
    )(x, oa, ob, oc, nrm, *ws)


def _ffn_kernel(x_ref, p_ref, fn_ref, wg_ref, wu_ref, wd_ref, pn_ref, wpg_ref, wpe_ref, fin_ref, o_ref,
                *, final):
    x = x_ref[...]
    h = _rms(x, fn_ref[...]).astype(BF16)
    a = _dot(h, wg_ref[...])
    u = _dot(h, wu_ref[...])
    y = (a * _sigmoid(a) * u).astype(BF16)
    x = x + _dot(y, wd_ref[...])
    gate = _sigmoid(_dot(_rms(x, pn_ref[...]).astype(BF16), wpg_ref[...]))
    x = x + gate * _dot(p_ref[...].astype(BF16), wpe_ref[...])
    if final:
        x = _rms(x, fin_ref[...])
    o_ref[...] = x


def _ffn_ple(x, p_i, ffn_norm, w_gate, w_up, w_down, ple_norm, w_ple_gate, w_ple_proj, final_norm, final, tm):
    T = x.shape[0]
    row = lambda n: pl.BlockSpec((tm, n), lambda i: (i, 0))
    full = lambda a: pl.BlockSpec(a.shape, lambda i: (0,) * a.ndim)
    fn = ffn_norm.reshape(1, D_MODEL)
    pn = ple_norm.reshape(1, D_MODEL)
    fin = final_norm.reshape(1, D_MODEL)
    wg, wu, wd, wpg, wpe = [w.astype(BF16) for w in (w_gate, w_up, w_down, w_ple_gate, w_ple_proj)]
    return pl.pallas_call(
        functools.partial(_ffn_kernel, final=final),
        grid=(T // tm,),
        in_specs=[row(D_MODEL), row(PLE_DIM), full(fn), full(wg), full(wu), full(wd), full(pn),
                  full(wpg), full(wpe), full(fin)],
        out_specs=row(D_MODEL),
        out_shape=jax.ShapeDtypeStruct((T, D_MODEL), F32),
        compiler_params=_cparams(("arbitrary",)),
        name="ffn_ple",
    )(x, p_i, fn, wg, wu, wd, pn, wpg, wpe, fin)


def _layer(x, p_i, tables, B, S, final_norm, final, mix_norm, w_in, a_cmp_pos_k, a_cmp_w1_k, a_cmp_w2_k,
           a_cmp_pos_v, a_cmp_w1_v, a_cmp_w2_v, b_sinks, c_q_norm, c_w_q_up, c_kv_norm, c_w_kv_up,
           w_branch_gate, w_branch_a, w_branch_b, w_branch_c, w_out, ffn_norm, w_ffn_gate, w_ffn_up,
           w_ffn_down, ple_norm, w_ple_proj, w_ple_gate):
    T = B * S
    tm = min(512, T)
    (kc, ks, kw, bk, vc, cq, ckv, aqt, bqt, vst, vwt, bvt, gt, kpet) = _inproj(x, mix_norm, w_in, tables, tm)

    k_cmp, v_cmp_t = _compress(kc, vc, a_cmp_pos_k, a_cmp_w1_k, a_cmp_w2_k,
                               a_cmp_pos_v, a_cmp_w1_v, a_cmp_w2_v, B, S)
    oat = _nsa(aqt, k_cmp, v_cmp_t, _group_rows(ks, B, S), vst, _group_rows(kw, B, S), vwt, gt, B, S)
    obt = _swa(bqt, _group_rows(bk, B, S), bvt, b_sinks, B, S)

    qft, kft, vt = _mla_proj(cq, ckv, kpet, c_q_norm, c_w_q_up, c_kv_norm, c_w_kv_up,
                             tables[4], tables[5], tm)
    oct = _mla(qft, kft.transpose(0, 2, 1), vt, B, S)

    x = _merge(x, oat.T, obt.T, oct.T, mix_norm, w_branch_gate, w_branch_a, w_branch_b, w_branch_c,
               w_out, min(256, T))
    return _ffn_ple(x, p_i, ffn_norm, w_ffn_gate, w_ffn_up, w_ffn_down, ple_norm, w_ple_gate, w_ple_proj,
                    final_norm, final, min(256, T))


def kernel(x, p, positions, mix_norm, w_in, a_cmp_pos_k, a_cmp_w1_k, a_cmp_w2_k, a_cmp_pos_v, a_cmp_w1_v, a_cmp_w2_v, b_sinks, c_q_norm, c_w_q_up, c_kv_norm, c_w_kv_up, w_branch_gate, w_branch_a, w_branch_b, w_branch_c, w_out, ffn_norm, w_ffn_gate, w_ffn_up, w_ffn_down, ple_norm, w_ple_proj, w_ple_gate, final_norm):
    B, S, D = x.shape
    depth = p.shape[0]
    T = B * S
    tables = _rope_tables(positions, min(512, T))
    xf = x.reshape(T, D)
    per_layer = (mix_norm, w_in, a_cmp_pos_k, a_cmp_w1_k, a_cmp_w2_k, a_cmp_pos_v, a_cmp_w1_v, a_cmp_w2_v,
                 b_sinks, c_q_norm, c_w_q_up, c_kv_norm, c_w_kv_up, w_branch_gate, w_branch_a, w_branch_b,
                 w_branch_c, w_out, ffn_norm, w_ffn_gate, w_ffn_up, w_ffn_down, ple_norm, w_ple_proj,
                 w_ple_gate)
    for i in range(depth):
        xf = _layer(xf, p[i].reshape(T, PLE_DIM), tables, B, S, final_norm, i == depth - 1,
                    *[w[i] for w in per_layer])
    return xf.reshape(B, S, D)
```

```python
import functools
import math

import numpy as np
import jax
import jax.numpy as jnp
from jax import lax
from jax.experimental import pallas as pl
from jax.experimental.pallas import tpu as pltpu

F32 = jnp.float32
BF16 = jnp.bfloat16

D_MODEL = 1024
PLE_DIM = 256
ROPE_THETA = 10000.0
EPS = 1e-6
NEG_INF = -1e30
HEAD_DIM = 64

A_HEADS = 8
A_KV_GROUPS = 2
A_GROUP_HEADS = A_HEADS // A_KV_GROUPS
A_CMP_LEN = 32
A_CMP_STRIDE = 16
A_CMP_HIDDEN = 256
A_SEL_LEN = 64
A_TOPK = 16
A_WINDOW = 512
A_Q_CHUNK = 128
A_FORCE_BONUS = 1e4

B_HEADS = 8
B_KV_HEADS = 2
B_WINDOW = 128
B_BLOCK = 128

C_HEADS = 8
C_Q_RANK = 256
C_KV_RANK = 256
C_NOPE = 64
C_ROPE = 32
C_V = 64
C_QK_PAD = 128

D_FF = int(math.ceil(8 * D_MODEL / 3 / 256)) * 256

LANES = 128
KV_TILE = 512
LOG2E = math.log2(math.e)
V_AUG = 16
K_AUG = 64
K_PAD_LANE = 8
BLOCKS_PER_TILE = KV_TILE // A_SEL_LEN
VMEM_LIMIT = 56 * 1024 * 1024

_ROW_K = 4 * 128
_ROW_COLS = 2 * _ROW_K + 128 + C_Q_RANK + C_KV_RANK
_T_AQ, _T_BQ = 0, 512
_T_VS, _T_VW, _T_BV = 1024, 1152, 1280
_T_G, _T_KPE = 1408, 1440
_T_ROWS = 1472
_GATE_ROWS = 16


def _cparams(sem):
    return pltpu.CompilerParams(dimension_semantics=sem, vmem_limit_bytes=VMEM_LIMIT)


def _rms(x, g):
    return x * lax.rsqrt(jnp.mean(x * x, axis=-1, keepdims=True) + EPS) * g


def _sigmoid(x):
    return 1.0 / (1.0 + jnp.exp(-x))


def _dot(a, b):
    return jnp.dot(a, b, preferred_element_type=F32)


def _dot_nt(a, b):
    return lax.dot_general(a, b, (((1,), (1,)), ((), ())), preferred_element_type=F32)


def _rope_kernel(pc_ref, pr_ref, i128_ref, i64_ref, i32_ref,
                 c128_ref, s128_ref, c64t_ref, s64t_ref, c32t_ref, s32t_ref):
    ang = pc_ref[...].astype(F32) * i128_ref[...]
    c128_ref[...] = jnp.cos(ang)
    s128_ref[...] = jnp.sin(ang)
    pr = pr_ref[...].astype(F32)
    a64 = i64_ref[...] * pr
    c64t_ref[...] = jnp.cos(a64)
    s64t_ref[...] = jnp.sin(a64)
    a32 = i32_ref[...] * pr
    c32t_ref[...] = jnp.cos(a32)
    s32t_ref[...] = jnp.sin(a32)


def _rope_tables(positions, tm):
    T = positions.size
    pos = positions.reshape(T)
    inv64 = jnp.power(F32(ROPE_THETA), -jnp.arange(0, HEAD_DIM, 2, dtype=F32) / HEAD_DIM)
    inv32 = jnp.power(F32(ROPE_THETA), -jnp.arange(0, C_ROPE, 2, dtype=F32) / C_ROPE)
    inv128 = jnp.tile(inv64, 4).reshape(1, 128)
    full = lambda shape: pl.BlockSpec(shape, lambda i: (0,) * len(shape))
    return pl.pallas_call(
        _rope_kernel,
        grid=(T // tm,),
        in_specs=[pl.BlockSpec((tm, 1), lambda i: (i, 0)),
                  pl.BlockSpec((1, tm), lambda i: (0, i)),
                  full((1, 128)), full((32, 1)), full((16, 1))],
        out_specs=[pl.BlockSpec((tm, 128), lambda i: (i, 0)),
                   pl.BlockSpec((tm, 128), lambda i: (i, 0)),
                   pl.BlockSpec((32, tm), lambda i: (0, i)),
                   pl.BlockSpec((32, tm), lambda i: (0, i)),
                   pl.BlockSpec((16, tm), lambda i: (0, i)),
                   pl.BlockSpec((16, tm), lambda i: (0, i))],
        out_shape=[jax.ShapeDtypeStruct((T, 128), F32), jax.ShapeDtypeStruct((T, 128), F32),
                   jax.ShapeDtypeStruct((32, T), F32), jax.ShapeDtypeStruct((32, T), F32),
                   jax.ShapeDtypeStruct((16, T), F32), jax.ShapeDtypeStruct((16, T), F32)],
        compiler_params=_cparams(("arbitrary",)),
        name="rope_tables",
    )(pos.reshape(T, 1), pos.reshape(1, T), inv128, inv64.reshape(32, 1), inv32.reshape(16, 1))


def _store_v_aug(ref, vt, n_heads):
    dv = vt.shape[0] // n_heads
    tm = vt.shape[1]
    ones_row = (lax.broadcasted_iota(jnp.int32, (V_AUG, tm), 0) == 0).astype(ref.dtype)
    for g in range(n_heads):
        r0 = g * (dv + V_AUG)
        ref[r0:r0 + dv, :] = vt[g * dv:(g + 1) * dv].astype(ref.dtype)
        ref[r0 + dv:r0 + dv + V_AUG, :] = ones_row

def _inproj_kernel(x_ref, nrm_ref, wr_ref, wt_ref, c128_ref, s128_ref, c64t_ref, s64t_ref,
                   c32t_ref, s32t_ref,
                   kc_ref, ks_ref, kw_ref, bk_ref, vc_ref, cq_ref, ckv_ref,
                   aqt_ref, bqt_ref, vst_ref, vwt_ref, bvt_ref, gt_ref, kpet_ref):
    h = _rms(x_ref[...], nrm_ref[...]).astype(BF16)
    zr = _dot(h, wr_ref[...])
    c = c128_ref[...]
    s = s128_ref[...]
    for i, ref in enumerate((kc_ref, ks_ref, kw_ref, bk_ref)):
        v = zr[:, 128 * i:128 * (i + 1)] * c + zr[:, _ROW_K + 128 * i:_ROW_K + 128 * (i + 1)] * s
        ref[...] = v.astype(ref.dtype)
    o = 2 * _ROW_K
    vc_ref[...] = zr[:, o:o + 128]
    cq_ref[...] = zr[:, o + 128:o + 128 + C_Q_RANK]
    ckv_ref[...] = zr[:, o + 128 + C_Q_RANK:o + 128 + C_Q_RANK + C_KV_RANK]

    zt = _dot_nt(wt_ref[...], h)
    ct = c64t_ref[...]
    st = s64t_ref[...]
    qscale = HEAD_DIM ** -0.5 * LOG2E
    for q_ref, base in ((aqt_ref, _T_AQ), (bqt_ref, _T_BQ)):
        for hh in range(8):
            r0 = base + hh * HEAD_DIM
            x1 = zt[r0:r0 + 32]
            x2 = zt[r0 + 32:r0 + 64]
            q_ref[hh * 64:hh * 64 + 32, :] = ((x1 * ct - x2 * st) * qscale).astype(BF16)
            q_ref[hh * 64 + 32:hh * 64 + 64, :] = ((x2 * ct + x1 * st) * qscale).astype(BF16)
    _store_v_aug(vst_ref, zt[_T_VS:_T_VS + 128], A_KV_GROUPS)
    _store_v_aug(vwt_ref, zt[_T_VW:_T_VW + 128], A_KV_GROUPS)
    _store_v_aug(bvt_ref, zt[_T_BV:_T_BV + 128], B_KV_HEADS)
    gt_ref[...] = _sigmoid(zt[_T_G:_T_G + 2 * _GATE_ROWS])
    c3 = c32t_ref[...]
    s3 = s32t_ref[...]
    x1 = zt[_T_KPE:_T_KPE + 16]
    x2 = zt[_T_KPE + 16:_T_KPE + 32]
    kpet_ref[0:16, :] = x1 * c3 - x2 * s3
    kpet_ref[16:32, :] = x2 * c3 + x1 * s3


def _rot_cols(w, dim):
    k, n = w.shape
    w4 = w.reshape(k, n // dim, 2, dim // 2)
    return jnp.concatenate([-w4[:, :, 1:2], w4[:, :, 0:1]], axis=2).reshape(k, n)


def _inproj(x, mix_norm, w_in, tables, tm):
    T = x.shape[0]
    c128, s128, c64t, s64t, c32t, s32t = tables
    sizes = (512, 128, 128, 128, 128, 128, 128, 24, 512, 128, 128, 256, 256, 32)
    offs = np.concatenate([[0], np.cumsum(sizes)])
    (w_aq, w_kc, w_vc, w_ks, w_vs, w_kw, w_vw, w_g, w_bq, w_bk, w_bv, w_cq, w_ckv, w_kpe) = [
        w_in[:, int(offs[i]):int(offs[i + 1])] for i in range(len(sizes))]
    kmain = jnp.concatenate([w_kc, w_ks, w_kw, w_bk], axis=1)
    w_row = jnp.concatenate([kmain, _rot_cols(kmain, HEAD_DIM), w_vc, w_cq, w_ckv], axis=1).astype(BF16)
    wg = w_g.reshape(D_MODEL, A_KV_GROUPS, A_GROUP_HEADS * 3)
    wg = jnp.pad(wg, ((0, 0), (0, 0), (0, _GATE_ROWS - A_GROUP_HEADS * 3))).reshape(D_MODEL, 2 * _GATE_ROWS)
    w_t = jnp.concatenate([w_aq, w_bq, w_vs, w_vw, w_bv, wg, w_kpe], axis=1).T.astype(BF16)

    row = lambda n: pl.BlockSpec((tm, n), lambda i: (i, 0))
    colT = lambda n: pl.BlockSpec((n, tm), lambda i: (0, i))
    full = lambda a: pl.BlockSpec(a.shape, lambda i: (0,) * a.ndim)
    nrm = mix_norm.reshape(1, D_MODEL)
    vrows = A_KV_GROUPS * (HEAD_DIM + V_AUG)
    outs = pl.pallas_call(
        _inproj_kernel,
        grid=(T // tm,),
        in_specs=[row(D_MODEL), full(nrm), full(w_row), full(w_t), row(128), row(128),
                  colT(32), colT(32), colT(16), colT(16)],
        out_specs=[row(128), row(128), row(128), row(128), row(128), row(C_Q_RANK), row(C_KV_RANK),
                   colT(512), colT(512), colT(vrows), colT(vrows), colT(vrows), colT(2 * _GATE_ROWS), colT(32)],
        out_shape=[jax.ShapeDtypeStruct((T, 128), F32), jax.ShapeDtypeStruct((T, 128), BF16),
                   jax.ShapeDtypeStruct((T, 128), BF16), jax.ShapeDtypeStruct((T, 128), BF16),
                   jax.ShapeDtypeStruct((T, 128), F32), jax.ShapeDtypeStruct((T, C_Q_RANK), F32),
                   jax.ShapeDtypeStruct((T, C_KV_RANK), F32),
                   jax.ShapeDtypeStruct((512, T), BF16), jax.ShapeDtypeStruct((512, T), BF16),
                   jax.ShapeDtypeStruct((vrows, T), BF16), jax.ShapeDtypeStruct((vrows, T), BF16),
                   jax.ShapeDtypeStruct((vrows, T), BF16), jax.ShapeDtypeStruct((2 * _GATE_ROWS, T), F32),
                   jax.ShapeDtypeStruct((32, T), F32)],
        compiler_params=_cparams(("arbitrary",)),
        name="in_projection",
    )(x, nrm, w_row, w_t, c128, s128, c64t, s64t, c32t, s32t)
    return outs


def _gelu_tanh(x):
    return 0.5 * x * (1.0 + jnp.tanh(math.sqrt(2.0 / math.pi) * (x + 0.044715 * (x * x * x))))


def _cmp_kernel(seg_ref, pos_ref, w1_ref, w2_ref, w2t_ref, o_ref, ot_ref):
    seg = seg_ref[0, 0]
    n_seg = seg.shape[0]
    half = seg.shape[1]
    a = (seg + pos_ref[0, 0:1, :]).astype(BF16)
    b = (seg + pos_ref[0, 1:2, :]).astype(BF16)
    u = _dot(a, w1_ref[0, 0:half, :])
    v = _dot(b, w1_ref[0, half:2 * half, :])
    pre = u + pltpu.roll(v, n_seg - 1, 0)
    g = _gelu_tanh(pre).astype(BF16)
    o_ref[0, 0] = _dot(g, w2_ref[0]).astype(BF16)
    ot_ref[0, 0] = _dot_nt(w2t_ref[0], g).astype(BF16)


def _compress(kc, vc, pos_k, w1_k, w2_k, pos_v, w1_v, w2_v, B, S):
    G, dh = A_KV_GROUPS, HEAD_DIM
    n_seg = S // A_CMP_STRIDE
    seg_w = A_CMP_STRIDE * dh

    def segs(t):
        return t.reshape(B, S, G, dh).transpose(0, 2, 1, 3).reshape(B * G, n_seg, seg_w)

    seg = jnp.stack([segs(kc), segs(vc)])
    pos = jnp.stack([pos_k.reshape(2, seg_w), pos_v.reshape(2, seg_w)])
    w1 = jnp.stack([w1_k, w1_v]).astype(BF16)
    w2 = jnp.stack([w2_k, w2_v]).astype(BF16)
    w2t = jnp.stack([w2_k.T, w2_v.T]).astype(BF16)
    per = lambda shape: pl.BlockSpec((1,) + shape, lambda w, i: (w,) + (0,) * len(shape))
    o, ot = pl.pallas_call(
        _cmp_kernel,
        grid=(2, B * G),
        in_specs=[pl.BlockSpec((1, 1, n_seg, seg_w), lambda w, i: (w, i, 0, 0)),
                  per((2, seg_w)), per((2 * seg_w, A_CMP_HIDDEN)), per((A_CMP_HIDDEN, dh)),
                  per((dh, A_CMP_HIDDEN))],
        out_specs=[pl.BlockSpec((1, 1, n_seg, dh), lambda w, i: (w, i, 0, 0)),
                   pl.BlockSpec((1, 1, dh, n_seg), lambda w, i: (w, i, 0, 0))],
        out_shape=[jax.ShapeDtypeStruct((2, B * G, n_seg, dh), BF16),
                   jax.ShapeDtypeStruct((2, B * G, dh, n_seg), BF16)],
        compiler_params=_cparams(("arbitrary", "arbitrary")),
        name="nsa_compress",
    )(seg, pos, w1, w2, w2t)
    return o[0], ot[1]


def _lane_heads(blk, dh, n):
    return jnp.concatenate([blk[r * dh:(r + 1) * dh] for r in range(n)], axis=1)


def _nsa_select(g, q_ref, kc_ref, vct_ref, smt_ref, selb_ref, qa_ref, tq, tq1):
    QC, R, dh = A_Q_CHUNK, A_GROUP_HEADS, HEAD_DIM
    NQ = R * QC
    qT = _lane_heads(q_ref[g * R * dh:(g + 1) * R * dh, :], dh, R)
    qa_ref[g, 0:dh, :] = qT
    qa_ref[g, dh + 16:dh + K_AUG, :] = jnp.zeros((K_AUG - 16, NQ), BF16)

    n_seg = kc_ref.shape[1]
    s = _dot(kc_ref[g], qT)
    cend = lax.broadcasted_iota(jnp.int32, (n_seg, 1), 0) * A_CMP_STRIDE + (A_CMP_LEN - 1)
    cmask = cend <= tq
    s = jnp.where(cmask, s, NEG_INF)
    m = jnp.max(s, axis=0, keepdims=True)
    e = jnp.where(cmask, jnp.exp2(s - m), 0.0)
    l = jnp.sum(e, axis=0, keepdims=True)
    p = e / jnp.where(l > 0.0, l, 1.0)
    o_cmp = _dot(vct_ref[g], p.astype(BF16))

    psum = p[:, 0:QC]
    for r in range(1, R):
        psum = psum + p[:, r * QC:(r + 1) * QC]
    hi = psum.astype(BF16)
    lo = (psum - hi.astype(F32)).astype(BF16)
    smt = smt_ref[...]
    imp = _dot(smt, hi) + _dot(smt, lo)
    n_sel = imp.shape[0]
    jid = lax.broadcasted_iota(jnp.int32, (n_sel, QC), 0).astype(F32)
    cur = (tq1 // A_SEL_LEN).astype(F32)
    valid = jid <= cur
    forced = (jid == 0.0) | (jid == cur) | (jid == cur - 1.0)

    removed = -3.0e38
    val = jnp.where(valid & jnp.logical_not(forced), imp, removed)
    sel = forced
    for _ in range(min(A_TOPK, n_sel) - 3):
        vmax = jnp.max(val, axis=0, keepdims=True)
        first = jnp.min(jnp.where(val == vmax, jid, float(n_sel)), axis=0, keepdims=True)
        hit = jid == first
        sel = sel | hit
        val = jnp.where(hit, removed, val)
    selb_ref[g] = jnp.where(sel & valid, 0.0, NEG_INF)
    return o_cmp


def _nsa_kernel(q_ref, kc_ref, vct_ref, smt_ref, ks_ref, vst_ref, kw_ref, vwt_ref, g_ref,
                o_ref, selb_ref, qa_ref, sa_ref, sb_ref):
    QC, R, dh, G = A_Q_CHUNK, A_GROUP_HEADS, HEAD_DIM, A_KV_GROUPS
    NQ = R * QC
    VR = dh + V_AUG
    c = pl.program_id(1)
    t0 = c * QC
    tq = t0 + (lax.broadcasted_iota(jnp.int32, (1, NQ), 1) & (QC - 1))
    tq1 = t0 + lax.broadcasted_iota(jnp.int32, (1, QC), 1)
    o_cmp = [_nsa_select(g, q_ref, kc_ref, vct_ref, smt_ref, selb_ref, qa_ref, tq, tq1) for g in range(G)]

    bpt = BLOCKS_PER_TILE
    KT = sa_ref.shape[1]
    kiota = lax.broadcasted_iota(jnp.int32, (KT, 1), 0)
    pad_rows = jnp.where(lax.broadcasted_iota(jnp.int32, (16 - bpt, NQ), 0) == K_PAD_LANE - bpt,
                         NEG_INF, 0.0)

    def set_bias_rows(j):
        for g in range(G):
            b8 = selb_ref[g, pl.ds(pl.multiple_of(j * bpt, bpt), bpt), :]
            rows = jnp.concatenate([jnp.concatenate([b8] * R, axis=1), pad_rows], axis=0)
            qa_ref[g, dh:dh + 16, :] = rows.astype(BF16)

    def scores(t, dst):
        k0 = pl.multiple_of(t * KT, KT)
        for g in range(G):
            dst[g] = _dot(ks_ref[g, pl.ds(k0, KT), :], qa_ref[g])

    def update(t, src, carry, causal):
        k0 = pl.multiple_of(t * KT, KT)
        out = []
        for g in range(G):
            m_i, acc = carry[2 * g], carry[2 * g + 1]
            s_i = src[g]
            if causal:
                cb = jnp.where((k0 + kiota) <= tq1, 0.0, NEG_INF)
                s_i = s_i + jnp.concatenate([cb] * R, axis=1)
            m_new = jnp.maximum(m_i, jnp.max(s_i, axis=0, keepdims=True))
            alpha = jnp.exp2(m_i - m_new)
            p_i = jnp.exp2(s_i - m_new)
            acc = alpha * acc + _dot(vst_ref[g * VR:(g + 1) * VR, pl.ds(k0, KT)], p_i.astype(BF16))
            out.extend((m_new, acc))
        return tuple(out)

    def sel_step(j, carry):
        scores(2 * j + 1, sb_ref)
        carry = update(2 * j, sa_ref, carry, False)
        set_bias_rows(j + 1)
        scores(2 * j + 2, sa_ref)
        return update(2 * j + 1, sb_ref, carry, False)

    n_tiles = (t0 + QC + 2 * KT - 1) // (2 * KT)
    set_bias_rows(0)
    scores(0, sa_ref)
    init = (jnp.full((1, NQ), NEG_INF, F32), jnp.zeros((VR, NQ), F32)) * G
    carry = lax.fori_loop(0, n_tiles - 1, sel_step, init)
    last = n_tiles - 1
    scores(2 * last + 1, sb_ref)
    carry = update(2 * last, sa_ref, carry, True)
    carry = update(2 * last + 1, sb_ref, carry, True)

    gt = g_ref[...]
    WL = A_WINDOW + QC
    io = lax.broadcasted_iota(jnp.int32, (QC, QC), 0)
    qo = lax.broadcasted_iota(jnp.int32, (QC, QC), 1)
    lo_bias = jnp.concatenate([jnp.where(io > qo, 0.0, NEG_INF)] * R, axis=1)
    hi_bias = jnp.concatenate([jnp.where(io <= qo, 0.0, NEG_INF)] * R, axis=1)
    outs = []
    for g in range(G):
        acc_s = carry[2 * g + 1]
        o_sel = acc_s[0:dh] / acc_s[dh:dh + 1]

        s_w = _dot(kw_ref[g, pl.ds(pl.multiple_of(t0, QC), WL), :], qa_ref[g])
        s_w = jnp.concatenate([s_w[0:QC] + lo_bias, s_w[QC:WL - QC], s_w[WL - QC:WL] + hi_bias], axis=0)
        m_w = jnp.max(s_w, axis=0, keepdims=True)
        p_w = jnp.exp2(s_w - m_w)
        acc_w = _dot(vwt_ref[g * VR:(g + 1) * VR, pl.ds(pl.multiple_of(t0, QC), WL)], p_w.astype(BF16))
        o_win = acc_w[0:dh] / acc_w[dh:dh + 1]

        for r in range(R):
            sl = slice(r * QC, (r + 1) * QC)
            gr = g * _GATE_ROWS + 3 * r
            outs.append(gt[gr:gr + 1] * o_cmp[g][:, sl] + gt[gr + 1:gr + 2] * o_sel[:, sl]
                        + gt[gr + 2:gr + 3] * o_win[:, sl])
    o_ref[...] = jnp.concatenate(outs, axis=0).astype(o_ref.dtype)


def _sel_map_t(S):
    n_cmp = (S - A_CMP_LEN) // A_CMP_STRIDE + 1
    n_seg = S // A_CMP_STRIDE
    n_sel = S // A_SEL_LEN
    tok = np.arange(n_cmp)[:, None] * A_CMP_STRIDE + np.arange(A_CMP_LEN)[None, :]
    sm = np.zeros((n_seg, n_sel), np.float32)
    np.add.at(sm, (np.repeat(np.arange(n_cmp), A_CMP_LEN), (tok // A_SEL_LEN).reshape(-1)), 1.0 / A_CMP_LEN)
    return jnp.asarray(sm.T, BF16)


def _group_rows(t, B, S):
    return t.reshape(B, S, A_KV_GROUPS, HEAD_DIM).transpose(0, 2, 1, 3).reshape(B * A_KV_GROUPS, S, HEAD_DIM)


def _nsa_keys(ks, kw, B, S):
    G, dh, W = A_KV_GROUPS, HEAD_DIM, A_WINDOW
    blk = (np.arange(S) // A_SEL_LEN) % BLOCKS_PER_TILE
    onehot = jnp.asarray(blk[:, None] == np.arange(K_AUG)[None, :], BF16)
    ks_aug = jnp.concatenate([_group_rows(ks, B, S), jnp.broadcast_to(onehot, (B * G, S, K_AUG))], axis=2)
    front = jnp.asarray(np.arange(dh + K_AUG) == dh + K_PAD_LANE, BF16)
    kw_aug = jnp.concatenate([jnp.broadcast_to(front, (B * G, W, dh + K_AUG)),
                              jnp.pad(_group_rows(kw, B, S), ((0, 0), (0, 0), (0, K_AUG)))], axis=1)
    return ks_aug, kw_aug


def _nsa(aqt, k_cmp, v_cmp_t, ks_aug, vst, kw_aug, vwt, gt, B, S):
    G, R, dh, QC, W = A_KV_GROUPS, A_GROUP_HEADS, HEAD_DIM, A_Q_CHUNK, A_WINDOW
    T = B * S
    nq = S // QC
    n_seg = S // A_CMP_STRIDE
    n_sel = S // A_SEL_LEN
    VR = dh + V_AUG
    smt = _sel_map_t(S)
    vwt_pad = jnp.pad(vwt.reshape(G * VR, B, S), ((0, 0), (0, 0), (W, 0))).reshape(G * VR, B * (S + W))
    qspec = pl.BlockSpec((G * R * dh, QC), lambda b, c: (0, b * nq + c))
    return pl.pallas_call(
        _nsa_kernel,
        grid=(B, nq),
        in_specs=[qspec,
                  pl.BlockSpec((G, n_seg, dh), lambda b, c: (b, 0, 0)),
                  pl.BlockSpec((G, dh, n_seg), lambda b, c: (b, 0, 0)),
                  pl.BlockSpec((n_sel, n_seg), lambda b, c: (0, 0)),
                  pl.BlockSpec((G, S, dh + K_AUG), lambda b, c: (b, 0, 0)),
                  pl.BlockSpec((G * VR, S), lambda b, c: (0, b)),
                  pl.BlockSpec((G, S + W, dh + K_AUG), lambda b, c: (b, 0, 0)),
                  pl.BlockSpec((G * VR, S + W), lambda b, c: (0, b)),
                  pl.BlockSpec((G * _GATE_ROWS, QC), lambda b, c: (0, b * nq + c))],
        out_specs=qspec,
        out_shape=jax.ShapeDtypeStruct((G * R * dh, T), BF16),
        scratch_shapes=[pltpu.VMEM((G, n_sel, QC), F32), pltpu.VMEM((G, dh + K_AUG, R * QC), BF16),
                        pltpu.VMEM((G, KV_TILE // 2, R * QC), F32), pltpu.VMEM((G, KV_TILE // 2, R * QC), F32)],
        compiler_params=_cparams(("arbitrary", "arbitrary")),
        name="nsa_attention",
    )(aqt, k_cmp, v_cmp_t, smt, ks_aug, vst, kw_aug, vwt_pad, gt)


def _swa_kernel(q_ref, kp_ref, kc_ref, vpt_ref, vct_ref, sink_ref, o_ref):
    BLK, R, dh = B_BLOCK, B_HEADS // B_KV_HEADS, HEAD_DIM
    NQ = R * BLK
    n = pl.program_id(2)
    qT = _lane_heads(q_ref[...], dh, R)
    k = jnp.concatenate([kp_ref[0], kc_ref[0]], axis=0)
    s = _dot(k, qT)
    kj = lax.broadcasted_iota(jnp.int32, (2 * BLK, 1), 0) - BLK
    qi = lax.broadcasted_iota(jnp.int32, (1, NQ), 1) & (BLK - 1)
    ok = (kj <= qi) & (kj > qi - B_WINDOW) & ((n > 0) | (kj >= 0))
    s = jnp.where(ok, s, NEG_INF)
    sink = sink_ref[0] * LOG2E
    m = jnp.maximum(jnp.max(s, axis=0, keepdims=True), sink)
    p = jnp.exp2(s - m)
    vT = jnp.concatenate([vpt_ref[...], vct_ref[...]], axis=1)
    acc = _dot(vT, p.astype(BF16))
    o = acc[0:dh] / (acc[dh:dh + 1] + jnp.exp2(sink - m))
    o_ref[...] = jnp.concatenate([o[:, r * BLK:(r + 1) * BLK] for r in range(R)], axis=0).astype(o_ref.dtype)


def _swa(bqt, bk, bvt, sinks, B, S):
    G, R, dh, BLK = B_KV_HEADS, B_HEADS // B_KV_HEADS, HEAD_DIM, B_BLOCK
    T = B * S
    nb = S // BLK
    sink_cols = jnp.repeat(sinks.astype(F32).reshape(G, R), BLK, axis=1).reshape(G, 1, R * BLK)
    prev = lambda n: jnp.maximum(n - 1, 0)
    qspec = pl.BlockSpec((R * dh, BLK), lambda b, g, n: (g, b * nb + n))
    return pl.pallas_call(
        _swa_kernel,
        grid=(B, G, nb),
        in_specs=[qspec,
                  pl.BlockSpec((1, BLK, dh), lambda b, g, n: (b * G + g, prev(n), 0)),
                  pl.BlockSpec((1, BLK, dh), lambda b, g, n: (b * G + g, n, 0)),
                  pl.BlockSpec((dh + V_AUG, BLK), lambda b, g, n: (g, b * nb + prev(n))),
                  pl.BlockSpec((dh + V_AUG, BLK), lambda b, g, n: (g, b * nb + n)),
                  pl.BlockSpec((1, 1, R * BLK), lambda b, g, n: (g, 0, 0))],
        out_specs=qspec,
        out_shape=jax.ShapeDtypeStruct((G * R * dh, T), BF16),
        compiler_params=_cparams(("arbitrary", "arbitrary", "arbitrary")),
        name="swa_sink_attention",
    )(bqt, bk, bk, bvt, bvt, sink_cols)


def _mla_proj_kernel(cq_ref, ckv_ref, kpet_ref, qn_ref, kvn_ref, wqt_ref, wkvt_ref, c_ref, s_ref,
                     qf_ref, kf_ref, vt_ref):
    hq = _rms(cq_ref[...], qn_ref[...]).astype(BF16)
    hkv = _rms(ckv_ref[...], kvn_ref[...]).astype(BF16)
    qt = _dot_nt(wqt_ref[...], hq)
    kvt = _dot_nt(wkvt_ref[...], hkv)
    c = c_ref[...]
    s = s_ref[...]
    kpe = kpet_ref[...].astype(BF16)
    scale = (C_NOPE + C_ROPE) ** -0.5 * LOG2E
    half = C_ROPE // 2
    tm = qt.shape[1]
    zpad = jnp.zeros((C_QK_PAD - C_NOPE - C_ROPE, tm), BF16)
    ones_row = (lax.broadcasted_iota(jnp.int32, (V_AUG, tm), 0) == 0).astype(BF16)
    for h in range(C_HEADS):
        b = h * C_QK_PAD
        qf_ref[h, 0:C_NOPE, :] = (qt[b:b + C_NOPE] * scale).astype(BF16)
        x1 = qt[b + C_NOPE:b + C_NOPE + half]
        x2 = qt[b + C_NOPE + half:b + C_NOPE + C_ROPE]
        qf_ref[h, C_NOPE:C_NOPE + half, :] = ((x1 * c - x2 * s) * scale).astype(BF16)
        qf_ref[h, C_NOPE + half:C_NOPE + C_ROPE, :] = ((x2 * c + x1 * s) * scale).astype(BF16)
        qf_ref[h, C_NOPE + C_ROPE:C_QK_PAD, :] = zpad
        kf_ref[h, 0:C_NOPE, :] = kvt[b:b + C_NOPE].astype(BF16)
        kf_ref[h, C_NOPE:C_NOPE + C_ROPE, :] = kpe
        kf_ref[h, C_NOPE + C_ROPE:C_QK_PAD, :] = zpad
        vt_ref[h, 0:C_V, :] = kvt[b + C_NOPE:b + C_NOPE + C_V].astype(BF16)
        vt_ref[h, C_V:C_V + V_AUG, :] = ones_row


def _mla_proj(cq, ckv, kpet, q_norm, w_q_up, kv_norm, w_kv_up, c32t, s32t, tm):
    T = cq.shape[0]
    H = C_HEADS
    wq = w_q_up.reshape(C_Q_RANK, H, C_NOPE + C_ROPE)
    wq = jnp.pad(wq, ((0, 0), (0, 0), (0, C_QK_PAD - C_NOPE - C_ROPE))).reshape(C_Q_RANK, H * C_QK_PAD)
    wqt = wq.T.astype(BF16)
    wkvt = w_kv_up.T.astype(BF16)
    full = lambda a: pl.BlockSpec(a.shape, lambda i: (0,) * a.ndim)
    qn = q_norm.reshape(1, C_Q_RANK)
    kvn = kv_norm.reshape(1, C_KV_RANK)
    return pl.pallas_call(
        _mla_proj_kernel,
        grid=(T // tm,),
        in_specs=[pl.BlockSpec((tm, C_Q_RANK), lambda i: (i, 0)),
                  pl.BlockSpec((tm, C_KV_RANK), lambda i: (i, 0)),
                  pl.BlockSpec((C_ROPE, tm), lambda i: (0, i)),
                  full(qn), full(kvn), full(wqt), full(wkvt),
                  pl.BlockSpec((C_ROPE // 2, tm), lambda i: (0, i)),
                  pl.BlockSpec((C_ROPE // 2, tm), lambda i: (0, i))],
        out_specs=[pl.BlockSpec((H, C_QK_PAD, tm), lambda i: (0, 0, i)),
                   pl.BlockSpec((H, C_QK_PAD, tm), lambda i: (0, 0, i)),
                   pl.BlockSpec((H, C_V + V_AUG, tm), lambda i: (0, 0, i))],
        out_shape=[jax.ShapeDtypeStruct((H, C_QK_PAD, T), BF16),
                   jax.ShapeDtypeStruct((H, C_QK_PAD, T), BF16),
                   jax.ShapeDtypeStruct((H, C_V + V_AUG, T), BF16)],
        compiler_params=_cparams(("arbitrary",)),
        name="mla_projection",
    )(cq, ckv, kpet, qn, kvn, wqt, wkvt, c32t, s32t)


MLA_HEADS_PER_STEP = 2


def _mla_kernel(q_ref, kf_ref, vt_ref, o_ref, sa_ref, sb_ref):
    HB = q_ref.shape[0]
    QT = q_ref.shape[2]
    KT = sa_ref.shape[1]
    qi = pl.program_id(2)
    tq = qi * QT + lax.broadcasted_iota(jnp.int32, (1, QT), 1)
    kiota = lax.broadcasted_iota(jnp.int32, (KT, 1), 0)

    def scores(t, dst):
        k0 = pl.multiple_of(t * KT, KT)
        for h in range(HB):
            dst[h] = _dot(kf_ref[h, pl.ds(k0, KT), :], q_ref[h])

    def update(t, src, carry, causal):
        k0 = pl.multiple_of(t * KT, KT)
        out = []
        for h in range(HB):
            m_i, acc = carry[2 * h], carry[2 * h + 1]
            s = src[h]
            if causal:
                s = jnp.where((k0 + kiota) <= tq, s, NEG_INF)
            m_new = jnp.maximum(m_i, jnp.max(s, axis=0, keepdims=True))
            alpha = jnp.exp2(m_i - m_new)
            p = jnp.exp2(s - m_new)
            acc = alpha * acc + _dot(vt_ref[h, :, pl.ds(k0, KT)], p.astype(BF16))
            out.extend((m_new, acc))
        return tuple(out)

    def step(j, carry):
        scores(2 * j + 1, sb_ref)
        carry = update(2 * j, sa_ref, carry, False)
        scores(2 * j + 2, sa_ref)
        return update(2 * j + 1, sb_ref, carry, False)

    scores(0, sa_ref)
    init = (jnp.full((1, QT), NEG_INF, F32), jnp.zeros((C_V + V_AUG, QT), F32)) * HB
    carry = lax.fori_loop(0, qi, step, init)
    scores(2 * qi + 1, sb_ref)
    carry = update(2 * qi, sa_ref, carry, True)
    carry = update(2 * qi + 1, sb_ref, carry, True)
    outs = [carry[2 * h + 1][0:C_V] / carry[2 * h + 1][C_V:C_V + 1] for h in range(HB)]
    o_ref[...] = jnp.concatenate(outs, axis=0).astype(o_ref.dtype)


def _mla(qft, kf, vt, B, S):
    H, HB = C_HEADS, MLA_HEADS_PER_STEP
    T = B * S
    QT = min(KV_TILE, S)
    nq = S // QT
    return pl.pallas_call(
        _mla_kernel,
        grid=(B, H // HB, nq),
        in_specs=[pl.BlockSpec((HB, C_QK_PAD, QT), lambda b, h, q: (h, 0, b * nq + q)),
                  pl.BlockSpec((HB, S, C_QK_PAD), lambda b, h, q: (h, b, 0)),
                  pl.BlockSpec((HB, C_V + V_AUG, S), lambda b, h, q: (h, 0, b))],
        out_specs=pl.BlockSpec((HB * C_V, QT), lambda b, h, q: (h, b * nq + q)),
        out_shape=jax.ShapeDtypeStruct((H * C_V, T), BF16),
        scratch_shapes=[pltpu.VMEM((HB, QT // 2, QT), F32), pltpu.VMEM((HB, QT // 2, QT), F32)],
        compiler_params=_cparams(("arbitrary", "arbitrary", "arbitrary")),
        name="mla_attention",
    )(qft, kf, vt)


def _merge_kernel(x_ref, oa_ref, ob_ref, oc_ref, nrm_ref, wg_ref, wa_ref, wb_ref, wc_ref, wo_ref, o_ref):
    x = x_ref[...]
    h = _rms(x, nrm_ref[...]).astype(BF16)
    g = _sigmoid(_dot(h, wg_ref[...]))
    D = D_MODEL
    m = (g[:, 0:D] * _dot(oa_ref[...], wa_ref[...]) + g[:, D:2 * D] * _dot(ob_ref[...], wb_ref[...])
         + g[:, 2 * D:3 * D] * _dot(oc_ref[...], wc_ref[...]))
    o_ref[...] = x + _dot(m.astype(BF16), wo_ref[...])


def _merge(x, oa, ob, oc, mix_norm, w_gate, w_a, w_b, w_c, w_out, tm):
    T = x.shape[0]
    row = lambda n: pl.BlockSpec((tm, n), lambda i: (i, 0))
    full = lambda a: pl.BlockSpec(a.shape, lambda i: (0,) * a.ndim)
    nrm = mix_norm.reshape(1, D_MODEL)
    ws = [w.astype(BF16) for w in (w_gate, w_a, w_b, w_c, w_out)]
    return pl.pallas_call(
        _merge_kernel,
        grid=(T // tm,),
        in_specs=[row(D_MODEL), row(512), row(512), row(512), full(nrm)] + [full(w) for w in ws],
        out_specs=row(D_MODEL),
        out_shape=jax.ShapeDtypeStruct((T, D_MODEL), F32),
        compiler_params=_cparams(("arbitrary",)),
        name="branch_merge",
    )(x, oa, ob, oc, nrm, *ws)


def _ffn_kernel(x_ref, p_ref, fn_ref, wg_ref, wu_ref, wd_ref, pn_ref, wpg_ref, wpe_ref, fin_ref, o_ref,
                *, final):
    x = x_ref[...]
    h = _rms(x, fn_ref[...]).astype(BF16)
    a = _dot(h, wg_ref[...])
    u = _dot(h, wu_ref[...])
    y = (a * _sigmoid(a) * u).astype(BF16)
    x = x + _dot(y, wd_ref[...])
    gate = _sigmoid(_dot(_rms(x, pn_ref[...]).astype(BF16), wpg_ref[...]))
    x = x + gate * _dot(p_ref[...].astype(BF16), wpe_ref[...])
    if final:
        x = _rms(x, fin_ref[...])
    o_ref[...] = x


def _ffn_ple(x, p_i, ffn_norm, w_gate, w_up, w_down, ple_norm, w_ple_gate, w_ple_proj, final_norm, final, tm):
    T = x.shape[0]
    row = lambda n: pl.BlockSpec((tm, n), lambda i: (i, 0))
    full = lambda a: pl.BlockSpec(a.shape, lambda i: (0,) * a.ndim)
    fn = ffn_norm.reshape(1, D_MODEL)
    pn = ple_norm.reshape(1, D_MODEL)
    fin = final_norm.reshape(1, D_MODEL)
    wg, wu, wd, wpg, wpe = [w.astype(BF16) for w in (w_gate, w_up, w_down, w_ple_gate, w_ple_proj)]
    return pl.pallas_call(
        functools.partial(_ffn_kernel, final=final),
        grid=(T // tm,),
        in_specs=[row(D_MODEL), row(PLE_DIM), full(fn), full(wg), full(wu), full(wd), full(pn),
                  full(wpg), full(wpe), full(fin)],
        out_specs=row(D_MODEL),
        out_shape=jax.ShapeDtypeStruct((T, D_MODEL), F32),
        compiler_params=_cparams(("arbitrary",)),
        name="ffn_ple",
    )(x, p_i, fn, wg, wu, wd, pn, wpg, wpe, fin)


def _layer(x, p_i, tables, B, S, final_norm, final, mix_norm, w_in, a_cmp_pos_k, a_cmp_w1_k, a_cmp_w2_k,
           a_cmp_pos_v, a_cmp_w1_v, a_cmp_w2_v, b_sinks, c_q_norm, c_w_q_up, c_kv_norm, c_w_kv_up,
           w_branch_gate, w_branch_a, w_branch_b, w_branch_c, w_out, ffn_norm, w_ffn_gate, w_ffn_up,
           w_ffn_down, ple_norm, w_ple_proj, w_ple_gate):
    T = B * S
    tm = min(512, T)
    (kc, ks, kw, bk, vc, cq, ckv, aqt, bqt, vst, vwt, bvt, gt, kpet) = _inproj(x, mix_norm, w_in, tables, tm)

    k_cmp, v_cmp_t = _compress(kc, vc, a_cmp_pos_k, a_cmp_w1_k, a_cmp_w2_k,
                               a_cmp_pos_v, a_cmp_w1_v, a_cmp_w2_v, B, S)
    ks_aug, kw_aug = _nsa_keys(ks, kw, B, S)
    oat = _nsa(aqt, k_cmp, v_cmp_t, ks_aug, vst, kw_aug, vwt, gt, B, S)
    obt = _swa(bqt, _group_rows(bk, B, S), bvt, b_sinks, B, S)

    qft, kft, vt = _mla_proj(cq, ckv, kpet, c_q_norm, c_w_q_up, c_kv_norm, c_w_kv_up,
                             tables[4], tables[5], tm)
    oct = _mla(qft, kft.transpose(0, 2, 1), vt, B, S)

    x = _merge(x, oat.T, obt.T, oct.T, mix_norm, w_branch_gate, w_branch_a, w_branch_b, w_branch_c,
               w_out, min(256, T))
    return _ffn_ple(x, p_i, ffn_norm, w_ffn_gate, w_ffn_up, w_ffn_down, ple_norm, w_ple_gate, w_ple_proj,
                    final_norm, final, min(256, T))


def kernel(x, p, positions, mix_norm, w_in, a_cmp_pos_k, a_cmp_w1_k, a_cmp_w2_k, a_cmp_pos_v, a_cmp_w1_v, a_cmp_w2_v, b_sinks, c_q_norm, c_w_q_up, c_kv_norm, c_w_kv_up, w_branch_gate, w_branch_a, w_branch_b, w_branch_c, w_out, ffn_norm, w_ffn_gate, w_ffn_up, w_ffn_down, ple_norm, w_ple_proj, w_ple_gate, final_norm):
    B, S, D = x.shape
    depth = p.shape[0]
    T = B * S
    tables = _rope_tables(positions, min(512, T))
    xf = x.reshape(T, D)
    per_layer = (mix_norm, w_in, a_cmp_pos_k, a_cmp_w1_k, a_cmp_w2_k, a_cmp_pos_v, a_cmp_w1_v, a_cmp_w2_v,
                 b_sinks, c_q_norm, c_w_q_up, c_kv_norm, c_w_kv_up, w_branch_gate, w_branch_a, w_branch_b,
                 w_branch_c, w_out, ffn_norm, w_ffn_gate, w_ffn_up, w_ffn_down, ple_norm, w_ple_proj,
                 w_ple_gate)
    for i in range(depth):
        xf = _layer(xf, p[i].reshape(T, PLE_DIM), tables, B, S, final_norm, i == depth - 1,
                    *[w[i] for w in per_layer])
    return xf.reshape(B, S, D)
```

```python
import functools
import math

import numpy as np
import jax
import jax.numpy as jnp
from jax import lax
from jax.experimental import pallas as pl
from jax.experimental.pallas import tpu as pltpu

F32 = jnp.float32
BF16 = jnp.bfloat16

D_MODEL = 1024
PLE_DIM = 256
ROPE_THETA = 10000.0
EPS = 1e-6
NEG_INF = -1e30
HEAD_DIM = 64

A_HEADS = 8
A_KV_GROUPS = 2
A_GROUP_HEADS = A_HEADS // A_KV_GROUPS
A_CMP_LEN = 32
A_CMP_STRIDE = 16
A_CMP_HIDDEN = 256
A_SEL_LEN = 64
A_TOPK = 16
A_WINDOW = 512
A_Q_CHUNK = 128
A_FORCE_BONUS = 1e4

B_HEADS = 8
B_KV_HEADS = 2
B_WINDOW = 128
B_BLOCK = 128

C_HEADS = 8
C_Q_RANK = 256
C_KV_RANK = 256
C_NOPE = 64
C_ROPE = 32
C_V = 64
C_QK_PAD = 128

D_FF = int(math.ceil(8 * D_MODEL / 3 / 256)) * 256

LANES = 128
KV_TILE = 512
LOG2E = math.log2(math.e)
V_AUG = 16
K_AUG = 64
K_PAD_LANE = 8
BLOCKS_PER_TILE = KV_TILE // A_SEL_LEN
VMEM_LIMIT = 56 * 1024 * 1024

_ROW_K = 6 * 128
_ROW_COLS = 2 * _ROW_K + 128 + C_Q_RANK + C_KV_RANK
_T_AQ, _T_BQ = 0, 512
_T_VS, _T_VW, _T_BV = 1024, 1152, 1280
_T_G, _T_KPE = 1408, 1440
_T_ROWS = 1472
_GATE_ROWS = 16


def _cparams(sem):
    return pltpu.CompilerParams(dimension_semantics=sem, vmem_limit_bytes=VMEM_LIMIT)


def _rms(x, g):
    return x * lax.rsqrt(jnp.mean(x * x, axis=-1, keepdims=True) + EPS) * g


def _sigmoid(x):
    return 1.0 / (1.0 + jnp.exp(-x))


def _dot(a, b):
    return jnp.dot(a, b, preferred_element_type=F32)


def _dot_nt(a, b):
    return lax.dot_general(a, b, (((1,), (1,)), ((), ())), preferred_element_type=F32)


def _dot_tn(a, b):
    return lax.dot_general(a, b, (((0,), (0,)), ((), ())), preferred_element_type=F32)


def _rope_kernel(pc_ref, pr_ref, i128_ref, i64_ref, i32_ref,
                 c128_ref, s128_ref, c64t_ref, s64t_ref, c32t_ref, s32t_ref):
    ang = pc_ref[...].astype(F32) * i128_ref[...]
    c128_ref[...] = jnp.cos(ang)
    s128_ref[...] = jnp.sin(ang)
    pr = pr_ref[...].astype(F32)
    a64 = i64_ref[...] * pr
    c64t_ref[...] = jnp.cos(a64)
    s64t_ref[...] = jnp.sin(a64)
    a32 = i32_ref[...] * pr
    c32t_ref[...] = jnp.cos(a32)
    s32t_ref[...] = jnp.sin(a32)


def _rope_tables(positions, tm):
    T = positions.size
    pos = positions.reshape(T)
    inv64 = jnp.power(F32(ROPE_THETA), -jnp.arange(0, HEAD_DIM, 2, dtype=F32) / HEAD_DIM)
    inv32 = jnp.power(F32(ROPE_THETA), -jnp.arange(0, C_ROPE, 2, dtype=F32) / C_ROPE)
    inv128 = jnp.tile(inv64, 4).reshape(1, 128)
    full = lambda shape: pl.BlockSpec(shape, lambda i: (0,) * len(shape))
    return pl.pallas_call(
        _rope_kernel,
        grid=(T // tm,),
        in_specs=[pl.BlockSpec((tm, 1), lambda i: (i, 0)),
                  pl.BlockSpec((1, tm), lambda i: (0, i)),
                  full((1, 128)), full((32, 1)), full((16, 1))],
        out_specs=[pl.BlockSpec((tm, 128), lambda i: (i, 0)),
                   pl.BlockSpec((tm, 128), lambda i: (i, 0)),
                   pl.BlockSpec((32, tm), lambda i: (0, i)),
                   pl.BlockSpec((32, tm), lambda i: (0, i)),
                   pl.BlockSpec((16, tm), lambda i: (0, i)),
                   pl.BlockSpec((16, tm), lambda i: (0, i))],
        out_shape=[jax.ShapeDtypeStruct((T, 128), F32), jax.ShapeDtypeStruct((T, 128), F32),
                   jax.ShapeDtypeStruct((32, T), F32), jax.ShapeDtypeStruct((32, T), F32),
                   jax.ShapeDtypeStruct((16, T), F32), jax.ShapeDtypeStruct((16, T), F32)],
        compiler_params=_cparams(("arbitrary",)),
        name="rope_tables",
    )(pos.reshape(T, 1), pos.reshape(1, T), inv128, inv64.reshape(32, 1), inv32.reshape(16, 1))


def _store_v_aug(ref, vt, n_heads):
    dv = vt.shape[0] // n_heads
    tm = vt.shape[1]
    ones_row = (lax.broadcasted_iota(jnp.int32, (V_AUG, tm), 0) == 0).astype(ref.dtype)
    for g in range(n_heads):
        r0 = g * (dv + V_AUG)
        ref[r0:r0 + dv, :] = vt[g * dv:(g + 1) * dv].astype(ref.dtype)
        ref[r0 + dv:r0 + dv + V_AUG, :] = ones_row

def _inproj_kernel(x_ref, nrm_ref, wr_ref, wt_ref, c128_ref, s128_ref, c64t_ref, s64t_ref,
                   c32t_ref, s32t_ref,
                   kc_ref, ks_ref, kw_ref, bk_ref, vc_ref, cq_ref, ckv_ref,
                   aqt_ref, bqt_ref, vst_ref, vwt_ref, bvt_ref, gt_ref, kpet_ref, *, n_seq_tiles):
    h = _rms(x_ref[...], nrm_ref[...]).astype(BF16)
    zr = _dot(h, wr_ref[...])
    c = c128_ref[...]
    s = s128_ref[...]
    tm, dh = c.shape[0], HEAD_DIM

    def roped(i):
        return zr[:, 128 * i:128 * (i + 1)] * c + zr[:, _ROW_K + 128 * i:_ROW_K + 128 * (i + 1)] * s

    pos = (pl.program_id(0) % n_seq_tiles) * tm + lax.broadcasted_iota(jnp.int32, (tm, 128), 0)
    lane = lax.broadcasted_iota(jnp.int32, (tm, 128), 1)
    onehot = (lane - dh == (pos // A_SEL_LEN) % BLOCKS_PER_TILE).astype(F32)
    kc = roped(0)
    bk = roped(5)
    o = 2 * _ROW_K
    vc = zr[:, o:o + 128]
    for g in range(A_KV_GROUPS):
        kc_ref[g] = kc[:, g * dh:(g + 1) * dh]
        vc_ref[g] = vc[:, g * dh:(g + 1) * dh]
        bk_ref[g] = bk[:, g * dh:(g + 1) * dh].astype(BF16)
        ks_ref[g] = (roped(1 + g) + onehot).astype(BF16)
        kw_ref[g] = roped(3 + g).astype(BF16)
    cq_ref[...] = zr[:, o + 128:o + 128 + C_Q_RANK]
    ckv_ref[...] = zr[:, o + 128 + C_Q_RANK:o + 128 + C_Q_RANK + C_KV_RANK]

    zt = _dot_nt(wt_ref[...], h)
    ct = c64t_ref[...]
    st = s64t_ref[...]
    qscale = HEAD_DIM ** -0.5 * LOG2E
    for q_ref, base in ((aqt_ref, _T_AQ), (bqt_ref, _T_BQ)):
        for hh in range(8):
            r0 = base + hh * HEAD_DIM
            x1 = zt[r0:r0 + 32]
            x2 = zt[r0 + 32:r0 + 64]
            q_ref[hh * 64:hh * 64 + 32, :] = ((x1 * ct - x2 * st) * qscale).astype(BF16)
            q_ref[hh * 64 + 32:hh * 64 + 64, :] = ((x2 * ct + x1 * st) * qscale).astype(BF16)
    _store_v_aug(vst_ref, zt[_T_VS:_T_VS + 128], A_KV_GROUPS)
    _store_v_aug(vwt_ref, zt[_T_VW:_T_VW + 128], A_KV_GROUPS)
    _store_v_aug(bvt_ref, zt[_T_BV:_T_BV + 128], B_KV_HEADS)
    gt_ref[...] = _sigmoid(zt[_T_G:_T_G + 2 * _GATE_ROWS])
    c3 = c32t_ref[...]
    s3 = s32t_ref[...]
    x1 = zt[_T_KPE:_T_KPE + 16]
    x2 = zt[_T_KPE + 16:_T_KPE + 32]
    kpet_ref[0:16, :] = x1 * c3 - x2 * s3
    kpet_ref[16:32, :] = x2 * c3 + x1 * s3


def _rot_cols(w, dim):
    k, n = w.shape
    w4 = w.reshape(k, n // dim, 2, dim // 2)
    return jnp.concatenate([-w4[:, :, 1:2], w4[:, :, 0:1]], axis=2).reshape(k, n)


def _inproj(x, mix_norm, w_in, tables, tm, B, S):
    T = x.shape[0]
    G, dh = A_KV_GROUPS, HEAD_DIM
    c128, s128, c64t, s64t, c32t, s32t = tables
    sizes = (512, 128, 128, 128, 128, 128, 128, 24, 512, 128, 128, 256, 256, 32)
    offs = np.concatenate([[0], np.cumsum(sizes)])
    (w_aq, w_kc, w_vc, w_ks, w_vs, w_kw, w_vw, w_g, w_bq, w_bk, w_bv, w_cq, w_ckv, w_kpe) = [
        w_in[:, int(offs[i]):int(offs[i + 1])] for i in range(len(sizes))]

    def spread(w):
        return jnp.pad(w.reshape(D_MODEL, G, dh), ((0, 0), (0, 0), (0, dh))).reshape(D_MODEL, G * 2 * dh)

    kmain = jnp.concatenate([w_kc, spread(w_ks), spread(w_kw), w_bk], axis=1)
    w_row = jnp.concatenate([kmain, _rot_cols(kmain, HEAD_DIM), w_vc, w_cq, w_ckv], axis=1).astype(BF16)
    wg = w_g.reshape(D_MODEL, A_KV_GROUPS, A_GROUP_HEADS * 3)
    wg = jnp.pad(wg, ((0, 0), (0, 0), (0, _GATE_ROWS - A_GROUP_HEADS * 3))).reshape(D_MODEL, 2 * _GATE_ROWS)
    w_t = jnp.concatenate([w_aq, w_bq, w_vs, w_vw, w_bv, wg, w_kpe], axis=1).T.astype(BF16)

    row = lambda n: pl.BlockSpec((tm, n), lambda i: (i, 0))
    colT = lambda n: pl.BlockSpec((n, tm), lambda i: (0, i))
    full = lambda a: pl.BlockSpec(a.shape, lambda i: (0,) * a.ndim)
    nrm = mix_norm.reshape(1, D_MODEL)
    vrows = A_KV_GROUPS * (HEAD_DIM + V_AUG)
    nst = S // tm
    grp = lambda w: pl.BlockSpec((G, tm, w), lambda i: (i // nst, i % nst, 0))
    outs = pl.pallas_call(
        functools.partial(_inproj_kernel, n_seq_tiles=nst),
        grid=(T // tm,),
        in_specs=[row(D_MODEL), full(nrm), full(w_row), full(w_t), row(128), row(128),
                  colT(32), colT(32), colT(16), colT(16)],
        out_specs=[grp(dh), grp(dh + K_AUG), grp(dh + K_AUG), grp(dh), grp(dh), row(C_Q_RANK), row(C_KV_RANK),
                   colT(512), colT(512), colT(vrows), colT(vrows), colT(vrows), colT(2 * _GATE_ROWS), colT(32)],
        out_shape=[jax.ShapeDtypeStruct((B * G, S, dh), F32), jax.ShapeDtypeStruct((B * G, S, dh + K_AUG), BF16),
                   jax.ShapeDtypeStruct((B * G, S, dh + K_AUG), BF16), jax.ShapeDtypeStruct((B * G, S, dh), BF16),
                   jax.ShapeDtypeStruct((B * G, S, dh), F32), jax.ShapeDtypeStruct((T, C_Q_RANK), F32),
                   jax.ShapeDtypeStruct((T, C_KV_RANK), F32),
                   jax.ShapeDtypeStruct((512, T), BF16), jax.ShapeDtypeStruct((512, T), BF16),
                   jax.ShapeDtypeStruct((vrows, T), BF16), jax.ShapeDtypeStruct((vrows, T), BF16),
                   jax.ShapeDtypeStruct((vrows, T), BF16), jax.ShapeDtypeStruct((2 * _GATE_ROWS, T), F32),
                   jax.ShapeDtypeStruct((32, T), F32)],
        compiler_params=_cparams(("arbitrary",)),
        name="in_projection",
    )(x, nrm, w_row, w_t, c128, s128, c64t, s64t, c32t, s32t)
    return outs


def _gelu_tanh(x):
    return 0.5 * x * (1.0 + jnp.tanh(math.sqrt(2.0 / math.pi) * (x + 0.044715 * (x * x * x))))


def _cmp_kernel(seg_ref, pos_ref, w1_ref, w2_ref, w2t_ref, o_ref, ot_ref):
    seg = seg_ref[0, 0]
    n_seg = seg.shape[0]
    half = seg.shape[1]
    a = (seg + pos_ref[0, 0:1, :]).astype(BF16)
    b = (seg + pos_ref[0, 1:2, :]).astype(BF16)
    u = _dot(a, w1_ref[0, 0:half, :])
    v = _dot(b, w1_ref[0, half:2 * half, :])
    pre = u + pltpu.roll(v, n_seg - 1, 0)
    g = _gelu_tanh(pre).astype(BF16)
    o_ref[0, 0] = _dot(g, w2_ref[0]).astype(BF16)
    ot_ref[0, 0] = _dot_nt(w2t_ref[0], g).astype(BF16)


def _compress(kc, vc, pos_k, w1_k, w2_k, pos_v, w1_v, w2_v, B, S):
    G, dh = A_KV_GROUPS, HEAD_DIM
    n_seg = S // A_CMP_STRIDE
    seg_w = A_CMP_STRIDE * dh

    seg = jnp.stack([kc.reshape(B * G, n_seg, seg_w), vc.reshape(B * G, n_seg, seg_w)])
    pos = jnp.stack([pos_k.reshape(2, seg_w), pos_v.reshape(2, seg_w)])
    w1 = jnp.stack([w1_k, w1_v]).astype(BF16)
    w2 = jnp.stack([w2_k, w2_v]).astype(BF16)
    w2t = jnp.stack([w2_k.T, w2_v.T]).astype(BF16)
    per = lambda shape: pl.BlockSpec((1,) + shape, lambda w, i: (w,) + (0,) * len(shape))
    o, ot = pl.pallas_call(
        _cmp_kernel,
        grid=(2, B * G),
        in_specs=[pl.BlockSpec((1, 1, n_seg, seg_w), lambda w, i: (w, i, 0, 0)),
                  per((2, seg_w)), per((2 * seg_w, A_CMP_HIDDEN)), per((A_CMP_HIDDEN, dh)),
                  per((dh, A_CMP_HIDDEN))],
        out_specs=[pl.BlockSpec((1, 1, n_seg, dh), lambda w, i: (w, i, 0, 0)),
                   pl.BlockSpec((1, 1, dh, n_seg), lambda w, i: (w, i, 0, 0))],
        out_shape=[jax.ShapeDtypeStruct((2, B * G, n_seg, dh), BF16),
                   jax.ShapeDtypeStruct((2, B * G, dh, n_seg), BF16)],
        compiler_params=_cparams(("arbitrary", "arbitrary")),
        name="nsa_compress",
    )(seg, pos, w1, w2, w2t)
    return o[0], ot[1]


def _lane_heads(blk, dh, n):
    return jnp.concatenate([blk[r * dh:(r + 1) * dh] for r in range(n)], axis=1)


def _nsa_select(g, q_ref, kc_ref, vct_ref, smt_ref, selb_ref, qa_ref, tq, tq1):
    QC, R, dh = A_Q_CHUNK, A_GROUP_HEADS, HEAD_DIM
    NQ = R * QC
    qT = _lane_heads(q_ref[g * R * dh:(g + 1) * R * dh, :], dh, R)
    flag_rows = jnp.where(lax.broadcasted_iota(jnp.int32, (K_AUG, NQ), 0) == K_PAD_LANE, NEG_INF, 0.0)
    qa_ref[g, 0:dh, :] = qT
    qa_ref[g, dh:dh + K_AUG, :] = flag_rows.astype(BF16)

    n_seg = kc_ref.shape[1]
    s = _dot(kc_ref[g], qT)
    cend = lax.broadcasted_iota(jnp.int32, (n_seg, 1), 0) * A_CMP_STRIDE + (A_CMP_LEN - 1)
    cmask = cend <= tq
    s = jnp.where(cmask, s, NEG_INF)
    m = jnp.max(s, axis=0, keepdims=True)
    e = jnp.where(cmask, jnp.exp2(s - m), 0.0)
    l = jnp.sum(e, axis=0, keepdims=True)
    p = e / jnp.where(l > 0.0, l, 1.0)
    o_cmp = _dot(vct_ref[g], p.astype(BF16))

    psum = p[:, 0:QC]
    for r in range(1, R):
        psum = psum + p[:, r * QC:(r + 1) * QC]
    hi = psum.astype(BF16)
    lo = (psum - hi.astype(F32)).astype(BF16)
    smt = smt_ref[...]
    imp = _dot(smt, hi) + _dot(smt, lo)
    n_sel = imp.shape[0]
    jid = lax.broadcasted_iota(jnp.int32, (n_sel, QC), 0).astype(F32)
    cur = (tq1 // A_SEL_LEN).astype(F32)
    valid = jid <= cur
    forced = (jid == 0.0) | (jid == cur) | (jid == cur - 1.0)

    removed = -3.0e38
    val = jnp.where(valid & jnp.logical_not(forced), imp, removed)
    sel = forced
    for _ in range(min(A_TOPK, n_sel) - 3):
        vmax = jnp.max(val, axis=0, keepdims=True)
        first = jnp.min(jnp.where(val == vmax, jid, float(n_sel)), axis=0, keepdims=True)
        hit = jid == first
        sel = sel | hit
        val = jnp.where(hit, removed, val)
    selb_ref[g] = jnp.where(sel & valid, 0.0, NEG_INF)
    return o_cmp


def _nsa_kernel(q_ref, kc_ref, vct_ref, smt_ref, ks_ref, vst_ref, kw_ref, vwt_ref, g_ref,
                o_ref, selb_ref, qa_ref, sa_ref, sb_ref):
    QC, R, dh, G = A_Q_CHUNK, A_GROUP_HEADS, HEAD_DIM, A_KV_GROUPS
    NQ = R * QC
    VR = dh + V_AUG
    c = pl.program_id(1)
    t0 = c * QC
    tq = t0 + (lax.broadcasted_iota(jnp.int32, (1, NQ), 1) & (QC - 1))
    tq1 = t0 + lax.broadcasted_iota(jnp.int32, (1, QC), 1)
    o_cmp = [_nsa_select(g, q_ref, kc_ref, vct_ref, smt_ref, selb_ref, qa_ref, tq, tq1) for g in range(G)]

    bpt = BLOCKS_PER_TILE
    KT = sa_ref.shape[1]
    kiota = lax.broadcasted_iota(jnp.int32, (KT, 1), 0)
    pad_rows = jnp.where(lax.broadcasted_iota(jnp.int32, (16 - bpt, NQ), 0) == K_PAD_LANE - bpt,
                         NEG_INF, 0.0)

    zero_rows = jnp.zeros((K_AUG - 16, NQ), BF16)

    def scores(t, dst):
        k0 = pl.multiple_of(t * KT, KT)
        for g in range(G):
            b8 = selb_ref[g, pl.ds(pl.multiple_of((t // 2) * bpt, bpt), bpt), :]
            rows = jnp.concatenate([jnp.concatenate([b8] * R, axis=1), pad_rows], axis=0).astype(BF16)
            rhs = jnp.concatenate([qa_ref[g, 0:dh, :], rows, zero_rows], axis=0)
            dst[g] = _dot(ks_ref[g, pl.ds(k0, KT), :], rhs)

    def update(t, src, carry, causal):
        k0 = pl.multiple_of(t * KT, KT)
        out = []
        for g in range(G):
            m_i, acc = carry[2 * g], carry[2 * g + 1]
            s_i = src[g]
            if causal:
                cb = jnp.where((k0 + kiota) <= tq1, 0.0, NEG_INF)
                s_i = s_i + jnp.concatenate([cb] * R, axis=1)
            m_new = jnp.maximum(m_i, jnp.max(s_i, axis=0, keepdims=True))
            alpha = jnp.exp2(m_i - m_new)
            p_i = jnp.exp2(s_i - m_new)
            acc = alpha * acc + _dot(vst_ref[g * VR:(g + 1) * VR, pl.ds(k0, KT)], p_i.astype(BF16))
            out.extend((m_new, acc))
        return tuple(out)

    def sel_step(j, carry):
        scores(2 * j + 1, sb_ref)
        carry = update(2 * j, sa_ref, carry, False)
        scores(2 * j + 2, sa_ref)
        return update(2 * j + 1, sb_ref, carry, False)

    n_tiles = (t0 + QC + 2 * KT - 1) // (2 * KT)
    scores(0, sa_ref)
    init = (jnp.full((1, NQ), NEG_INF, F32), jnp.zeros((VR, NQ), F32)) * G
    carry = lax.fori_loop(0, n_tiles - 1, sel_step, init)
    last = n_tiles - 1
    scores(2 * last + 1, sb_ref)
    carry = update(2 * last, sa_ref, carry, True)
    carry = update(2 * last + 1, sb_ref, carry, True)

    gt = g_ref[...]
    WL = A_WINDOW + QC
    io = lax.broadcasted_iota(jnp.int32, (QC, QC), 0)
    qo = lax.broadcasted_iota(jnp.int32, (QC, QC), 1)
    lo_bias = jnp.concatenate([jnp.where(io > qo, 0.0, NEG_INF)] * R, axis=1)
    hi_bias = jnp.concatenate([jnp.where(io <= qo, 0.0, NEG_INF)] * R, axis=1)
    outs = []
    for g in range(G):
        acc_s = carry[2 * g + 1]
        o_sel = acc_s[0:dh] / acc_s[dh:dh + 1]

        s_w = _dot(kw_ref[g, pl.ds(pl.multiple_of(t0, QC), WL), :], qa_ref[g])
        s_w = jnp.concatenate([s_w[0:QC] + lo_bias, s_w[QC:WL - QC], s_w[WL - QC:WL] + hi_bias], axis=0)
        m_w = jnp.max(s_w, axis=0, keepdims=True)
        p_w = jnp.exp2(s_w - m_w)
        acc_w = _dot(vwt_ref[g * VR:(g + 1) * VR, pl.ds(pl.multiple_of(t0, QC), WL)], p_w.astype(BF16))
        o_win = acc_w[0:dh] / acc_w[dh:dh + 1]

        for r in range(R):
            sl = slice(r * QC, (r + 1) * QC)
            gr = g * _GATE_ROWS + 3 * r
            outs.append(gt[gr:gr + 1] * o_cmp[g][:, sl] + gt[gr + 1:gr + 2] * o_sel[:, sl]
                        + gt[gr + 2:gr + 3] * o_win[:, sl])
    o_ref[...] = jnp.concatenate(outs, axis=0).astype(o_ref.dtype)


def _sel_map_t(S):
    n_cmp = (S - A_CMP_LEN) // A_CMP_STRIDE + 1
    n_seg = S // A_CMP_STRIDE
    n_sel = S // A_SEL_LEN
    tok = np.arange(n_cmp)[:, None] * A_CMP_STRIDE + np.arange(A_CMP_LEN)[None, :]
    sm = np.zeros((n_seg, n_sel), np.float32)
    np.add.at(sm, (np.repeat(np.arange(n_cmp), A_CMP_LEN), (tok // A_SEL_LEN).reshape(-1)), 1.0 / A_CMP_LEN)
    return jnp.asarray(sm.T, BF16)


def _window_keys(kw, B):
    G, dh, W = A_KV_GROUPS, HEAD_DIM, A_WINDOW
    front = jnp.asarray(np.arange(dh + K_AUG) == dh + K_PAD_LANE, BF16)
    return jnp.concatenate([jnp.broadcast_to(front, (B * G, W, dh + K_AUG)), kw], axis=1)


def _nsa(aqt, k_cmp, v_cmp_t, ks_aug, vst, kw_aug, vwt, gt, B, S):
    G, R, dh, QC, W = A_KV_GROUPS, A_GROUP_HEADS, HEAD_DIM, A_Q_CHUNK, A_WINDOW
    T = B * S
    nq = S // QC
    n_seg = S // A_CMP_STRIDE
    n_sel = S // A_SEL_LEN
    VR = dh + V_AUG
    smt = _sel_map_t(S)
    vwt_pad = jnp.pad(vwt.reshape(G * VR, B, S), ((0, 0), (0, 0), (W, 0))).reshape(G * VR, B * (S + W))
    qspec = pl.BlockSpec((G * R * dh, QC), lambda b, c: (0, b * nq + c))
    return pl.pallas_call(
        _nsa_kernel,
        grid=(B, nq),
        in_specs=[qspec,
                  pl.BlockSpec((G, n_seg, dh), lambda b, c: (b, 0, 0)),
                  pl.BlockSpec((G, dh, n_seg), lambda b, c: (b, 0, 0)),
                  pl.BlockSpec((n_sel, n_seg), lambda b, c: (0, 0)),
                  pl.BlockSpec((G, S, dh + K_AUG), lambda b, c: (b, 0, 0)),
                  pl.BlockSpec((G * VR, S), lambda b, c: (0, b)),
                  pl.BlockSpec((G, S + W, dh + K_AUG), lambda b, c: (b, 0, 0)),
                  pl.BlockSpec((G * VR, S + W), lambda b, c: (0, b)),
                  pl.BlockSpec((G * _GATE_ROWS, QC), lambda b, c: (0, b * nq + c))],
        out_specs=qspec,
        out_shape=jax.ShapeDtypeStruct((G * R * dh, T), BF16),
        scratch_shapes=[pltpu.VMEM((G, n_sel, QC), F32), pltpu.VMEM((G, dh + K_AUG, R * QC), BF16),
                        pltpu.VMEM((G, KV_TILE // 2, R * QC), F32), pltpu.VMEM((G, KV_TILE // 2, R * QC), F32)],
        compiler_params=_cparams(("arbitrary", "arbitrary")),
        name="nsa_attention",
    )(aqt, k_cmp, v_cmp_t, smt, ks_aug, vst, kw_aug, vwt_pad, gt)


SWA_BLOCKS_PER_STEP = 4


def _swa_kernel(q_ref, kp_ref, kc_ref, vpt_ref, vct_ref, sink_ref, o_ref):
    BLK, G, R, dh, NB = B_BLOCK, B_KV_HEADS, B_HEADS // B_KV_HEADS, HEAD_DIM, SWA_BLOCKS_PER_STEP
    NQ = R * BLK
    VR = dh + V_AUG
    n = pl.program_id(1)
    kj = lax.broadcasted_iota(jnp.int32, (2 * BLK, 1), 0) - BLK
    qi = lax.broadcasted_iota(jnp.int32, (1, NQ), 1) & (BLK - 1)
    band = (kj <= qi) & (kj > qi - B_WINDOW)
    for g in range(G):
        sink = sink_ref[g] * LOG2E
        for j in range(NB):
            cols = slice(j * BLK, (j + 1) * BLK)
            qT = _lane_heads(q_ref[g * R * dh:(g + 1) * R * dh, cols], dh, R)
            if j == 0:
                k = jnp.concatenate([kp_ref[g], kc_ref[g, 0:BLK, :]], axis=0)
                vT = jnp.concatenate([vpt_ref[g * VR:(g + 1) * VR, :], vct_ref[g * VR:(g + 1) * VR, cols]], axis=1)
                ok = band & ((n > 0) | (kj >= 0))
            else:
                k = kc_ref[g, (j - 1) * BLK:(j + 1) * BLK, :]
                vT = vct_ref[g * VR:(g + 1) * VR, (j - 1) * BLK:(j + 1) * BLK]
                ok = band
            s = jnp.where(ok, _dot(k, qT), NEG_INF)
            m = jnp.maximum(jnp.max(s, axis=0, keepdims=True), sink)
            p = jnp.exp2(s - m)
            acc = _dot(vT, p.astype(BF16))
            o = acc[0:dh] / (acc[dh:dh + 1] + jnp.exp2(sink - m))
            for r in range(R):
                row0 = (g * R + r) * dh
                o_ref[row0:row0 + dh, cols] = o[:, r * BLK:(r + 1) * BLK].astype(o_ref.dtype)


def _swa(bqt, bk, bvt, sinks, B, S):
    G, R, dh, BLK = B_KV_HEADS, B_HEADS // B_KV_HEADS, HEAD_DIM, B_BLOCK
    T = B * S
    nb = S // BLK
    sink_cols = jnp.repeat(sinks.astype(F32).reshape(G, R), BLK, axis=1).reshape(G, 1, R * BLK)
    NB = min(SWA_BLOCKS_PER_STEP, nb)
    ns = nb // NB
    prev = lambda n: jnp.maximum(n * NB - 1, 0)
    VR = dh + V_AUG
    qspec = pl.BlockSpec((G * R * dh, NB * BLK), lambda b, n: (0, b * ns + n))
    return pl.pallas_call(
        _swa_kernel,
        grid=(B, ns),
        in_specs=[qspec,
                  pl.BlockSpec((G, BLK, dh), lambda b, n: (b, prev(n), 0)),
                  pl.BlockSpec((G, NB * BLK, dh), lambda b, n: (b, n, 0)),
                  pl.BlockSpec((G * VR, BLK), lambda b, n: (0, b * nb + prev(n))),
                  pl.BlockSpec((G * VR, NB * BLK), lambda b, n: (0, b * ns + n)),
                  pl.BlockSpec((G, 1, R * BLK), lambda b, n: (0, 0, 0))],
        out_specs=qspec,
        out_shape=jax.ShapeDtypeStruct((G * R * dh, T), BF16),
        compiler_params=_cparams(("arbitrary", "arbitrary")),
        name="swa_sink_attention",
    )(bqt, bk, bk, bvt, bvt, sink_cols)


def _mla_proj_kernel(cq_ref, ckv_ref, kpet_ref, qn_ref, kvn_ref, wqt_ref, wkvt_ref, c_ref, s_ref,
                     qf_ref, kf_ref, vt_ref):
    hq = _rms(cq_ref[...], qn_ref[...]).astype(BF16)
    hkv = _rms(ckv_ref[...], kvn_ref[...]).astype(BF16)
    qt = _dot_nt(wqt_ref[...], hq)
    kvt = _dot_nt(wkvt_ref[...], hkv)
    c = c_ref[...]
    s = s_ref[...]
    kpe = kpet_ref[...]
    scale = (C_NOPE + C_ROPE) ** -0.5 * LOG2E
    half = C_ROPE // 2
    tm = qt.shape[1]
    zpad = jnp.zeros((C_QK_PAD - C_NOPE - C_ROPE, tm), BF16)
    zpad_f32 = jnp.zeros((C_QK_PAD - C_NOPE - C_ROPE, tm), F32)
    ones_row = (lax.broadcasted_iota(jnp.int32, (V_AUG, tm), 0) == 0).astype(BF16)
    for h in range(C_HEADS):
        b = h * C_QK_PAD
        qf_ref[h, 0:C_NOPE, :] = (qt[b:b + C_NOPE] * scale).astype(BF16)
        x1 = qt[b + C_NOPE:b + C_NOPE + half]
        x2 = qt[b + C_NOPE + half:b + C_NOPE + C_ROPE]
        qf_ref[h, C_NOPE:C_NOPE + half, :] = ((x1 * c - x2 * s) * scale).astype(BF16)
        qf_ref[h, C_NOPE + half:C_NOPE + C_ROPE, :] = ((x2 * c + x1 * s) * scale).astype(BF16)
        qf_ref[h, C_NOPE + C_ROPE:C_QK_PAD, :] = zpad
        kft = jnp.concatenate([kvt[b:b + C_NOPE], kpe, zpad_f32], axis=0)
        kf_ref[h] = kft.T.astype(BF16)
        vt_ref[h, 0:C_V, :] = kvt[b + C_NOPE:b + C_NOPE + C_V].astype(BF16)
        vt_ref[h, C_V:C_V + V_AUG, :] = ones_row


def _mla_proj(cq, ckv, kpet, q_norm, w_q_up, kv_norm, w_kv_up, c32t, s32t, tm):
    T = cq.shape[0]
    H = C_HEADS
    wq = w_q_up.reshape(C_Q_RANK, H, C_NOPE + C_ROPE)
    wq = jnp.pad(wq, ((0, 0), (0, 0), (0, C_QK_PAD - C_NOPE - C_ROPE))).reshape(C_Q_RANK, H * C_QK_PAD)
    wqt = wq.T.astype(BF16)
    wkvt = w_kv_up.T.astype(BF16)
    full = lambda a: pl.BlockSpec(a.shape, lambda i: (0,) * a.ndim)
    qn = q_norm.reshape(1, C_Q_RANK)
    kvn = kv_norm.reshape(1, C_KV_RANK)
    return pl.pallas_call(
        _mla_proj_kernel,
        grid=(T // tm,),
        in_specs=[pl.BlockSpec((tm, C_Q_RANK), lambda i: (i, 0)),
                  pl.BlockSpec((tm, C_KV_RANK), lambda i: (i, 0)),
                  pl.BlockSpec((C_ROPE, tm), lambda i: (0, i)),
                  full(qn), full(kvn), full(wqt), full(wkvt),
                  pl.BlockSpec((C_ROPE // 2, tm), lambda i: (0, i)),
                  pl.BlockSpec((C_ROPE // 2, tm), lambda i: (0, i))],
        out_specs=[pl.BlockSpec((H, C_QK_PAD, tm), lambda i: (0, 0, i)),
                   pl.BlockSpec((H, tm, C_QK_PAD), lambda i: (0, i, 0)),
                   pl.BlockSpec((H, C_V + V_AUG, tm), lambda i: (0, 0, i))],
        out_shape=[jax.ShapeDtypeStruct((H, C_QK_PAD, T), BF16),
                   jax.ShapeDtypeStruct((H, T, C_QK_PAD), BF16),
                   jax.ShapeDtypeStruct((H, C_V + V_AUG, T), BF16)],
        compiler_params=_cparams(("arbitrary",)),
        name="mla_projection",
    )(cq, ckv, kpet, qn, kvn, wqt, wkvt, c32t, s32t)


MLA_HEADS_PER_STEP = 2


def _mla_kernel(q_ref, kf_ref, vt_ref, o_ref, sa_ref, sb_ref):
    HB = q_ref.shape[0]
    QT = q_ref.shape[2]
    KT = sa_ref.shape[1]
    qi = pl.program_id(2)
    tq = qi * QT + lax.broadcasted_iota(jnp.int32, (1, QT), 1)
    kiota = lax.broadcasted_iota(jnp.int32, (KT, 1), 0)

    def scores(t, dst):
        k0 = pl.multiple_of(t * KT, KT)
        for h in range(HB):
            dst[h] = _dot(kf_ref[h, pl.ds(k0, KT), :], q_ref[h])

    def update(t, src, carry, causal):
        k0 = pl.multiple_of(t * KT, KT)
        out = []
        for h in range(HB):
            m_i, acc = carry[2 * h], carry[2 * h + 1]
            s = src[h]
            if causal:
                s = jnp.where((k0 + kiota) <= tq, s, NEG_INF)
            m_new = jnp.maximum(m_i, jnp.max(s, axis=0, keepdims=True))
            alpha = jnp.exp2(m_i - m_new)
            p = jnp.exp2(s - m_new)
            acc = alpha * acc + _dot(vt_ref[h, :, pl.ds(k0, KT)], p.astype(BF16))
            out.extend((m_new, acc))
        return tuple(out)

    def step(j, carry):
        scores(2 * j + 1, sb_ref)
        carry = update(2 * j, sa_ref, carry, False)
        scores(2 * j + 2, sa_ref)
        return update(2 * j + 1, sb_ref, carry, False)

    scores(0, sa_ref)
    init = (jnp.full((1, QT), NEG_INF, F32), jnp.zeros((C_V + V_AUG, QT), F32)) * HB
    carry = lax.fori_loop(0, qi, step, init)
    scores(2 * qi + 1, sb_ref)
    carry = update(2 * qi, sa_ref, carry, True)
    carry = update(2 * qi + 1, sb_ref, carry, True)
    outs = [carry[2 * h + 1][0:C_V] / carry[2 * h + 1][C_V:C_V + 1] for h in range(HB)]
    o_ref[...] = jnp.concatenate(outs, axis=0).astype(o_ref.dtype)


def _mla(qft, kf, vt, B, S):
    H, HB = C_HEADS, MLA_HEADS_PER_STEP
    T = B * S
    QT = min(KV_TILE, S)
    nq = S // QT
    return pl.pallas_call(
        _mla_kernel,
        grid=(B, H // HB, nq),
        in_specs=[pl.BlockSpec((HB, C_QK_PAD, QT), lambda b, h, q: (h, 0, b * nq + q)),
                  pl.BlockSpec((HB, S, C_QK_PAD), lambda b, h, q: (h, b, 0)),
                  pl.BlockSpec((HB, C_V + V_AUG, S), lambda b, h, q: (h, 0, b))],
        out_specs=pl.BlockSpec((HB * C_V, QT), lambda b, h, q: (h, b * nq + q)),
        out_shape=jax.ShapeDtypeStruct((H * C_V, T), BF16),
        scratch_shapes=[pltpu.VMEM((HB, QT // 2, QT), F32), pltpu.VMEM((HB, QT // 2, QT), F32)],
        compiler_params=_cparams(("arbitrary", "arbitrary", "arbitrary")),
        name="mla_attention",
    )(qft, kf, vt)


def _merge_kernel(x_ref, oa_ref, ob_ref, oc_ref, nrm_ref, wg_ref, wa_ref, wb_ref, wc_ref, wo_ref, o_ref):
    x = x_ref[...]
    h = _rms(x, nrm_ref[...]).astype(BF16)
    g = _sigmoid(_dot(h, wg_ref[...]))
    D = D_MODEL
    m = (g[:, 0:D] * _dot_tn(oa_ref[...], wa_ref[...]) + g[:, D:2 * D] * _dot_tn(ob_ref[...], wb_ref[...])
         + g[:, 2 * D:3 * D] * _dot_tn(oc_ref[...], wc_ref[...]))
    o_ref[...] = x + _dot(m.astype(BF16), wo_ref[...])


def _merge(x, oat, obt, oct, mix_norm, w_gate, w_a, w_b, w_c, w_out, tm):
    T = x.shape[0]
    row = lambda n: pl.BlockSpec((tm, n), lambda i: (i, 0))
    colT = lambda n: pl.BlockSpec((n, tm), lambda i: (0, i))
    full = lambda a: pl.BlockSpec(a.shape, lambda i: (0,) * a.ndim)
    nrm = mix_norm.reshape(1, D_MODEL)
    ws = [w.astype(BF16) for w in (w_gate, w_a, w_b, w_c, w_out)]
    return pl.pallas_call(
        _merge_kernel,
        grid=(T // tm,),
        in_specs=[row(D_MODEL), colT(512), colT(512), colT(512), full(nrm)] + [full(w) for w in ws],
        out_specs=row(D_MODEL),
        out_shape=jax.ShapeDtypeStruct((T, D_MODEL), F32),
        compiler_params=_cparams(("arbitrary",)),
        name="branch_merge",
    )(x, oat, obt, oct, nrm, *ws)


def _ffn_kernel(x_ref, p_ref, fn_ref, wg_ref, wu_ref, wd_ref, pn_ref, wpg_ref, wpe_ref, fin_ref, o_ref,
                *, final):
    x = x_ref[...]
    h = _rms(x, fn_ref[...]).astype(BF16)
    a = _dot(h, wg_ref[...])
    u = _dot(h, wu_ref[...])
    y = (a * _sigmoid(a) * u).astype(BF16)
    x = x + _dot(y, wd_ref[...])
    gate = _sigmoid(_dot(_rms(x, pn_ref[...]).astype(BF16), wpg_ref[...]))
    x = x + gate * _dot(p_ref[...].astype(BF16), wpe_ref[...])
    if final:
        x = _rms(x, fin_ref[...])
    o_ref[...] = x


def _ffn_ple(x, p_i, ffn_norm, w_gate, w_up, w_down, ple_norm, w_ple_gate, w_ple_proj, final_norm, final, tm):
    T = x.shape[0]
    row = lambda n: pl.BlockSpec((tm, n), lambda i: (i, 0))
    full = lambda a: pl.BlockSpec(a.shape, lambda i: (0,) * a.ndim)
    fn = ffn_norm.reshape(1, D_MODEL)
    pn = ple_norm.reshape(1, D_MODEL)
    fin = final_norm.reshape(1, D_MODEL)
    wg, wu, wd, wpg, wpe = [w.astype(BF16) for w in (w_gate, w_up, w_down, w_ple_gate, w_ple_proj)]
    return pl.pallas_call(
        functools.partial(_ffn_kernel, final=final),
        grid=(T // tm,),
        in_specs=[row(D_MODEL), row(PLE_DIM), full(fn), full(wg), full(wu), full(wd), full(pn),
                  full(wpg), full(wpe), full(fin)],
        out_specs=row(D_MODEL),
        out_shape=jax.ShapeDtypeStruct((T, D_MODEL), F32),
        compiler_params=_cparams(("arbitrary",)),
        name="ffn_ple",
    )(x, p_i, fn, wg, wu, wd, pn, wpg, wpe, fin)


def _layer(x, p_i, tables, B, S, final_norm, final, mix_norm, w_in, a_cmp_pos_k, a_cmp_w1_k, a_cmp_w2_k,
           a_cmp_pos_v, a_cmp_w1_v, a_cmp_w2_v, b_sinks, c_q_norm, c_w_q_up, c_kv_norm, c_w_kv_up,
           w_branch_gate, w_branch_a, w_branch_b, w_branch_c, w_out, ffn_norm, w_ffn_gate, w_ffn_up,
           w_ffn_down, ple_norm, w_ple_proj, w_ple_gate):
    T = B * S
    tm = min(512, T)
    (kc, ks_aug, kw, bk, vc, cq, ckv, aqt, bqt, vst, vwt, bvt, gt, kpet) = _inproj(
        x, mix_norm, w_in, tables, tm, B, S)

    k_cmp, v_cmp_t = _compress(kc, vc, a_cmp_pos_k, a_cmp_w1_k, a_cmp_w2_k,
                               a_cmp_pos_v, a_cmp_w1_v, a_cmp_w2_v, B, S)
    oat = _nsa(aqt, k_cmp, v_cmp_t, ks_aug, vst, _window_keys(kw, B), vwt, gt, B, S)
    obt = _swa(bqt, bk, bvt, b_sinks, B, S)

    qft, kf, vt = _mla_proj(cq, ckv, kpet, c_q_norm, c_w_q_up, c_kv_norm, c_w_kv_up,
                            tables[4], tables[5], tm)
    oct = _mla(qft, kf, vt, B, S)

    x = _merge(x, oat, obt, oct, mix_norm, w_branch_gate, w_branch_a, w_branch_b, w_branch_c,
               w_out, min(256, T))
    return _ffn_ple(x, p_i, ffn_norm, w_ffn_gate, w_ffn_up, w_ffn_down, ple_norm, w_ple_gate, w_ple_proj,
                    final_norm, final, min(256, T))


def kernel(x, p, positions, mix_norm, w_in, a_cmp_pos_k, a_cmp_w1_k, a_cmp_w2_k, a_cmp_pos_v, a_cmp_w1_v, a_cmp_w2_v, b_sinks, c_q_norm, c_w_q_up, c_kv_norm, c_w_kv_up, w_branch_gate, w_branch_a, w_branch_b, w_branch_c, w_out, ffn_norm, w_ffn_gate, w_ffn_up, w_ffn_down, ple_norm, w_ple_proj, w_ple_gate, final_norm):
    B, S, D = x.shape
    depth = p.shape[0]
    T = B * S
    tables = _rope_tables(positions, min(512, T))
    xf = x.reshape(T, D)
    per_layer = (mix_norm, w_in, a_cmp_pos_k, a_cmp_w1_k, a_cmp_w2_k, a_cmp_pos_v, a_cmp_w1_v, a_cmp_w2_v,
                 b_sinks, c_q_norm, c_w_q_up, c_kv_norm, c_w_kv_up, w_branch_gate, w_branch_a, w_branch_b,
                 w_branch_c, w_out, ffn_norm, w_ffn_gate, w_ffn_up, w_ffn_down, ple_norm, w_ple_proj,
                 w_ple_gate)
    for i in range(depth):
        xf = _layer(xf, p[i].reshape(T, PLE_DIM), tables, B, S, final_norm, i == depth - 1,
                    *[w[i] for w in per_layer])
    return xf.reshape(B, S, D)
```

```python
import functools
import math

import numpy as np
import jax
import jax.numpy as jnp
from jax import lax
from jax.experimental import pallas as pl
from jax.experimental.pallas import tpu as pltpu

F32 = jnp.float32
BF16 = jnp.bfloat16

D_MODEL = 1024
PLE_DIM = 256
ROPE_THETA = 10000.0
EPS = 1e-6
NEG_INF = -1e30
HEAD_DIM = 64

A_HEADS = 8
A_KV_GROUPS = 2
A_GROUP_HEADS = A_HEADS // A_KV_GROUPS
A_CMP_LEN = 32
A_CMP_STRIDE = 16
A_CMP_HIDDEN = 256
A_SEL_LEN = 64
A_TOPK = 16
A_WINDOW = 512
A_Q_CHUNK = 128
A_FORCE_BONUS = 1e4

B_HEADS = 8
B_KV_HEADS = 2
B_WINDOW = 128
B_BLOCK = 128

C_HEADS = 8
C_Q_RANK = 256
C_KV_RANK = 256
C_NOPE = 64
C_ROPE = 32
C_V = 64
C_QK_PAD = 128

D_FF = int(math.ceil(8 * D_MODEL / 3 / 256)) * 256

LANES = 128
KV_TILE = 512
LOG2E = math.log2(math.e)
V_AUG = 16
K_AUG = 64
K_PAD_LANE = 8
BLOCKS_PER_TILE = KV_TILE // A_SEL_LEN
VMEM_LIMIT = 56 * 1024 * 1024

_ROW_K = 6 * 128
_ROW_COLS = 2 * _ROW_K + 128 + C_Q_RANK + C_KV_RANK
_T_AQ, _T_BQ = 0, 512
_T_VS, _T_VW, _T_BV = 1024, 1152, 1280
_T_G, _T_KPE = 1408, 1440
_T_ROWS = 1472
_GATE_ROWS = 16


def _cparams(sem):
    return pltpu.CompilerParams(dimension_semantics=sem, vmem_limit_bytes=VMEM_LIMIT)


def _rms(x, g):
    return x * lax.rsqrt(jnp.mean(x * x, axis=-1, keepdims=True) + EPS) * g


def _sigmoid(x):
    return 1.0 / (1.0 + jnp.exp(-x))


def _dot(a, b):
    return jnp.dot(a, b, preferred_element_type=F32)


def _dot_nt(a, b):
    return lax.dot_general(a, b, (((1,), (1,)), ((), ())), preferred_element_type=F32)


TILE_UNROLL = 4


def _tile_loop(n, step, carry):
    main = n // TILE_UNROLL

    def trip(j, c):
        for u in range(TILE_UNROLL):
            c = step(j * TILE_UNROLL + u, c)
        return c

    carry = lax.fori_loop(0, main, trip, carry)
    return lax.fori_loop(main * TILE_UNROLL, n, step, carry)


def _dot_tn(a, b):
    return lax.dot_general(a, b, (((0,), (0,)), ((), ())), preferred_element_type=F32)


def _rope_kernel(pc_ref, pr_ref, i128_ref, i64_ref, i32_ref,
                 c128_ref, s128_ref, c64t_ref, s64t_ref, c32t_ref, s32t_ref):
    ang = pc_ref[...].astype(F32) * i128_ref[...]
    c128_ref[...] = jnp.cos(ang)
    s128_ref[...] = jnp.sin(ang)
    pr = pr_ref[...].astype(F32)
    a64 = i64_ref[...] * pr
    c64t_ref[...] = jnp.cos(a64)
    s64t_ref[...] = jnp.sin(a64)
    a32 = i32_ref[...] * pr
    c32t_ref[...] = jnp.cos(a32)
    s32t_ref[...] = jnp.sin(a32)


def _rope_tables(positions, tm):
    T = positions.size
    pos = positions.reshape(T)
    inv64 = jnp.power(F32(ROPE_THETA), -jnp.arange(0, HEAD_DIM, 2, dtype=F32) / HEAD_DIM)
    inv32 = jnp.power(F32(ROPE_THETA), -jnp.arange(0, C_ROPE, 2, dtype=F32) / C_ROPE)
    inv128 = jnp.tile(inv64, 4).reshape(1, 128)
    full = lambda shape: pl.BlockSpec(shape, lambda i: (0,) * len(shape))
    return pl.pallas_call(
        _rope_kernel,
        grid=(T // tm,),
        in_specs=[pl.BlockSpec((tm, 1), lambda i: (i, 0)),
                  pl.BlockSpec((1, tm), lambda i: (0, i)),
                  full((1, 128)), full((32, 1)), full((16, 1))],
        out_specs=[pl.BlockSpec((tm, 128), lambda i: (i, 0)),
                   pl.BlockSpec((tm, 128), lambda i: (i, 0)),
                   pl.BlockSpec((32, tm), lambda i: (0, i)),
                   pl.BlockSpec((32, tm), lambda i: (0, i)),
                   pl.BlockSpec((16, tm), lambda i: (0, i)),
                   pl.BlockSpec((16, tm), lambda i: (0, i))],
        out_shape=[jax.ShapeDtypeStruct((T, 128), F32), jax.ShapeDtypeStruct((T, 128), F32),
                   jax.ShapeDtypeStruct((32, T), F32), jax.ShapeDtypeStruct((32, T), F32),
                   jax.ShapeDtypeStruct((16, T), F32), jax.ShapeDtypeStruct((16, T), F32)],
        compiler_params=_cparams(("arbitrary",)),
        name="rope_tables",
    )(pos.reshape(T, 1), pos.reshape(1, T), inv128, inv64.reshape(32, 1), inv32.reshape(16, 1))


def _store_v_aug(ref, vt, n_heads):
    dv = vt.shape[0] // n_heads
    tm = vt.shape[1]
    ones_row = (lax.broadcasted_iota(jnp.int32, (V_AUG, tm), 0) == 0).astype(ref.dtype)
    for g in range(n_heads):
        r0 = g * (dv + V_AUG)
        ref[r0:r0 + dv, :] = vt[g * dv:(g + 1) * dv].astype(ref.dtype)
        ref[r0 + dv:r0 + dv + V_AUG, :] = ones_row

def _inproj_kernel(x_ref, nrm_ref, wr_ref, wt_ref, c128_ref, s128_ref, c64t_ref, s64t_ref,
                   c32t_ref, s32t_ref,
                   kc_ref, ks_ref, kw_ref, bk_ref, vc_ref, cq_ref, ckv_ref,
                   aqt_ref, bqt_ref, vst_ref, vwt_ref, bvt_ref, gt_ref, kpet_ref, *, n_seq_tiles):
    h = _rms(x_ref[...], nrm_ref[...]).astype(BF16)
    zr = _dot(h, wr_ref[...])
    c = c128_ref[...]
    s = s128_ref[...]
    tm, dh = c.shape[0], HEAD_DIM

    def roped(i):
        return zr[:, 128 * i:128 * (i + 1)] * c + zr[:, _ROW_K + 128 * i:_ROW_K + 128 * (i + 1)] * s

    pos = (pl.program_id(0) % n_seq_tiles) * tm + lax.broadcasted_iota(jnp.int32, (tm, 128), 0)
    lane = lax.broadcasted_iota(jnp.int32, (tm, 128), 1)
    onehot = (lane - dh == (pos // A_SEL_LEN) % BLOCKS_PER_TILE).astype(F32)
    kc = roped(0)
    bk = roped(5)
    o = 2 * _ROW_K
    vc = zr[:, o:o + 128]
    for g in range(A_KV_GROUPS):
        kc_ref[g] = kc[:, g * dh:(g + 1) * dh]
        vc_ref[g] = vc[:, g * dh:(g + 1) * dh]
        bk_ref[g] = bk[:, g * dh:(g + 1) * dh].astype(BF16)
        ks_ref[g] = (roped(1 + g) + onehot).astype(BF16)
        kw_ref[g] = roped(3 + g).astype(BF16)
    cq_ref[...] = zr[:, o + 128:o + 128 + C_Q_RANK]
    ckv_ref[...] = zr[:, o + 128 + C_Q_RANK:o + 128 + C_Q_RANK + C_KV_RANK]

    zt = _dot_nt(wt_ref[...], h)
    ct = c64t_ref[...]
    st = s64t_ref[...]
    qscale = HEAD_DIM ** -0.5 * LOG2E
    for q_ref, base in ((aqt_ref, _T_AQ), (bqt_ref, _T_BQ)):
        for hh in range(8):
            r0 = base + hh * HEAD_DIM
            x1 = zt[r0:r0 + 32]
            x2 = zt[r0 + 32:r0 + 64]
            q_ref[hh * 64:hh * 64 + 32, :] = ((x1 * ct - x2 * st) * qscale).astype(BF16)
            q_ref[hh * 64 + 32:hh * 64 + 64, :] = ((x2 * ct + x1 * st) * qscale).astype(BF16)
    _store_v_aug(vst_ref, zt[_T_VS:_T_VS + 128], A_KV_GROUPS)
    _store_v_aug(vwt_ref, zt[_T_VW:_T_VW + 128], A_KV_GROUPS)
    _store_v_aug(bvt_ref, zt[_T_BV:_T_BV + 128], B_KV_HEADS)
    gt_ref[...] = _sigmoid(zt[_T_G:_T_G + 2 * _GATE_ROWS])
    c3 = c32t_ref[...]
    s3 = s32t_ref[...]
    x1 = zt[_T_KPE:_T_KPE + 16]
    x2 = zt[_T_KPE + 16:_T_KPE + 32]
    kpet_ref[0:16, :] = x1 * c3 - x2 * s3
    kpet_ref[16:32, :] = x2 * c3 + x1 * s3


def _rot_cols(w, dim):
    k, n = w.shape
    w4 = w.reshape(k, n // dim, 2, dim // 2)
    return jnp.concatenate([-w4[:, :, 1:2], w4[:, :, 0:1]], axis=2).reshape(k, n)


def _inproj(x, mix_norm, w_in, tables, tm, B, S):
    T = x.shape[0]
    G, dh = A_KV_GROUPS, HEAD_DIM
    c128, s128, c64t, s64t, c32t, s32t = tables
    sizes = (512, 128, 128, 128, 128, 128, 128, 24, 512, 128, 128, 256, 256, 32)
    offs = np.concatenate([[0], np.cumsum(sizes)])
    (w_aq, w_kc, w_vc, w_ks, w_vs, w_kw, w_vw, w_g, w_bq, w_bk, w_bv, w_cq, w_ckv, w_kpe) = [
        w_in[:, int(offs[i]):int(offs[i + 1])] for i in range(len(sizes))]

    def spread(w):
        return jnp.pad(w.reshape(D_MODEL, G, dh), ((0, 0), (0, 0), (0, dh))).reshape(D_MODEL, G * 2 * dh)

    kmain = jnp.concatenate([w_kc, spread(w_ks), spread(w_kw), w_bk], axis=1)
    w_row = jnp.concatenate([kmain, _rot_cols(kmain, HEAD_DIM), w_vc, w_cq, w_ckv], axis=1).astype(BF16)
    wg = w_g.reshape(D_MODEL, A_KV_GROUPS, A_GROUP_HEADS * 3)
    wg = jnp.pad(wg, ((0, 0), (0, 0), (0, _GATE_ROWS - A_GROUP_HEADS * 3))).reshape(D_MODEL, 2 * _GATE_ROWS)
    w_t = jnp.concatenate([w_aq, w_bq, w_vs, w_vw, w_bv, wg, w_kpe], axis=1).T.astype(BF16)

    row = lambda n: pl.BlockSpec((tm, n), lambda i: (i, 0))
    colT = lambda n: pl.BlockSpec((n, tm), lambda i: (0, i))
    full = lambda a: pl.BlockSpec(a.shape, lambda i: (0,) * a.ndim)
    nrm = mix_norm.reshape(1, D_MODEL)
    vrows = A_KV_GROUPS * (HEAD_DIM + V_AUG)
    nst = S // tm
    grp = lambda w: pl.BlockSpec((G, tm, w), lambda i: (i // nst, i % nst, 0))
    outs = pl.pallas_call(
        functools.partial(_inproj_kernel, n_seq_tiles=nst),
        grid=(T // tm,),
        in_specs=[row(D_MODEL), full(nrm), full(w_row), full(w_t), row(128), row(128),
                  colT(32), colT(32), colT(16), colT(16)],
        out_specs=[grp(dh), grp(dh + K_AUG), grp(dh + K_AUG), grp(dh), grp(dh), row(C_Q_RANK), row(C_KV_RANK),
                   colT(512), colT(512), colT(vrows), colT(vrows), colT(vrows), colT(2 * _GATE_ROWS), colT(32)],
        out_shape=[jax.ShapeDtypeStruct((B * G, S, dh), F32), jax.ShapeDtypeStruct((B * G, S, dh + K_AUG), BF16),
                   jax.ShapeDtypeStruct((B * G, S, dh + K_AUG), BF16), jax.ShapeDtypeStruct((B * G, S, dh), BF16),
                   jax.ShapeDtypeStruct((B * G, S, dh), F32), jax.ShapeDtypeStruct((T, C_Q_RANK), F32),
                   jax.ShapeDtypeStruct((T, C_KV_RANK), F32),
                   jax.ShapeDtypeStruct((512, T), BF16), jax.ShapeDtypeStruct((512, T), BF16),
                   jax.ShapeDtypeStruct((vrows, T), BF16), jax.ShapeDtypeStruct((vrows, T), BF16),
                   jax.ShapeDtypeStruct((vrows, T), BF16), jax.ShapeDtypeStruct((2 * _GATE_ROWS, T), F32),
                   jax.ShapeDtypeStruct((32, T), F32)],
        compiler_params=_cparams(("arbitrary",)),
        name="in_projection",
    )(x, nrm, w_row, w_t, c128, s128, c64t, s64t, c32t, s32t)
    return outs


def _gelu_tanh(x):
    return 0.5 * x * (1.0 + jnp.tanh(math.sqrt(2.0 / math.pi) * (x + 0.044715 * (x * x * x))))


def _cmp_kernel(seg_ref, pos_ref, w1_ref, w2_ref, w2t_ref, o_ref, ot_ref):
    seg = seg_ref[0, 0]
    n_seg = seg.shape[0]
    half = seg.shape[1]
    a = (seg + pos_ref[0, 0:1, :]).astype(BF16)
    b = (seg + pos_ref[0, 1:2, :]).astype(BF16)
    u = _dot(a, w1_ref[0, 0:half, :])
    v = _dot(b, w1_ref[0, half:2 * half, :])
    pre = u + pltpu.roll(v, n_seg - 1, 0)
    g = _gelu_tanh(pre).astype(BF16)
    o_ref[0, 0] = _dot(g, w2_ref[0]).astype(BF16)
    ot_ref[0, 0] = _dot_nt(w2t_ref[0], g).astype(BF16)


def _compress(kc, vc, pos_k, w1_k, w2_k, pos_v, w1_v, w2_v, B, S):
    G, dh = A_KV_GROUPS, HEAD_DIM
    n_seg = S // A_CMP_STRIDE
    seg_w = A_CMP_STRIDE * dh

    seg = jnp.stack([kc.reshape(B * G, n_seg, seg_w), vc.reshape(B * G, n_seg, seg_w)])
    pos = jnp.stack([pos_k.reshape(2, seg_w), pos_v.reshape(2, seg_w)])
    w1 = jnp.stack([w1_k, w1_v]).astype(BF16)
    w2 = jnp.stack([w2_k, w2_v]).astype(BF16)
    w2t = jnp.stack([w2_k.T, w2_v.T]).astype(BF16)
    per = lambda shape: pl.BlockSpec((1,) + shape, lambda w, i: (w,) + (0,) * len(shape))
    o, ot = pl.pallas_call(
        _cmp_kernel,
        grid=(2, B * G),
        in_specs=[pl.BlockSpec((1, 1, n_seg, seg_w), lambda w, i: (w, i, 0, 0)),
                  per((2, seg_w)), per((2 * seg_w, A_CMP_HIDDEN)), per((A_CMP_HIDDEN, dh)),
                  per((dh, A_CMP_HIDDEN))],
        out_specs=[pl.BlockSpec((1, 1, n_seg, dh), lambda w, i: (w, i, 0, 0)),
                   pl.BlockSpec((1, 1, dh, n_seg), lambda w, i: (w, i, 0, 0))],
        out_shape=[jax.ShapeDtypeStruct((2, B * G, n_seg, dh), BF16),
                   jax.ShapeDtypeStruct((2, B * G, dh, n_seg), BF16)],
        compiler_params=_cparams(("arbitrary", "arbitrary")),
        name="nsa_compress",
    )(seg, pos, w1, w2, w2t)
    return o[0], ot[1]


def _lane_heads(blk, dh, n):
    return jnp.concatenate([blk[r * dh:(r + 1) * dh] for r in range(n)], axis=1)


def _argmax_lowest(val, idx):
    n = val.shape[0]
    v = [val[i:i + 8] for i in range(0, n, 8)]
    j = [idx[i:i + 8] for i in range(0, n, 8)]
    while len(v) > 1:
        nv, nj = [], []
        for a in range(0, len(v) - 1, 2):
            take_b = v[a + 1] > v[a]
            nv.append(jnp.where(take_b, v[a + 1], v[a]))
            nj.append(jnp.where(take_b, j[a + 1], j[a]))
        if len(v) % 2:
            nv.append(v[-1])
            nj.append(j[-1])
        v, j = nv, nj
    vmax = jnp.max(v[0], axis=0, keepdims=True)
    return jnp.min(jnp.where(v[0] == vmax, j[0], float(n)), axis=0, keepdims=True)


def _nsa_select(g, q_ref, kc_ref, vct_ref, smt_ref, selb_ref, qa_ref, tq, tq1):
    QC, R, dh = A_Q_CHUNK, A_GROUP_HEADS, HEAD_DIM
    NQ = R * QC
    qT = _lane_heads(q_ref[g * R * dh:(g + 1) * R * dh, :], dh, R)
    flag_rows = jnp.where(lax.broadcasted_iota(jnp.int32, (K_AUG, NQ), 0) == K_PAD_LANE, NEG_INF, 0.0)
    qa_ref[g, 0:dh, :] = qT
    qa_ref[g, dh:dh + K_AUG, :] = flag_rows.astype(BF16)

    n_seg = kc_ref.shape[1]
    s = _dot(kc_ref[g], qT)
    cend = lax.broadcasted_iota(jnp.int32, (n_seg, 1), 0) * A_CMP_STRIDE + (A_CMP_LEN - 1)
    cmask = cend <= tq
    s = jnp.where(cmask, s, NEG_INF)
    m = jnp.max(s, axis=0, keepdims=True)
    e = jnp.where(cmask, jnp.exp2(s - m), 0.0)
    l = jnp.sum(e, axis=0, keepdims=True)
    p = e / jnp.where(l > 0.0, l, 1.0)
    o_cmp = _dot(vct_ref[g], p.astype(BF16))

    psum = p[:, 0:QC]
    for r in range(1, R):
        psum = psum + p[:, r * QC:(r + 1) * QC]
    hi = psum.astype(BF16)
    lo = (psum - hi.astype(F32)).astype(BF16)
    smt = smt_ref[...]
    imp = _dot(smt, hi) + _dot(smt, lo)
    n_sel = imp.shape[0]
    jid = lax.broadcasted_iota(jnp.int32, (n_sel, QC), 0).astype(F32)
    cur = (tq1 // A_SEL_LEN).astype(F32)
    valid = jid <= cur
    forced = (jid == 0.0) | (jid == cur) | (jid == cur - 1.0)

    removed = -3.0e38
    val = jnp.where(valid & jnp.logical_not(forced), imp, removed)
    sel = forced
    for _ in range(min(A_TOPK, n_sel) - 3):
        first = _argmax_lowest(val, jid)
        hit = jid == first
        sel = sel | hit
        val = jnp.where(hit, removed, val)
    selb_ref[g] = jnp.where(sel & valid, 0.0, NEG_INF)
    return o_cmp


def _nsa_kernel(q_ref, kc_ref, vct_ref, smt_ref, ks_ref, vst_ref, kw_ref, vwt_ref, g_ref,
                o_ref, selb_ref, qa_ref, sa_ref, sb_ref):
    QC, R, dh, G = A_Q_CHUNK, A_GROUP_HEADS, HEAD_DIM, A_KV_GROUPS
    NQ = R * QC
    VR = dh + V_AUG
    c = pl.program_id(1)
    t0 = c * QC
    tq = t0 + (lax.broadcasted_iota(jnp.int32, (1, NQ), 1) & (QC - 1))
    tq1 = t0 + lax.broadcasted_iota(jnp.int32, (1, QC), 1)
    o_cmp = [_nsa_select(g, q_ref, kc_ref, vct_ref, smt_ref, selb_ref, qa_ref, tq, tq1) for g in range(G)]

    gt = g_ref[...]
    WL = A_WINDOW + QC
    io = lax.broadcasted_iota(jnp.int32, (QC, QC), 0)
    qo = lax.broadcasted_iota(jnp.int32, (QC, QC), 1)
    lo_bias = jnp.concatenate([jnp.where(io > qo, 0.0, NEG_INF)] * R, axis=1)
    hi_bias = jnp.concatenate([jnp.where(io <= qo, 0.0, NEG_INF)] * R, axis=1)
    o_cw = []
    for g in range(G):
        s_w = _dot(kw_ref[g, pl.ds(pl.multiple_of(t0, QC), WL), :], qa_ref[g])
        s_w = jnp.concatenate([s_w[0:QC] + lo_bias, s_w[QC:WL - QC], s_w[WL - QC:WL] + hi_bias], axis=0)
        m_w = jnp.max(s_w, axis=0, keepdims=True)
        p_w = jnp.exp2(s_w - m_w)
        acc_w = _dot(vwt_ref[g * VR:(g + 1) * VR, pl.ds(pl.multiple_of(t0, QC), WL)], p_w.astype(BF16))
        o_win = acc_w[0:dh] / acc_w[dh:dh + 1]
        pieces = []
        for r in range(R):
            sl = slice(r * QC, (r + 1) * QC)
            gr = g * _GATE_ROWS + 3 * r
            pieces.append(gt[gr:gr + 1] * o_cmp[g][:, sl] + gt[gr + 2:gr + 3] * o_win[:, sl])
        o_cw.append(jnp.concatenate(pieces, axis=1))

    bpt = BLOCKS_PER_TILE
    KT = sa_ref.shape[1]
    kiota = lax.broadcasted_iota(jnp.int32, (KT, 1), 0)
    pad_rows = jnp.where(lax.broadcasted_iota(jnp.int32, (16 - bpt, NQ), 0) == K_PAD_LANE - bpt,
                         NEG_INF, 0.0)

    zero_rows = jnp.zeros((K_AUG - 16, NQ), BF16)

    def scores(t, dst):
        k0 = pl.multiple_of(t * KT, KT)
        for g in range(G):
            b8 = selb_ref[g, pl.ds(pl.multiple_of((t // 2) * bpt, bpt), bpt), :]
            rows = jnp.concatenate([jnp.concatenate([b8] * R, axis=1), pad_rows], axis=0).astype(BF16)
            rhs = jnp.concatenate([qa_ref[g, 0:dh, :], rows, zero_rows], axis=0)
            dst[g] = _dot(ks_ref[g, pl.ds(k0, KT), :], rhs)

    def update(t, src, carry, causal):
        k0 = pl.multiple_of(t * KT, KT)
        out = []
        for g in range(G):
            m_i, acc = carry[2 * g], carry[2 * g + 1]
            s_i = src[g]
            if causal:
                cb = jnp.where((k0 + kiota) <= tq1, 0.0, NEG_INF)
                s_i = s_i + jnp.concatenate([cb] * R, axis=1)
            m_new = jnp.maximum(m_i, jnp.max(s_i, axis=0, keepdims=True))
            alpha = jnp.exp2(m_i - m_new)
            p_i = jnp.exp2(s_i - m_new)
            acc = alpha * acc + _dot(vst_ref[g * VR:(g + 1) * VR, pl.ds(k0, KT)], p_i.astype(BF16))
            out.extend((m_new, acc))
        return tuple(out)

    def sel_step(j, carry):
        scores(2 * j + 1, sb_ref)
        carry = update(2 * j, sa_ref, carry, False)
        scores(2 * j + 2, sa_ref)
        return update(2 * j + 1, sb_ref, carry, False)

    n_tiles = (t0 + QC + 2 * KT - 1) // (2 * KT)
    scores(0, sa_ref)
    init = (jnp.full((1, NQ), NEG_INF, F32), jnp.zeros((VR, NQ), F32)) * G
    carry = _tile_loop(n_tiles - 1, sel_step, init)
    last = n_tiles - 1
    scores(2 * last + 1, sb_ref)
    carry = update(2 * last, sa_ref, carry, True)
    carry = update(2 * last + 1, sb_ref, carry, True)

    outs = []
    for g in range(G):
        acc_s = carry[2 * g + 1]
        o_sel = acc_s[0:dh] / acc_s[dh:dh + 1]
        for r in range(R):
            sl = slice(r * QC, (r + 1) * QC)
            gr = g * _GATE_ROWS + 3 * r
            outs.append(o_cw[g][:, sl] + gt[gr + 1:gr + 2] * o_sel[:, sl])
    o_ref[...] = jnp.concatenate(outs, axis=0).astype(o_ref.dtype)


def _sel_map_t(S):
    n_cmp = (S - A_CMP_LEN) // A_CMP_STRIDE + 1
    n_seg = S // A_CMP_STRIDE
    n_sel = S // A_SEL_LEN
    tok = np.arange(n_cmp)[:, None] * A_CMP_STRIDE + np.arange(A_CMP_LEN)[None, :]
    sm = np.zeros((n_seg, n_sel), np.float32)
    np.add.at(sm, (np.repeat(np.arange(n_cmp), A_CMP_LEN), (tok // A_SEL_LEN).reshape(-1)), 1.0 / A_CMP_LEN)
    return jnp.asarray(sm.T, BF16)


def _window_keys(kw, B):
    G, dh, W = A_KV_GROUPS, HEAD_DIM, A_WINDOW
    front = jnp.asarray(np.arange(dh + K_AUG) == dh + K_PAD_LANE, BF16)
    return jnp.concatenate([jnp.broadcast_to(front, (B * G, W, dh + K_AUG)), kw], axis=1)


def _nsa(aqt, k_cmp, v_cmp_t, ks_aug, vst, kw_aug, vwt, gt, B, S):
    G, R, dh, QC, W = A_KV_GROUPS, A_GROUP_HEADS, HEAD_DIM, A_Q_CHUNK, A_WINDOW
    T = B * S
    nq = S // QC
    n_seg = S // A_CMP_STRIDE
    n_sel = S // A_SEL_LEN
    VR = dh + V_AUG
    smt = _sel_map_t(S)
    vwt_pad = jnp.pad(vwt.reshape(G * VR, B, S), ((0, 0), (0, 0), (W, 0))).reshape(G * VR, B * (S + W))
    qspec = pl.BlockSpec((G * R * dh, QC), lambda b, c: (0, b * nq + c))
    return pl.pallas_call(
        _nsa_kernel,
        grid=(B, nq),
        in_specs=[qspec,
                  pl.BlockSpec((G, n_seg, dh), lambda b, c: (b, 0, 0)),
                  pl.BlockSpec((G, dh, n_seg), lambda b, c: (b, 0, 0)),
                  pl.BlockSpec((n_sel, n_seg), lambda b, c: (0, 0)),
                  pl.BlockSpec((G, S, dh + K_AUG), lambda b, c: (b, 0, 0)),
                  pl.BlockSpec((G * VR, S), lambda b, c: (0, b)),
                  pl.BlockSpec((G, S + W, dh + K_AUG), lambda b, c: (b, 0, 0)),
                  pl.BlockSpec((G * VR, S + W), lambda b, c: (0, b)),
                  pl.BlockSpec((G * _GATE_ROWS, QC), lambda b, c: (0, b * nq + c))],
        out_specs=qspec,
        out_shape=jax.ShapeDtypeStruct((G * R * dh, T), BF16),
        scratch_shapes=[pltpu.VMEM((G, n_sel, QC), F32), pltpu.VMEM((G, dh + K_AUG, R * QC), BF16),
                        pltpu.VMEM((G, KV_TILE // 2, R * QC), F32), pltpu.VMEM((G, KV_TILE // 2, R * QC), F32)],
        compiler_params=_cparams(("arbitrary", "arbitrary")),
        name="nsa_attention",
    )(aqt, k_cmp, v_cmp_t, smt, ks_aug, vst, kw_aug, vwt_pad, gt)


SWA_BLOCKS_PER_STEP = 4


def _swa_kernel(q_ref, kp_ref, kc_ref, vpt_ref, vct_ref, sink_ref, o_ref):
    BLK, G, R, dh, NB = B_BLOCK, B_KV_HEADS, B_HEADS // B_KV_HEADS, HEAD_DIM, SWA_BLOCKS_PER_STEP
    NQ = R * BLK
    VR = dh + V_AUG
    n = pl.program_id(1)
    kj = lax.broadcasted_iota(jnp.int32, (2 * BLK, 1), 0) - BLK
    qi = lax.broadcasted_iota(jnp.int32, (1, NQ), 1) & (BLK - 1)
    band = (kj <= qi) & (kj > qi - B_WINDOW)
    for g in range(G):
        sink = sink_ref[g] * LOG2E
        for j in range(NB):
            cols = slice(j * BLK, (j + 1) * BLK)
            qT = _lane_heads(q_ref[g * R * dh:(g + 1) * R * dh, cols], dh, R)
            if j == 0:
                k = jnp.concatenate([kp_ref[g], kc_ref[g, 0:BLK, :]], axis=0)
                vT = jnp.concatenate([vpt_ref[g * VR:(g + 1) * VR, :], vct_ref[g * VR:(g + 1) * VR, cols]], axis=1)
                ok = band & ((n > 0) | (kj >= 0))
            else:
                k = kc_ref[g, (j - 1) * BLK:(j + 1) * BLK, :]
                vT = vct_ref[g * VR:(g + 1) * VR, (j - 1) * BLK:(j + 1) * BLK]
                ok = band
            s = jnp.where(ok, _dot(k, qT), NEG_INF)
            m = jnp.maximum(jnp.max(s, axis=0, keepdims=True), sink)
            p = jnp.exp2(s - m)
            acc = _dot(vT, p.astype(BF16))
            o = acc[0:dh] / (acc[dh:dh + 1] + jnp.exp2(sink - m))
            for r in range(R):
                row0 = (g * R + r) * dh
                o_ref[row0:row0 + dh, cols] = o[:, r * BLK:(r + 1) * BLK].astype(o_ref.dtype)


def _swa(bqt, bk, bvt, sinks, B, S):
    G, R, dh, BLK = B_KV_HEADS, B_HEADS // B_KV_HEADS, HEAD_DIM, B_BLOCK
    T = B * S
    nb = S // BLK
    sink_cols = jnp.repeat(sinks.astype(F32).reshape(G, R), BLK, axis=1).reshape(G, 1, R * BLK)
    NB = min(SWA_BLOCKS_PER_STEP, nb)
    ns = nb // NB
    prev = lambda n: jnp.maximum(n * NB - 1, 0)
    VR = dh + V_AUG
    qspec = pl.BlockSpec((G * R * dh, NB * BLK), lambda b, n: (0, b * ns + n))
    return pl.pallas_call(
        _swa_kernel,
        grid=(B, ns),
        in_specs=[qspec,
                  pl.BlockSpec((G, BLK, dh), lambda b, n: (b, prev(n), 0)),
                  pl.BlockSpec((G, NB * BLK, dh), lambda b, n: (b, n, 0)),
                  pl.BlockSpec((G * VR, BLK), lambda b, n: (0, b * nb + prev(n))),
                  pl.BlockSpec((G * VR, NB * BLK), lambda b, n: (0, b * ns + n)),
                  pl.BlockSpec((G, 1, R * BLK), lambda b, n: (0, 0, 0))],
        out_specs=qspec,
        out_shape=jax.ShapeDtypeStruct((G * R * dh, T), BF16),
        compiler_params=_cparams(("arbitrary", "arbitrary")),
        name="swa_sink_attention",
    )(bqt, bk, bk, bvt, bvt, sink_cols)


def _mla_proj_kernel(cq_ref, ckv_ref, kpet_ref, qn_ref, kvn_ref, wqt_ref, wkvt_ref, c_ref, s_ref,
                     qf_ref, kf_ref, vt_ref):
    hq = _rms(cq_ref[...], qn_ref[...]).astype(BF16)
    hkv = _rms(ckv_ref[...], kvn_ref[...]).astype(BF16)
    qt = _dot_nt(wqt_ref[...], hq)
    kvt = _dot_nt(wkvt_ref[...], hkv)
    c = c_ref[...]
    s = s_ref[...]
    kpe = kpet_ref[...]
    scale = (C_NOPE + C_ROPE) ** -0.5 * LOG2E
    half = C_ROPE // 2
    tm = qt.shape[1]
    zpad = jnp.zeros((C_QK_PAD - C_NOPE - C_ROPE, tm), BF16)
    zpad_f32 = jnp.zeros((C_QK_PAD - C_NOPE - C_ROPE, tm), F32)
    ones_row = (lax.broadcasted_iota(jnp.int32, (V_AUG, tm), 0) == 0).astype(BF16)
    for h in range(C_HEADS):
        b = h * C_QK_PAD
        qf_ref[h, 0:C_NOPE, :] = (qt[b:b + C_NOPE] * scale).astype(BF16)
        x1 = qt[b + C_NOPE:b + C_NOPE + half]
        x2 = qt[b + C_NOPE + half:b + C_NOPE + C_ROPE]
        qf_ref[h, C_NOPE:C_NOPE + half, :] = ((x1 * c - x2 * s) * scale).astype(BF16)
        qf_ref[h, C_NOPE + half:C_NOPE + C_ROPE, :] = ((x2 * c + x1 * s) * scale).astype(BF16)
        qf_ref[h, C_NOPE + C_ROPE:C_QK_PAD, :] = zpad
        kft = jnp.concatenate([kvt[b:b + C_NOPE], kpe, zpad_f32], axis=0)
        kf_ref[h] = kft.T.astype(BF16)
        vt_ref[h, 0:C_V, :] = kvt[b + C_NOPE:b + C_NOPE + C_V].astype(BF16)
        vt_ref[h, C_V:C_V + V_AUG, :] = ones_row


def _mla_proj(cq, ckv, kpet, q_norm, w_q_up, kv_norm, w_kv_up, c32t, s32t, tm):
    T = cq.shape[0]
    H = C_HEADS
    wq = w_q_up.reshape(C_Q_RANK, H, C_NOPE + C_ROPE)
    wq = jnp.pad(wq, ((0, 0), (0, 0), (0, C_QK_PAD - C_NOPE - C_ROPE))).reshape(C_Q_RANK, H * C_QK_PAD)
    wqt = wq.T.astype(BF16)
    wkvt = w_kv_up.T.astype(BF16)
    full = lambda a: pl.BlockSpec(a.shape, lambda i: (0,) * a.ndim)
    qn = q_norm.reshape(1, C_Q_RANK)
    kvn = kv_norm.reshape(1, C_KV_RANK)
    return pl.pallas_call(
        _mla_proj_kernel,
        grid=(T // tm,),
        in_specs=[pl.BlockSpec((tm, C_Q_RANK), lambda i: (i, 0)),
                  pl.BlockSpec((tm, C_KV_RANK), lambda i: (i, 0)),
                  pl.BlockSpec((C_ROPE, tm), lambda i: (0, i)),
                  full(qn), full(kvn), full(wqt), full(wkvt),
                  pl.BlockSpec((C_ROPE // 2, tm), lambda i: (0, i)),
                  pl.BlockSpec((C_ROPE // 2, tm), lambda i: (0, i))],
        out_specs=[pl.BlockSpec((H, C_QK_PAD, tm), lambda i: (0, 0, i)),
                   pl.BlockSpec((H, tm, C_QK_PAD), lambda i: (0, i, 0)),
                   pl.BlockSpec((H, C_V + V_AUG, tm), lambda i: (0, 0, i))],
        out_shape=[jax.ShapeDtypeStruct((H, C_QK_PAD, T), BF16),
                   jax.ShapeDtypeStruct((H, T, C_QK_PAD), BF16),
                   jax.ShapeDtypeStruct((H, C_V + V_AUG, T), BF16)],
        compiler_params=_cparams(("arbitrary",)),
        name="mla_projection",
    )(cq, ckv, kpet, qn, kvn, wqt, wkvt, c32t, s32t)


MLA_HEADS_PER_STEP = 2


def _mla_kernel(q_ref, kf_ref, vt_ref, o_ref, sa_ref, sb_ref):
    HB = q_ref.shape[0]
    QT = q_ref.shape[2]
    KT = sa_ref.shape[1]
    qi = pl.program_id(2)
    tq = qi * QT + lax.broadcasted_iota(jnp.int32, (1, QT), 1)
    kiota = lax.broadcasted_iota(jnp.int32, (KT, 1), 0)

    def scores(t, dst):
        k0 = pl.multiple_of(t * KT, KT)
        for h in range(HB):
            dst[h] = _dot(kf_ref[h, pl.ds(k0, KT), :], q_ref[h])

    def update(t, src, carry, causal):
        k0 = pl.multiple_of(t * KT, KT)
        out = []
        for h in range(HB):
            m_i, acc = carry[2 * h], carry[2 * h + 1]
            s = src[h]
            if causal:
                s = jnp.where((k0 + kiota) <= tq, s, NEG_INF)
            m_new = jnp.maximum(m_i, jnp.max(s, axis=0, keepdims=True))
            alpha = jnp.exp2(m_i - m_new)
            p = jnp.exp2(s - m_new)
            acc = alpha * acc + _dot(vt_ref[h, :, pl.ds(k0, KT)], p.astype(BF16))
            out.extend((m_new, acc))
        return tuple(out)

    def step(j, carry):
        scores(2 * j + 1, sb_ref)
        carry = update(2 * j, sa_ref, carry, False)
        scores(2 * j + 2, sa_ref)
        return update(2 * j + 1, sb_ref, carry, False)

    scores(0, sa_ref)
    init = (jnp.full((1, QT), NEG_INF, F32), jnp.zeros((C_V + V_AUG, QT), F32)) * HB
    carry = _tile_loop(qi, step, init)
    scores(2 * qi + 1, sb_ref)
    carry = update(2 * qi, sa_ref, carry, True)
    carry = update(2 * qi + 1, sb_ref, carry, True)
    outs = [carry[2 * h + 1][0:C_V] / carry[2 * h + 1][C_V:C_V + 1] for h in range(HB)]
    o_ref[...] = jnp.concatenate(outs, axis=0).astype(o_ref.dtype)


def _mla(qft, kf, vt, B, S):
    H, HB = C_HEADS, MLA_HEADS_PER_STEP
    T = B * S
    QT = min(KV_TILE, S)
    nq = S // QT
    return pl.pallas_call(
        _mla_kernel,
        grid=(B, H // HB, nq),
        in_specs=[pl.BlockSpec((HB, C_QK_PAD, QT), lambda b, h, q: (h, 0, b * nq + q)),
                  pl.BlockSpec((HB, S, C_QK_PAD), lambda b, h, q: (h, b, 0)),
                  pl.BlockSpec((HB, C_V + V_AUG, S), lambda b, h, q: (h, 0, b))],
        out_specs=pl.BlockSpec((HB * C_V, QT), lambda b, h, q: (h, b * nq + q)),
        out_shape=jax.ShapeDtypeStruct((H * C_V, T), BF16),
        scratch_shapes=[pltpu.VMEM((HB, QT // 2, QT), F32), pltpu.VMEM((HB, QT // 2, QT), F32)],
        compiler_params=_cparams(("arbitrary", "arbitrary", "arbitrary")),
        name="mla_attention",
    )(qft, kf, vt)


def _merge_kernel(x_ref, oa_ref, ob_ref, oc_ref, nrm_ref, wg_ref, wa_ref, wb_ref, wc_ref, wo_ref, o_ref):
    x = x_ref[...]
    h = _rms(x, nrm_ref[...]).astype(BF16)
    g = _sigmoid(_dot(h, wg_ref[...]))
    D = D_MODEL
    m = (g[:, 0:D] * _dot_tn(oa_ref[...], wa_ref[...]) + g[:, D:2 * D] * _dot_tn(ob_ref[...], wb_ref[...])
         + g[:, 2 * D:3 * D] * _dot_tn(oc_ref[...], wc_ref[...]))
    o_ref[...] = x + _dot(m.astype(BF16), wo_ref[...])


def _merge(x, oat, obt, oct, mix_norm, w_gate, w_a, w_b, w_c, w_out, tm):
    T = x.shape[0]
    row = lambda n: pl.BlockSpec((tm, n), lambda i: (i, 0))
    colT = lambda n: pl.BlockSpec((n, tm), lambda i: (0, i))
    full = lambda a: pl.BlockSpec(a.shape, lambda i: (0,) * a.ndim)
    nrm = mix_norm.reshape(1, D_MODEL)
    ws = [w.astype(BF16) for w in (w_gate, w_a, w_b, w_c, w_out)]
    return pl.pallas_call(
        _merge_kernel,
        grid=(T // tm,),
        in_specs=[row(D_MODEL), colT(512), colT(512), colT(512), full(nrm)] + [full(w) for w in ws],
        out_specs=row(D_MODEL),
        out_shape=jax.ShapeDtypeStruct((T, D_MODEL), F32),
        compiler_params=_cparams(("arbitrary",)),
        name="branch_merge",
    )(x, oat, obt, oct, nrm, *ws)


def _ffn_kernel(x_ref, p_ref, fn_ref, wg_ref, wu_ref, wd_ref, pn_ref, wpg_ref, wpe_ref, fin_ref, o_ref,
                *, final):
    x = x_ref[...]
    h = _rms(x, fn_ref[...]).astype(BF16)
    a = _dot(h, wg_ref[...])
    u = _dot(h, wu_ref[...])
    y = (a * _sigmoid(a) * u).astype(BF16)
    x = x + _dot(y, wd_ref[...])
    gate = _sigmoid(_dot(_rms(x, pn_ref[...]).astype(BF16), wpg_ref[...]))
    x = x + gate * _dot(p_ref[...].astype(BF16), wpe_ref[...])
    if final:
        x = _rms(x, fin_ref[...])
    o_ref[...] = x


def _ffn_ple(x, p_i, ffn_norm, w_gate, w_up, w_down, ple_norm, w_ple_gate, w_ple_proj, final_norm, final, tm):
    T = x.shape[0]
    row = lambda n: pl.BlockSpec((tm, n), lambda i: (i, 0))
    full = lambda a: pl.BlockSpec(a.shape, lambda i: (0,) * a.ndim, pipeline_mode=pl.Buffered(1))
    fn = ffn_norm.reshape(1, D_MODEL)
    pn = ple_norm.reshape(1, D_MODEL)
    fin = final_norm.reshape(1, D_MODEL)
    wg, wu, wd, wpg, wpe = [w.astype(BF16) for w in (w_gate, w_up, w_down, w_ple_gate, w_ple_proj)]
    return pl.pallas_call(
        functools.partial(_ffn_kernel, final=final),
        grid=(T // tm,),
        in_specs=[row(D_MODEL), row(PLE_DIM), full(fn), full(wg), full(wu), full(wd), full(pn),
                  full(wpg), full(wpe), full(fin)],
        out_specs=row(D_MODEL),
        out_shape=jax.ShapeDtypeStruct((T, D_MODEL), F32),
        compiler_params=_cparams(("arbitrary",)),
        name="ffn_ple",
    )(x, p_i, fn, wg, wu, wd, pn, wpg, wpe, fin)


def _layer(x, p_i, tables, B, S, final_norm, final, mix_norm, w_in, a_cmp_pos_k, a_cmp_w1_k, a_cmp_w2_k,
           a_cmp_pos_v, a_cmp_w1_v, a_cmp_w2_v, b_sinks, c_q_norm, c_w_q_up, c_kv_norm, c_w_kv_up,
           w_branch_gate, w_branch_a, w_branch_b, w_branch_c, w_out, ffn_norm, w_ffn_gate, w_ffn_up,
           w_ffn_down, ple_norm, w_ple_proj, w_ple_gate):
    T = B * S
    tm = min(512, T)
    (kc, ks_aug, kw, bk, vc, cq, ckv, aqt, bqt, vst, vwt, bvt, gt, kpet) = _inproj(
        x, mix_norm, w_in, tables, tm, B, S)

    k_cmp, v_cmp_t = _compress(kc, vc, a_cmp_pos_k, a_cmp_w1_k, a_cmp_w2_k,
                               a_cmp_pos_v, a_cmp_w1_v, a_cmp_w2_v, B, S)
    oat = _nsa(aqt, k_cmp, v_cmp_t, ks_aug, vst, _window_keys(kw, B), vwt, gt, B, S)
    obt = _swa(bqt, bk, bvt, b_sinks, B, S)

    qft, kf, vt = _mla_proj(cq, ckv, kpet, c_q_norm, c_w_q_up, c_kv_norm, c_w_kv_up,
                            tables[4], tables[5], tm)
    oct = _mla(qft, kf, vt, B, S)

    x = _merge(x, oat, obt, oct, mix_norm, w_branch_gate, w_branch_a, w_branch_b, w_branch_c,
               w_out, min(256, T))
    return _ffn_ple(x, p_i, ffn_norm, w_ffn_gate, w_ffn_up, w_ffn_down, ple_norm, w_ple_gate, w_ple_proj,
                    final_norm, final, min(512, T))


def kernel(x, p, positions, mix_norm, w_in, a_cmp_pos_k, a_cmp_w1_k, a_cmp_w2_k, a_cmp_pos_v, a_cmp_w1_v, a_cmp_w2_v, b_sinks, c_q_norm, c_w_q_up, c_kv_norm, c_w_kv_up, w_branch_gate, w_branch_a, w_branch_b, w_branch_c, w_out, ffn_norm, w_ffn_gate, w_ffn_up, w_ffn_down, ple_norm, w_ple_proj, w_ple_gate, final_norm):
    B, S, D = x.shape
    depth = p.shape[0]
    T = B * S
    tables = _rope_tables(positions, min(512, T))
    xf = x.reshape(T, D)
    per_layer = (mix_norm, w_in, a_cmp_pos_k, a_cmp_w1_k, a_cmp_w2_k, a_cmp_pos_v, a_cmp_w1_v, a_cmp_w2_v,
                 b_sinks, c_q_norm, c_w_q_up, c_kv_norm, c_w_kv_up, w_branch_gate, w_branch_a, w_branch_b,
                 w_branch_c, w_out, ffn_norm, w_ffn_gate, w_ffn_up, w_ffn_down, ple_norm, w_ple_proj,
                 w_ple_gate)
    for i in range(depth):
        xf = _layer(xf, p[i].reshape(T, PLE_DIM), tables, B, S, final_norm, i == depth - 1,
                    *[w[i] for w in per_layer])
    return xf.reshape(B, S, D)
```

```python
import functools
import math

import numpy as np
import jax
import jax.numpy as jnp
from jax import lax
from jax.experimental import pallas as pl
from jax.experimental.pallas import tpu as pltpu

F32 = jnp.float32
BF16 = jnp.bfloat16

D_MODEL = 1024
PLE_DIM = 256
ROPE_THETA = 10000.0
EPS = 1e-6
NEG_INF = -1e30
HEAD_DIM = 64

A_HEADS = 8
A_KV_GROUPS = 2
A_GROUP_HEADS = A_HEADS // A_KV_GROUPS
A_CMP_LEN = 32
A_CMP_STRIDE = 16
A_CMP_HIDDEN = 256
A_SEL_LEN = 64
A_TOPK = 16
A_WINDOW = 512
A_Q_CHUNK = 128
A_FORCE_BONUS = 1e4

B_HEADS = 8
B_KV_HEADS = 2
B_WINDOW = 128
B_BLOCK = 128

C_HEADS = 8
C_Q_RANK = 256
C_KV_RANK = 256
C_NOPE = 64
C_ROPE = 32
C_V = 64
C_QK_PAD = 128

D_FF = int(math.ceil(8 * D_MODEL / 3 / 256)) * 256

LANES = 128
KV_TILE = 512
LOG2E = math.log2(math.e)
V_AUG = 16
K_AUG = 64
K_PAD_LANE = 8
BLOCKS_PER_TILE = KV_TILE // A_SEL_LEN
VMEM_LIMIT = 56 * 1024 * 1024

_ROW_K = 6 * 128
_ROW_COLS = 2 * _ROW_K + 128 + C_Q_RANK + C_KV_RANK
_T_AQ, _T_BQ = 0, 512
_T_VS, _T_VW, _T_BV = 1024, 1152, 1280
_T_G, _T_KPE = 1408, 1440
_T_ROWS = 1472
_GATE_ROWS = 16


def _cparams(sem):
    return pltpu.CompilerParams(dimension_semantics=sem, vmem_limit_bytes=VMEM_LIMIT)


def _rms(x, g):
    return x * lax.rsqrt(jnp.mean(x * x, axis=-1, keepdims=True) + EPS) * g


def _sigmoid(x):
    return 1.0 / (1.0 + jnp.exp(-x))


def _dot(a, b):
    return jnp.dot(a, b, preferred_element_type=F32)


def _dot_nt(a, b):
    return lax.dot_general(a, b, (((1,), (1,)), ((), ())), preferred_element_type=F32)


TILE_UNROLL = 4


def _tile_loop(n, step, carry):
    main = n // TILE_UNROLL

    def trip(j, c):
        for u in range(TILE_UNROLL):
            c = step(j * TILE_UNROLL + u, c)
        return c

    carry = lax.fori_loop(0, main, trip, carry)
    return lax.fori_loop(main * TILE_UNROLL, n, step, carry)


def _dot_tn(a, b):
    return lax.dot_general(a, b, (((0,), (0,)), ((), ())), preferred_element_type=F32)


def _rope_kernel(pc_ref, pr_ref, i128_ref, i64_ref, i32_ref,
                 c128_ref, s128_ref, c64t_ref, s64t_ref, c32t_ref, s32t_ref):
    ang = pc_ref[...].astype(F32) * i128_ref[...]
    c128_ref[...] = jnp.cos(ang)
    s128_ref[...] = jnp.sin(ang)
    pr = pr_ref[...].astype(F32)
    a64 = i64_ref[...] * pr
    c64t_ref[...] = jnp.cos(a64)
    s64t_ref[...] = jnp.sin(a64)
    a32 = i32_ref[...] * pr
    c32t_ref[...] = jnp.cos(a32)
    s32t_ref[...] = jnp.sin(a32)


def _rope_tables(positions, tm):
    T = positions.size
    pos = positions.reshape(T)
    inv64 = jnp.power(F32(ROPE_THETA), -jnp.arange(0, HEAD_DIM, 2, dtype=F32) / HEAD_DIM)
    inv32 = jnp.power(F32(ROPE_THETA), -jnp.arange(0, C_ROPE, 2, dtype=F32) / C_ROPE)
    inv128 = jnp.tile(inv64, 4).reshape(1, 128)
    full = lambda shape: pl.BlockSpec(shape, lambda i: (0,) * len(shape))
    return pl.pallas_call(
        _rope_kernel,
        grid=(T // tm,),
        in_specs=[pl.BlockSpec((tm, 1), lambda i: (i, 0)),
                  pl.BlockSpec((1, tm), lambda i: (0, i)),
                  full((1, 128)), full((32, 1)), full((16, 1))],
        out_specs=[pl.BlockSpec((tm, 128), lambda i: (i, 0)),
                   pl.BlockSpec((tm, 128), lambda i: (i, 0)),
                   pl.BlockSpec((32, tm), lambda i: (0, i)),
                   pl.BlockSpec((32, tm), lambda i: (0, i)),
                   pl.BlockSpec((16, tm), lambda i: (0, i)),
                   pl.BlockSpec((16, tm), lambda i: (0, i))],
        out_shape=[jax.ShapeDtypeStruct((T, 128), F32), jax.ShapeDtypeStruct((T, 128), F32),
                   jax.ShapeDtypeStruct((32, T), F32), jax.ShapeDtypeStruct((32, T), F32),
                   jax.ShapeDtypeStruct((16, T), F32), jax.ShapeDtypeStruct((16, T), F32)],
        compiler_params=_cparams(("arbitrary",)),
        name="rope_tables",
    )(pos.reshape(T, 1), pos.reshape(1, T), inv128, inv64.reshape(32, 1), inv32.reshape(16, 1))


def _store_v_aug(ref, vt, n_heads):
    dv = vt.shape[0] // n_heads
    tm = vt.shape[1]
    ones_row = (lax.broadcasted_iota(jnp.int32, (V_AUG, tm), 0) == 0).astype(ref.dtype)
    for g in range(n_heads):
        r0 = g * (dv + V_AUG)
        ref[r0:r0 + dv, :] = vt[g * dv:(g + 1) * dv].astype(ref.dtype)
        ref[r0 + dv:r0 + dv + V_AUG, :] = ones_row

def _inproj_kernel(x_ref, nrm_ref, wr_ref, wt_ref, c128_ref, s128_ref, c64t_ref, s64t_ref,
                   c32t_ref, s32t_ref,
                   kc_ref, ks_ref, kw_ref, bk_ref, vc_ref, cq_ref, ckv_ref,
                   aqt_ref, bqt_ref, vst_ref, vwt_ref, bvt_ref, gt_ref, kpet_ref, *, n_seq_tiles):
    h = _rms(x_ref[...], nrm_ref[...]).astype(BF16)
    zr = _dot(h, wr_ref[...])
    c = c128_ref[...]
    s = s128_ref[...]
    tm, dh = c.shape[0], HEAD_DIM

    def roped(i):
        return zr[:, 128 * i:128 * (i + 1)] * c + zr[:, _ROW_K + 128 * i:_ROW_K + 128 * (i + 1)] * s

    pos = (pl.program_id(0) % n_seq_tiles) * tm + lax.broadcasted_iota(jnp.int32, (tm, 128), 0)
    lane = lax.broadcasted_iota(jnp.int32, (tm, 128), 1)
    onehot = (lane - dh == (pos // A_SEL_LEN) % BLOCKS_PER_TILE).astype(F32)
    kc = roped(0)
    bk = roped(5)
    o = 2 * _ROW_K
    vc = zr[:, o:o + 128]
    for g in range(A_KV_GROUPS):
        kc_ref[g] = kc[:, g * dh:(g + 1) * dh]
        vc_ref[g] = vc[:, g * dh:(g + 1) * dh]
        bk_ref[g] = bk[:, g * dh:(g + 1) * dh].astype(BF16)
        ks_ref[g] = (roped(1 + g) + onehot).astype(BF16)
        kw_ref[g] = roped(3 + g).astype(BF16)
    cq_ref[...] = zr[:, o + 128:o + 128 + C_Q_RANK]
    ckv_ref[...] = zr[:, o + 128 + C_Q_RANK:o + 128 + C_Q_RANK + C_KV_RANK]

    zt = _dot_nt(wt_ref[...], h)
    ct = c64t_ref[...]
    st = s64t_ref[...]
    qscale = HEAD_DIM ** -0.5 * LOG2E
    for q_ref, base in ((aqt_ref, _T_AQ), (bqt_ref, _T_BQ)):
        for hh in range(8):
            r0 = base + hh * HEAD_DIM
            x1 = zt[r0:r0 + 32]
            x2 = zt[r0 + 32:r0 + 64]
            q_ref[hh * 64:hh * 64 + 32, :] = ((x1 * ct - x2 * st) * qscale).astype(BF16)
            q_ref[hh * 64 + 32:hh * 64 + 64, :] = ((x2 * ct + x1 * st) * qscale).astype(BF16)
    _store_v_aug(vst_ref, zt[_T_VS:_T_VS + 128], A_KV_GROUPS)
    _store_v_aug(vwt_ref, zt[_T_VW:_T_VW + 128], A_KV_GROUPS)
    _store_v_aug(bvt_ref, zt[_T_BV:_T_BV + 128], B_KV_HEADS)
    gt_ref[...] = _sigmoid(zt[_T_G:_T_G + 2 * _GATE_ROWS])
    c3 = c32t_ref[...]
    s3 = s32t_ref[...]
    x1 = zt[_T_KPE:_T_KPE + 16]
    x2 = zt[_T_KPE + 16:_T_KPE + 32]
    kpet_ref[0:16, :] = x1 * c3 - x2 * s3
    kpet_ref[16:32, :] = x2 * c3 + x1 * s3


def _rot_cols(w, dim):
    k, n = w.shape
    w4 = w.reshape(k, n // dim, 2, dim // 2)
    return jnp.concatenate([-w4[:, :, 1:2], w4[:, :, 0:1]], axis=2).reshape(k, n)


def _inproj(x, mix_norm, w_in, tables, tm, B, S):
    T = x.shape[0]
    G, dh = A_KV_GROUPS, HEAD_DIM
    c128, s128, c64t, s64t, c32t, s32t = tables
    sizes = (512, 128, 128, 128, 128, 128, 128, 24, 512, 128, 128, 256, 256, 32)
    offs = np.concatenate([[0], np.cumsum(sizes)])
    (w_aq, w_kc, w_vc, w_ks, w_vs, w_kw, w_vw, w_g, w_bq, w_bk, w_bv, w_cq, w_ckv, w_kpe) = [
        w_in[:, int(offs[i]):int(offs[i + 1])] for i in range(len(sizes))]

    def spread(w):
        return jnp.pad(w.reshape(D_MODEL, G, dh), ((0, 0), (0, 0), (0, dh))).reshape(D_MODEL, G * 2 * dh)

    kmain = jnp.concatenate([w_kc, spread(w_ks), spread(w_kw), w_bk], axis=1)
    w_row = jnp.concatenate([kmain, _rot_cols(kmain, HEAD_DIM), w_vc, w_cq, w_ckv], axis=1).astype(BF16)
    wg = w_g.reshape(D_MODEL, A_KV_GROUPS, A_GROUP_HEADS * 3)
    wg = jnp.pad(wg, ((0, 0), (0, 0), (0, _GATE_ROWS - A_GROUP_HEADS * 3))).reshape(D_MODEL, 2 * _GATE_ROWS)
    w_t = jnp.concatenate([w_aq, w_bq, w_vs, w_vw, w_bv, wg, w_kpe], axis=1).T.astype(BF16)

    row = lambda n: pl.BlockSpec((tm, n), lambda i: (i, 0))
    colT = lambda n: pl.BlockSpec((n, tm), lambda i: (0, i))
    full = lambda a: pl.BlockSpec(a.shape, lambda i: (0,) * a.ndim)
    nrm = mix_norm.reshape(1, D_MODEL)
    vrows = A_KV_GROUPS * (HEAD_DIM + V_AUG)
    nst = S // tm
    grp = lambda w: pl.BlockSpec((G, tm, w), lambda i: (i // nst, i % nst, 0))
    outs = pl.pallas_call(
        functools.partial(_inproj_kernel, n_seq_tiles=nst),
        grid=(T // tm,),
        in_specs=[row(D_MODEL), full(nrm), full(w_row), full(w_t), row(128), row(128),
                  colT(32), colT(32), colT(16), colT(16)],
        out_specs=[grp(dh), grp(dh + K_AUG), grp(dh + K_AUG), grp(dh), grp(dh), row(C_Q_RANK), row(C_KV_RANK),
                   colT(512), colT(512), colT(vrows), colT(vrows), colT(vrows), colT(2 * _GATE_ROWS), colT(32)],
        out_shape=[jax.ShapeDtypeStruct((B * G, S, dh), F32), jax.ShapeDtypeStruct((B * G, S, dh + K_AUG), BF16),
                   jax.ShapeDtypeStruct((B * G, S, dh + K_AUG), BF16), jax.ShapeDtypeStruct((B * G, S, dh), BF16),
                   jax.ShapeDtypeStruct((B * G, S, dh), F32), jax.ShapeDtypeStruct((T, C_Q_RANK), F32),
                   jax.ShapeDtypeStruct((T, C_KV_RANK), F32),
                   jax.ShapeDtypeStruct((512, T), BF16), jax.ShapeDtypeStruct((512, T), BF16),
                   jax.ShapeDtypeStruct((vrows, T), BF16), jax.ShapeDtypeStruct((vrows, T), BF16),
                   jax.ShapeDtypeStruct((vrows, T), BF16), jax.ShapeDtypeStruct((2 * _GATE_ROWS, T), F32),
                   jax.ShapeDtypeStruct((32, T), F32)],
        compiler_params=_cparams(("arbitrary",)),
        name="in_projection",
    )(x, nrm, w_row, w_t, c128, s128, c64t, s64t, c32t, s32t)
    return outs


def _gelu_tanh(x):
    return 0.5 * x * (1.0 + jnp.tanh(math.sqrt(2.0 / math.pi) * (x + 0.044715 * (x * x * x))))


def _cmp_kernel(seg_ref, pos_ref, w1_ref, w2_ref, w2t_ref, o_ref, ot_ref):
    seg = seg_ref[0, 0]
    n_seg = seg.shape[0]
    half = seg.shape[1]
    a = (seg + pos_ref[0, 0:1, :]).astype(BF16)
    b = (seg + pos_ref[0, 1:2, :]).astype(BF16)
    u = _dot(a, w1_ref[0, 0:half, :])
    v = _dot(b, w1_ref[0, half:2 * half, :])
    pre = u + pltpu.roll(v, n_seg - 1, 0)
    g = _gelu_tanh(pre).astype(BF16)
    o_ref[0, 0] = _dot(g, w2_ref[0]).astype(BF16)
    ot_ref[0, 0] = _dot_nt(w2t_ref[0], g).astype(BF16)


def _compress(kc, vc, pos_k, w1_k, w2_k, pos_v, w1_v, w2_v, B, S):
    G, dh = A_KV_GROUPS, HEAD_DIM
    n_seg = S // A_CMP_STRIDE
    seg_w = A_CMP_STRIDE * dh

    seg = jnp.stack([kc.reshape(B * G, n_seg, seg_w), vc.reshape(B * G, n_seg, seg_w)])
    pos = jnp.stack([pos_k.reshape(2, seg_w), pos_v.reshape(2, seg_w)])
    w1 = jnp.stack([w1_k, w1_v]).astype(BF16)
    w2 = jnp.stack([w2_k, w2_v]).astype(BF16)
    w2t = jnp.stack([w2_k.T, w2_v.T]).astype(BF16)
    per = lambda shape: pl.BlockSpec((1,) + shape, lambda w, i: (w,) + (0,) * len(shape))
    o, ot = pl.pallas_call(
        _cmp_kernel,
        grid=(2, B * G),
        in_specs=[pl.BlockSpec((1, 1, n_seg, seg_w), lambda w, i: (w, i, 0, 0)),
                  per((2, seg_w)), per((2 * seg_w, A_CMP_HIDDEN)), per((A_CMP_HIDDEN, dh)),
                  per((dh, A_CMP_HIDDEN))],
        out_specs=[pl.BlockSpec((1, 1, n_seg, dh), lambda w, i: (w, i, 0, 0)),
                   pl.BlockSpec((1, 1, dh, n_seg), lambda w, i: (w, i, 0, 0))],
        out_shape=[jax.ShapeDtypeStruct((2, B * G, n_seg, dh), BF16),
                   jax.ShapeDtypeStruct((2, B * G, dh, n_seg), BF16)],
        compiler_params=_cparams(("arbitrary", "arbitrary")),
        name="nsa_compress",
    )(seg, pos, w1, w2, w2t)
    return o[0], ot[1]


def _lane_heads(blk, dh, n):
    return jnp.concatenate([blk[r * dh:(r + 1) * dh] for r in range(n)], axis=1)


def _argmax_lowest(val, idx):
    n = val.shape[0]
    v = [val[i:i + 8] for i in range(0, n, 8)]
    j = [idx[i:i + 8] for i in range(0, n, 8)]
    while len(v) > 1:
        nv, nj = [], []
        for a in range(0, len(v) - 1, 2):
            take_b = v[a + 1] > v[a]
            nv.append(jnp.where(take_b, v[a + 1], v[a]))
            nj.append(jnp.where(take_b, j[a + 1], j[a]))
        if len(v) % 2:
            nv.append(v[-1])
            nj.append(j[-1])
        v, j = nv, nj
    vmax = jnp.max(v[0], axis=0, keepdims=True)
    return jnp.min(jnp.where(v[0] == vmax, j[0], float(n)), axis=0, keepdims=True)


def _nsa_select(g, q_ref, kc_ref, vct_ref, smt_ref, selb_ref, qa_ref, tq, tq1):
    QC, R, dh = A_Q_CHUNK, A_GROUP_HEADS, HEAD_DIM
    NQ = R * QC
    qT = _lane_heads(q_ref[g * R * dh:(g + 1) * R * dh, :], dh, R)
    flag_rows = jnp.where(lax.broadcasted_iota(jnp.int32, (K_AUG, NQ), 0) == K_PAD_LANE, NEG_INF, 0.0)
    qa_ref[g, 0:dh, :] = qT
    qa_ref[g, dh:dh + K_AUG, :] = flag_rows.astype(BF16)

    n_seg = kc_ref.shape[1]
    s = _dot(kc_ref[g], qT)
    cend = lax.broadcasted_iota(jnp.int32, (n_seg, 1), 0) * A_CMP_STRIDE + (A_CMP_LEN - 1)
    s = jnp.where(cend <= tq, s, NEG_INF)
    m = jnp.max(s, axis=0, keepdims=True)
    e = jnp.exp2(s - m)
    l = jnp.sum(e, axis=0, keepdims=True)
    inv_l = jnp.where(tq >= A_CMP_LEN - 1, 1.0 / l, 0.0)
    p = e * inv_l
    o_cmp = _dot(vct_ref[g], p.astype(BF16))

    psum = p[:, 0:QC]
    for r in range(1, R):
        psum = psum + p[:, r * QC:(r + 1) * QC]
    hi = psum.astype(BF16)
    lo = (psum - hi.astype(F32)).astype(BF16)
    smt = smt_ref[...]
    imp = _dot(smt, hi) + _dot(smt, lo)
    n_sel = imp.shape[0]
    jid = lax.broadcasted_iota(jnp.int32, (n_sel, QC), 0).astype(F32)
    cur = (tq1 // A_SEL_LEN).astype(F32)
    valid = jid <= cur
    forced = (jid == 0.0) | (jid == cur) | (jid == cur - 1.0)

    ineligible, picked = -3.0e38, -2.0e38
    val = jnp.where(valid & jnp.logical_not(forced), imp, ineligible)
    for _ in range(min(A_TOPK, n_sel) - 3):
        first = _argmax_lowest(val, jid)
        val = jnp.where(jid == first, picked, val)
    selb_ref[g] = jnp.where((forced | (val == picked)) & valid, 0.0, NEG_INF)
    return o_cmp


def _nsa_kernel(q_ref, kc_ref, vct_ref, smt_ref, ks_ref, vst_ref, kw_ref, vwt_ref, g_ref,
                o_ref, selb_ref, qa_ref, sa_ref, sb_ref):
    QC, R, dh, G = A_Q_CHUNK, A_GROUP_HEADS, HEAD_DIM, A_KV_GROUPS
    NQ = R * QC
    VR = dh + V_AUG
    c = pl.program_id(1)
    t0 = c * QC
    tq = t0 + (lax.broadcasted_iota(jnp.int32, (1, NQ), 1) & (QC - 1))
    tq1 = t0 + lax.broadcasted_iota(jnp.int32, (1, QC), 1)
    o_cmp = [_nsa_select(g, q_ref, kc_ref, vct_ref, smt_ref, selb_ref, qa_ref, tq, tq1) for g in range(G)]

    gt = g_ref[...]
    WL = A_WINDOW + QC
    io = lax.broadcasted_iota(jnp.int32, (QC, QC), 0)
    qo = lax.broadcasted_iota(jnp.int32, (QC, QC), 1)
    lo_bias = jnp.concatenate([jnp.where(io > qo, 0.0, NEG_INF)] * R, axis=1)
    hi_bias = jnp.concatenate([jnp.where(io <= qo, 0.0, NEG_INF)] * R, axis=1)
    o_cw = []
    for g in range(G):
        s_w = _dot(kw_ref[g, pl.ds(pl.multiple_of(t0, QC), WL), :], qa_ref[g])
        s_w = jnp.concatenate([s_w[0:QC] + lo_bias, s_w[QC:WL - QC], s_w[WL - QC:WL] + hi_bias], axis=0)
        m_w = jnp.max(s_w, axis=0, keepdims=True)
        p_w = jnp.exp2(s_w - m_w)
        acc_w = _dot(vwt_ref[g * VR:(g + 1) * VR, pl.ds(pl.multiple_of(t0, QC), WL)], p_w.astype(BF16))
        o_win = acc_w[0:dh] * (1.0 / acc_w[dh:dh + 1])
        pieces = []
        for r in range(R):
            sl = slice(r * QC, (r + 1) * QC)
            gr = g * _GATE_ROWS + 3 * r
            pieces.append(gt[gr:gr + 1] * o_cmp[g][:, sl] + gt[gr + 2:gr + 3] * o_win[:, sl])
        o_cw.append(jnp.concatenate(pieces, axis=1))

    bpt = BLOCKS_PER_TILE
    KT = sa_ref.shape[1]
    kiota = lax.broadcasted_iota(jnp.int32, (KT, 1), 0)
    pad_rows = jnp.where(lax.broadcasted_iota(jnp.int32, (16 - bpt, NQ), 0) == K_PAD_LANE - bpt,
                         NEG_INF, 0.0)

    zero_rows = jnp.zeros((K_AUG - 16, NQ), BF16)

    def scores(t, dst):
        k0 = pl.multiple_of(t * KT, KT)
        for g in range(G):
            b8 = selb_ref[g, pl.ds(pl.multiple_of((t // 2) * bpt, bpt), bpt), :]
            rows = jnp.concatenate([jnp.concatenate([b8] * R, axis=1), pad_rows], axis=0).astype(BF16)
            rhs = jnp.concatenate([qa_ref[g, 0:dh, :], rows, zero_rows], axis=0)
            dst[g] = _dot(ks_ref[g, pl.ds(k0, KT), :], rhs)

    def update(t, src, carry, causal):
        k0 = pl.multiple_of(t * KT, KT)
        out = []
        for g in range(G):
            m_i, acc = carry[2 * g], carry[2 * g + 1]
            s_i = src[g]
            if causal:
                cb = jnp.where((k0 + kiota) <= tq1, 0.0, NEG_INF)
                s_i = s_i + jnp.concatenate([cb] * R, axis=1)
            m_new = jnp.maximum(m_i, jnp.max(s_i, axis=0, keepdims=True))
            alpha = jnp.exp2(m_i - m_new)
            p_i = jnp.exp2(s_i - m_new)
            acc = alpha * acc + _dot(vst_ref[g * VR:(g + 1) * VR, pl.ds(k0, KT)], p_i.astype(BF16))
            out.extend((m_new, acc))
        return tuple(out)

    def sel_step(j, carry):
        scores(2 * j + 1, sb_ref)
        carry = update(2 * j, sa_ref, carry, False)
        scores(2 * j + 2, sa_ref)
        return update(2 * j + 1, sb_ref, carry, False)

    n_tiles = (t0 + QC + 2 * KT - 1) // (2 * KT)
    scores(0, sa_ref)
    init = (jnp.full((1, NQ), NEG_INF, F32), jnp.zeros((VR, NQ), F32)) * G
    carry = _tile_loop(n_tiles - 1, sel_step, init)
    last = n_tiles - 1
    scores(2 * last + 1, sb_ref)
    carry = update(2 * last, sa_ref, carry, True)
    carry = update(2 * last + 1, sb_ref, carry, True)

    outs = []
    for g in range(G):
        acc_s = carry[2 * g + 1]
        o_sel = acc_s[0:dh] * (1.0 / acc_s[dh:dh + 1])
        for r in range(R):
            sl = slice(r * QC, (r + 1) * QC)
            gr = g * _GATE_ROWS + 3 * r
            outs.append(o_cw[g][:, sl] + gt[gr + 1:gr + 2] * o_sel[:, sl])
    o_ref[...] = jnp.concatenate(outs, axis=0).astype(o_ref.dtype)


def _sel_map_t(S):
    n_cmp = (S - A_CMP_LEN) // A_CMP_STRIDE + 1
    n_seg = S // A_CMP_STRIDE
    n_sel = S // A_SEL_LEN
    tok = np.arange(n_cmp)[:, None] * A_CMP_STRIDE + np.arange(A_CMP_LEN)[None, :]
    sm = np.zeros((n_seg, n_sel), np.float32)
    np.add.at(sm, (np.repeat(np.arange(n_cmp), A_CMP_LEN), (tok // A_SEL_LEN).reshape(-1)), 1.0 / A_CMP_LEN)
    return jnp.asarray(sm.T, BF16)


def _window_keys(kw, B):
    G, dh, W = A_KV_GROUPS, HEAD_DIM, A_WINDOW
    front = jnp.asarray(np.arange(dh + K_AUG) == dh + K_PAD_LANE, BF16)
    return jnp.concatenate([jnp.broadcast_to(front, (B * G, W, dh + K_AUG)), kw], axis=1)


def _nsa(aqt, k_cmp, v_cmp_t, ks_aug, vst, kw_aug, vwt, gt, B, S):
    G, R, dh, QC, W = A_KV_GROUPS, A_GROUP_HEADS, HEAD_DIM, A_Q_CHUNK, A_WINDOW
    T = B * S
    nq = S // QC
    n_seg = S // A_CMP_STRIDE
    n_sel = S // A_SEL_LEN
    VR = dh + V_AUG
    smt = _sel_map_t(S)
    vwt_pad = jnp.pad(vwt.reshape(G * VR, B, S), ((0, 0), (0, 0), (W, 0))).reshape(G * VR, B * (S + W))
    qspec = pl.BlockSpec((G * R * dh, QC), lambda b, c: (0, b * nq + c))
    return pl.pallas_call(
        _nsa_kernel,
        grid=(B, nq),
        in_specs=[qspec,
                  pl.BlockSpec((G, n_seg, dh), lambda b, c: (b, 0, 0)),
                  pl.BlockSpec((G, dh, n_seg), lambda b, c: (b, 0, 0)),
                  pl.BlockSpec((n_sel, n_seg), lambda b, c: (0, 0)),
                  pl.BlockSpec((G, S, dh + K_AUG), lambda b, c: (b, 0, 0)),
                  pl.BlockSpec((G * VR, S), lambda b, c: (0, b)),
                  pl.BlockSpec((G, S + W, dh + K_AUG), lambda b, c: (b, 0, 0)),
                  pl.BlockSpec((G * VR, S + W), lambda b, c: (0, b)),
                  pl.BlockSpec((G * _GATE_ROWS, QC), lambda b, c: (0, b * nq + c))],
        out_specs=qspec,
        out_shape=jax.ShapeDtypeStruct((G * R * dh, T), BF16),
        scratch_shapes=[pltpu.VMEM((G, n_sel, QC), F32), pltpu.VMEM((G, dh + K_AUG, R * QC), BF16),
                        pltpu.VMEM((G, KV_TILE // 2, R * QC), F32), pltpu.VMEM((G, KV_TILE // 2, R * QC), F32)],
        compiler_params=_cparams(("arbitrary", "arbitrary")),
        name="nsa_attention",
    )(aqt, k_cmp, v_cmp_t, smt, ks_aug, vst, kw_aug, vwt_pad, gt)


SWA_BLOCKS_PER_STEP = 4


def _swa_kernel(q_ref, kp_ref, kc_ref, vpt_ref, vct_ref, sink_ref, o_ref):
    BLK, G, R, dh, NB = B_BLOCK, B_KV_HEADS, B_HEADS // B_KV_HEADS, HEAD_DIM, SWA_BLOCKS_PER_STEP
    NQ = R * BLK
    VR = dh + V_AUG
    n = pl.program_id(1)
    kj = lax.broadcasted_iota(jnp.int32, (2 * BLK, 1), 0) - BLK
    qi = lax.broadcasted_iota(jnp.int32, (1, NQ), 1) & (BLK - 1)
    band = (kj <= qi) & (kj > qi - B_WINDOW)
    for g in range(G):
        sink = sink_ref[g] * LOG2E
        for j in range(NB):
            cols = slice(j * BLK, (j + 1) * BLK)
            qT = _lane_heads(q_ref[g * R * dh:(g + 1) * R * dh, cols], dh, R)
            if j == 0:
                k = jnp.concatenate([kp_ref[g], kc_ref[g, 0:BLK, :]], axis=0)
                vT = jnp.concatenate([vpt_ref[g * VR:(g + 1) * VR, :], vct_ref[g * VR:(g + 1) * VR, cols]], axis=1)
                ok = band & ((n > 0) | (kj >= 0))
            else:
                k = kc_ref[g, (j - 1) * BLK:(j + 1) * BLK, :]
                vT = vct_ref[g * VR:(g + 1) * VR, (j - 1) * BLK:(j + 1) * BLK]
                ok = band
            s = jnp.where(ok, _dot(k, qT), NEG_INF)
            m = jnp.maximum(jnp.max(s, axis=0, keepdims=True), sink)
            p = jnp.exp2(s - m)
            acc = _dot(vT, p.astype(BF16))
            o = acc[0:dh] * (1.0 / (acc[dh:dh + 1] + jnp.exp2(sink - m)))
            for r in range(R):
                row0 = (g * R + r) * dh
                o_ref[row0:row0 + dh, cols] = o[:, r * BLK:(r + 1) * BLK].astype(o_ref.dtype)


def _swa(bqt, bk, bvt, sinks, B, S):
    G, R, dh, BLK = B_KV_HEADS, B_HEADS // B_KV_HEADS, HEAD_DIM, B_BLOCK
    T = B * S
    nb = S // BLK
    sink_cols = jnp.repeat(sinks.astype(F32).reshape(G, R), BLK, axis=1).reshape(G, 1, R * BLK)
    NB = min(SWA_BLOCKS_PER_STEP, nb)
    ns = nb // NB
    prev = lambda n: jnp.maximum(n * NB - 1, 0)
    VR = dh + V_AUG
    qspec = pl.BlockSpec((G * R * dh, NB * BLK), lambda b, n: (0, b * ns + n))
    return pl.pallas_call(
        _swa_kernel,
        grid=(B, ns),
        in_specs=[qspec,
                  pl.BlockSpec((G, BLK, dh), lambda b, n: (b, prev(n), 0)),
                  pl.BlockSpec((G, NB * BLK, dh), lambda b, n: (b, n, 0)),
                  pl.BlockSpec((G * VR, BLK), lambda b, n: (0, b * nb + prev(n))),
                  pl.BlockSpec((G * VR, NB * BLK), lambda b, n: (0, b * ns + n)),
                  pl.BlockSpec((G, 1, R * BLK), lambda b, n: (0, 0, 0))],
        out_specs=qspec,
        out_shape=jax.ShapeDtypeStruct((G * R * dh, T), BF16),
        compiler_params=_cparams(("arbitrary", "arbitrary")),
        name="swa_sink_attention",
    )(bqt, bk, bk, bvt, bvt, sink_cols)


def _mla_proj_kernel(cq_ref, ckv_ref, kpet_ref, qn_ref, kvn_ref, wqt_ref, wkvt_ref, c_ref, s_ref,
                     qf_ref, kf_ref, vt_ref):
    hq = _rms(cq_ref[...], qn_ref[...]).astype(BF16)
    hkv = _rms(ckv_ref[...], kvn_ref[...]).astype(BF16)
    qt = _dot_nt(wqt_ref[...], hq)
    kvt = _dot_nt(wkvt_ref[...], hkv)
    c = c_ref[...]
    s = s_ref[...]
    kpe = kpet_ref[...]
    scale = (C_NOPE + C_ROPE) ** -0.5 * LOG2E
    half = C_ROPE // 2
    tm = qt.shape[1]
    zpad = jnp.zeros((C_QK_PAD - C_NOPE - C_ROPE, tm), BF16)
    zpad_f32 = jnp.zeros((C_QK_PAD - C_NOPE - C_ROPE, tm), F32)
    ones_row = (lax.broadcasted_iota(jnp.int32, (V_AUG, tm), 0) == 0).astype(BF16)
    for h in range(C_HEADS):
        b = h * C_QK_PAD
        qf_ref[h, 0:C_NOPE, :] = (qt[b:b + C_NOPE] * scale).astype(BF16)
        x1 = qt[b + C_NOPE:b + C_NOPE + half]
        x2 = qt[b + C_NOPE + half:b + C_NOPE + C_ROPE]
        qf_ref[h, C_NOPE:C_NOPE + half, :] = ((x1 * c - x2 * s) * scale).astype(BF16)
        qf_ref[h, C_NOPE + half:C_NOPE + C_ROPE, :] = ((x2 * c + x1 * s) * scale).astype(BF16)
        qf_ref[h, C_NOPE + C_ROPE:C_QK_PAD, :] = zpad
        kft = jnp.concatenate([kvt[b:b + C_NOPE], kpe, zpad_f32], axis=0)
        kf_ref[h] = kft.T.astype(BF16)
        vt_ref[h, 0:C_V, :] = kvt[b + C_NOPE:b + C_NOPE + C_V].astype(BF16)
        vt_ref[h, C_V:C_V + V_AUG, :] = ones_row


def _mla_proj(cq, ckv, kpet, q_norm, w_q_up, kv_norm, w_kv_up, c32t, s32t, tm):
    T = cq.shape[0]
    H = C_HEADS
    wq = w_q_up.reshape(C_Q_RANK, H, C_NOPE + C_ROPE)
    wq = jnp.pad(wq, ((0, 0), (0, 0), (0, C_QK_PAD - C_NOPE - C_ROPE))).reshape(C_Q_RANK, H * C_QK_PAD)
    wqt = wq.T.astype(BF16)
    wkvt = w_kv_up.T.astype(BF16)
    full = lambda a: pl.BlockSpec(a.shape, lambda i: (0,) * a.ndim)
    qn = q_norm.reshape(1, C_Q_RANK)
    kvn = kv_norm.reshape(1, C_KV_RANK)
    return pl.pallas_call(
        _mla_proj_kernel,
        grid=(T // tm,),
        in_specs=[pl.BlockSpec((tm, C_Q_RANK), lambda i: (i, 0)),
                  pl.BlockSpec((tm, C_KV_RANK), lambda i: (i, 0)),
                  pl.BlockSpec((C_ROPE, tm), lambda i: (0, i)),
                  full(qn), full(kvn), full(wqt), full(wkvt),
                  pl.BlockSpec((C_ROPE // 2, tm), lambda i: (0, i)),
                  pl.BlockSpec((C_ROPE // 2, tm), lambda i: (0, i))],
        out_specs=[pl.BlockSpec((H, C_QK_PAD, tm), lambda i: (0, 0, i)),
                   pl.BlockSpec((H, tm, C_QK_PAD), lambda i: (0, i, 0)),
                   pl.BlockSpec((H, C_V + V_AUG, tm), lambda i: (0, 0, i))],
        out_shape=[jax.ShapeDtypeStruct((H, C_QK_PAD, T), BF16),
                   jax.ShapeDtypeStruct((H, T, C_QK_PAD), BF16),
                   jax.ShapeDtypeStruct((H, C_V + V_AUG, T), BF16)],
        compiler_params=_cparams(("arbitrary",)),
        name="mla_projection",
    )(cq, ckv, kpet, qn, kvn, wqt, wkvt, c32t, s32t)


MLA_HEADS_PER_STEP = 2


def _mla_kernel(q_ref, kf_ref, vt_ref, o_ref, sa_ref, sb_ref):
    HB = q_ref.shape[0]
    S = q_ref.shape[2]
    KT, QT = sa_ref.shape[1], sa_ref.shape[2]
    nq = S // QT
    lane = lax.broadcasted_iota(jnp.int32, (1, QT), 1)
    kiota = lax.broadcasted_iota(jnp.int32, (KT, 1), 0)

    def scores(qi, t, dst):
        k0 = pl.multiple_of(t * KT, KT)
        q0 = pl.multiple_of(qi * QT, QT)
        for h in range(HB):
            dst[h] = _dot(kf_ref[h, pl.ds(k0, KT), :], q_ref[h, :, pl.ds(q0, QT)])

    def update(qi, t, src, carry, causal):
        k0 = pl.multiple_of(t * KT, KT)
        out = []
        for h in range(HB):
            m_i, acc = carry[2 * h], carry[2 * h + 1]
            s = src[h]
            if causal:
                s = jnp.where((k0 + kiota) <= qi * QT + lane, s, NEG_INF)
            m_new = jnp.maximum(m_i, jnp.max(s, axis=0, keepdims=True))
            alpha = jnp.exp2(m_i - m_new)
            p = jnp.exp2(s - m_new)
            acc = alpha * acc + _dot(vt_ref[h, :, pl.ds(k0, KT)], p.astype(BF16))
            out.extend((m_new, acc))
        return tuple(out)

    def q_tile(qi, _):
        def step(j, carry):
            scores(qi, 2 * j + 1, sb_ref)
            carry = update(qi, 2 * j, sa_ref, carry, False)
            scores(qi, 2 * j + 2, sa_ref)
            return update(qi, 2 * j + 1, sb_ref, carry, False)

        init = (jnp.full((1, QT), NEG_INF, F32), jnp.zeros((C_V + V_AUG, QT), F32)) * HB
        carry = _tile_loop(qi, step, init)
        scores(qi, 2 * qi + 1, sb_ref)
        carry = update(qi, 2 * qi, sa_ref, carry, True)
        scores(jnp.minimum(qi + 1, nq - 1), 0, sa_ref)
        carry = update(qi, 2 * qi + 1, sb_ref, carry, True)
        q0 = pl.multiple_of(qi * QT, QT)
        for h in range(HB):
            acc = carry[2 * h + 1]
            o_ref[h * C_V:(h + 1) * C_V, pl.ds(q0, QT)] = (acc[0:C_V] * (1.0 / acc[C_V:C_V + 1])).astype(o_ref.dtype)
        return 0

    scores(0, 0, sa_ref)
    lax.fori_loop(0, nq, q_tile, 0)


def _mla(qft, kf, vt, B, S):
    H, HB = C_HEADS, MLA_HEADS_PER_STEP
    T = B * S
    QT = min(KV_TILE, S)
    nq = S // QT
    return pl.pallas_call(
        _mla_kernel,
        grid=(B, H // HB),
        in_specs=[pl.BlockSpec((HB, C_QK_PAD, S), lambda b, h: (h, 0, b)),
                  pl.BlockSpec((HB, S, C_QK_PAD), lambda b, h: (h, b, 0)),
                  pl.BlockSpec((HB, C_V + V_AUG, S), lambda b, h: (h, 0, b))],
        out_specs=pl.BlockSpec((HB * C_V, S), lambda b, h: (h, b)),
        out_shape=jax.ShapeDtypeStruct((H * C_V, T), BF16),
        scratch_shapes=[pltpu.VMEM((HB, QT // 2, QT), F32), pltpu.VMEM((HB, QT // 2, QT), F32)],
        compiler_params=_cparams(("arbitrary", "arbitrary")),
        name="mla_attention",
    )(qft, kf, vt)


def _merge_kernel(x_ref, oa_ref, ob_ref, oc_ref, nrm_ref, wg_ref, wa_ref, wb_ref, wc_ref, wo_ref, o_ref):
    x = x_ref[...]
    h = _rms(x, nrm_ref[...]).astype(BF16)
    g = _sigmoid(_dot(h, wg_ref[...]))
    D = D_MODEL
    m = (g[:, 0:D] * _dot_tn(oa_ref[...], wa_ref[...]) + g[:, D:2 * D] * _dot_tn(ob_ref[...], wb_ref[...])
         + g[:, 2 * D:3 * D] * _dot_tn(oc_ref[...], wc_ref[...]))
    o_ref[...] = x + _dot(m.astype(BF16), wo_ref[...])


def _merge(x, oat, obt, oct, mix_norm, w_gate, w_a, w_b, w_c, w_out, tm):
    T = x.shape[0]
    row = lambda n: pl.BlockSpec((tm, n), lambda i: (i, 0))
    colT = lambda n: pl.BlockSpec((n, tm), lambda i: (0, i))
    full = lambda a: pl.BlockSpec(a.shape, lambda i: (0,) * a.ndim)
    nrm = mix_norm.reshape(1, D_MODEL)
    ws = [w.astype(BF16) for w in (w_gate, w_a, w_b, w_c, w_out)]
    return pl.pallas_call(
        _merge_kernel,
        grid=(T // tm,),
        in_specs=[row(D_MODEL), colT(512), colT(512), colT(512), full(nrm)] + [full(w) for w in ws],
        out_specs=row(D_MODEL),
        out_shape=jax.ShapeDtypeStruct((T, D_MODEL), F32),
        compiler_params=_cparams(("arbitrary",)),
        name="branch_merge",
    )(x, oat, obt, oct, nrm, *ws)


def _ffn_kernel(x_ref, p_ref, fn_ref, wg_ref, wu_ref, wd_ref, pn_ref, wpg_ref, wpe_ref, fin_ref, o_ref,
                *, final):
    x = x_ref[...]
    h = _rms(x, fn_ref[...]).astype(BF16)
    a = _dot(h, wg_ref[...])
    u = _dot(h, wu_ref[...])
    y = (a * _sigmoid(a) * u).astype(BF16)
    x = x + _dot(y, wd_ref[...])
    gate = _sigmoid(_dot(_rms(x, pn_ref[...]).astype(BF16), wpg_ref[...]))
    x = x + gate * _dot(p_ref[...].astype(BF16), wpe_ref[...])
    if final:
        x = _rms(x, fin_ref[...])
    o_ref[...] = x


def _ffn_ple(x, p_i, ffn_norm, w_gate, w_up, w_down, ple_norm, w_ple_gate, w_ple_proj, final_norm, final, tm):
    T = x.shape[0]
    row = lambda n: pl.BlockSpec((tm, n), lambda i: (i, 0))
    full = lambda a: pl.BlockSpec(a.shape, lambda i: (0,) * a.ndim, pipeline_mode=pl.Buffered(1))
    fn = ffn_norm.reshape(1, D_MODEL)
    pn = ple_norm.reshape(1, D_MODEL)
    fin = final_norm.reshape(1, D_MODEL)
    wg, wu, wd, wpg, wpe = [w.astype(BF16) for w in (w_gate, w_up, w_down, w_ple_gate, w_ple_proj)]
    return pl.pallas_call(
        functools.partial(_ffn_kernel, final=final),
        grid=(T // tm,),
        in_specs=[row(D_MODEL), row(PLE_DIM), full(fn), full(wg), full(wu), full(wd), full(pn),
                  full(wpg), full(wpe), full(fin)],
        out_specs=row(D_MODEL),
        out_shape=jax.ShapeDtypeStruct((T, D_MODEL), F32),
        compiler_params=_cparams(("arbitrary",)),
        name="ffn_ple",
    )(x, p_i, fn, wg, wu, wd, pn, wpg, wpe, fin)


def _layer(x, p_i, tables, B, S, final_norm, final, mix_norm, w_in, a_cmp_pos_k, a_cmp_w1_k, a_cmp_w2_k,
           a_cmp_pos_v, a_cmp_w1_v, a_cmp_w2_v, b_sinks, c_q_norm, c_w_q_up, c_kv_norm, c_w_kv_up,
           w_branch_gate, w_branch_a, w_branch_b, w_branch_c, w_out, ffn_norm, w_ffn_gate, w_ffn_up,
           w_ffn_down, ple_norm, w_ple_proj, w_ple_gate):
    T = B * S
    tm = min(512, T)
    (kc, ks_aug, kw, bk, vc, cq, ckv, aqt, bqt, vst, vwt, bvt, gt, kpet) = _inproj(
        x, mix_norm, w_in, tables, tm, B, S)

    k_cmp, v_cmp_t = _compress(kc, vc, a_cmp_pos_k, a_cmp_w1_k, a_cmp_w2_k,
                               a_cmp_pos_v, a_cmp_w1_v, a_cmp_w2_v, B, S)
    oat = _nsa(aqt, k_cmp, v_cmp_t, ks_aug, vst, _window_keys(kw, B), vwt, gt, B, S)
    obt = _swa(bqt, bk, bvt, b_sinks, B, S)

    qft, kf, vt = _mla_proj(cq, ckv, kpet, c_q_norm, c_w_q_up, c_kv_norm, c_w_kv_up,
                            tables[4], tables[5], tm)
    oct = _mla(qft, kf, vt, B, S)

    x = _merge(x, oat, obt, oct, mix_norm, w_branch_gate, w_branch_a, w_branch_b, w_branch_c,
               w_out, min(256, T))
    return _ffn_ple(x, p_i, ffn_norm, w_ffn_gate, w_ffn_up, w_ffn_down, ple_norm, w_ple_gate, w_ple_proj,
                    final_norm, final, min(512, T))


def kernel(x, p, positions, mix_norm, w_in, a_cmp_pos_k, a_cmp_w1_k, a_cmp_w2_k, a_cmp_pos_v, a_cmp_w1_v, a_cmp_w2_v, b_sinks, c_q_norm, c_w_q_up, c_kv_norm, c_w_kv_up, w_branch_gate, w_branch_a, w_branch_b, w_branch_c, w_out, ffn_norm, w_ffn_gate, w_ffn_up, w_ffn_down, ple_norm, w_ple_proj, w_ple_gate, final_norm):
    B, S, D = x.shape
    depth = p.shape[0]
    T = B * S
    tables = _rope_tables(positions, min(512, T))
    xf = x.reshape(T, D)
    per_layer = (mix_norm, w_in, a_cmp_pos_k, a_cmp_w1_k, a_cmp_w2_k, a_cmp_pos_v, a_cmp_w1_v, a_cmp_w2_v,
                 b_sinks, c_q_norm, c_w_q_up, c_kv_norm, c_w_kv_up, w_branch_gate, w_branch_a, w_branch_b,
                 w_branch_c, w_out, ffn_norm, w_ffn_gate, w_ffn_up, w_ffn_down, ple_norm, w_ple_proj,
                 w_ple_gate)
    for i in range(depth):
        xf = _layer(xf, p[i].reshape(T, PLE_DIM), tables, B, S, final_norm, i == depth - 1,
                    *[w[i] for w in per_layer])
    return xf.reshape(B, S, D)
```

```python
import functools
import math

import numpy as np
import jax
import jax.numpy as jnp
from jax import lax
from jax.experimental import pallas as pl
from jax.experimental.pallas import tpu as pltpu

F32 = jnp.float32
BF16 = jnp.bfloat16

D_MODEL = 1024
PLE_DIM = 256
ROPE_THETA = 10000.0
EPS = 1e-6
NEG_INF = -1e30
HEAD_DIM = 64

A_HEADS = 8
A_KV_GROUPS = 2
A_GROUP_HEADS = A_HEADS // A_KV_GROUPS
A_CMP_LEN = 32
A_CMP_STRIDE = 16
A_CMP_HIDDEN = 256
A_SEL_LEN = 64
A_TOPK = 16
A_WINDOW = 512
A_Q_CHUNK = 128
A_FORCE_BONUS = 1e4

B_HEADS = 8
B_KV_HEADS = 2
B_WINDOW = 128
B_BLOCK = 128

C_HEADS = 8
C_Q_RANK = 256
C_KV_RANK = 256
C_NOPE = 64
C_ROPE = 32
C_V = 64
C_QK_PAD = 128

D_FF = int(math.ceil(8 * D_MODEL / 3 / 256)) * 256

LANES = 128
KV_TILE = 512
LOG2E = math.log2(math.e)
V_AUG = 16
K_AUG = 64
K_PAD_LANE = 8
BLOCKS_PER_TILE = KV_TILE // A_SEL_LEN
NSA_PREFIX_CLASSES = 4
VMEM_LIMIT = 56 * 1024 * 1024

_ROW_K = 6 * 128
_ROW_COLS = 2 * _ROW_K + 128 + C_Q_RANK + C_KV_RANK
_T_AQ, _T_BQ = 0, 512
_T_VS, _T_VW, _T_BV = 1024, 1152, 1280
_T_G, _T_KPE = 1408, 1440
_T_ROWS = 1472
_GATE_ROWS = 16


def _cparams(sem):
    return pltpu.CompilerParams(dimension_semantics=sem, vmem_limit_bytes=VMEM_LIMIT)


def _rms(x, g):
    return x * lax.rsqrt(jnp.mean(x * x, axis=-1, keepdims=True) + EPS) * g


def _sigmoid(x):
    return 1.0 / (1.0 + jnp.exp(-x))


def _dot(a, b):
    return jnp.dot(a, b, preferred_element_type=F32)


def _dot_nt(a, b):
    return lax.dot_general(a, b, (((1,), (1,)), ((), ())), preferred_element_type=F32)


TILE_UNROLL = 4


def _tile_loop(n, step, carry):
    main = n // TILE_UNROLL

    def trip(j, c):
        for u in range(TILE_UNROLL):
            c = step(j * TILE_UNROLL + u, c)
        return c

    carry = lax.fori_loop(0, main, trip, carry)
    return lax.fori_loop(main * TILE_UNROLL, n, step, carry)


def _dot_tn(a, b):
    return lax.dot_general(a, b, (((0,), (0,)), ((), ())), preferred_element_type=F32)


def _rope_kernel(pc_ref, pr_ref, i128_ref, i64_ref, i32_ref,
                 c128_ref, s128_ref, c64t_ref, s64t_ref, c32t_ref, s32t_ref):
    ang = pc_ref[...].astype(F32) * i128_ref[...]
    c128_ref[...] = jnp.cos(ang)
    s128_ref[...] = jnp.sin(ang)
    pr = pr_ref[...].astype(F32)
    a64 = i64_ref[...] * pr
    c64t_ref[...] = jnp.cos(a64)
    s64t_ref[...] = jnp.sin(a64)
    a32 = i32_ref[...] * pr
    c32t_ref[...] = jnp.cos(a32)
    s32t_ref[...] = jnp.sin(a32)


def _rope_tables(positions, tm):
    T = positions.size
    pos = positions.reshape(T)
    inv64 = jnp.power(F32(ROPE_THETA), -jnp.arange(0, HEAD_DIM, 2, dtype=F32) / HEAD_DIM)
    inv32 = jnp.power(F32(ROPE_THETA), -jnp.arange(0, C_ROPE, 2, dtype=F32) / C_ROPE)
    inv128 = jnp.tile(inv64, 4).reshape(1, 128)
    full = lambda shape: pl.BlockSpec(shape, lambda i: (0,) * len(shape))
    return pl.pallas_call(
        _rope_kernel,
        grid=(T // tm,),
        in_specs=[pl.BlockSpec((tm, 1), lambda i: (i, 0)),
                  pl.BlockSpec((1, tm), lambda i: (0, i)),
                  full((1, 128)), full((32, 1)), full((16, 1))],
        out_specs=[pl.BlockSpec((tm, 128), lambda i: (i, 0)),
                   pl.BlockSpec((tm, 128), lambda i: (i, 0)),
                   pl.BlockSpec((32, tm), lambda i: (0, i)),
                   pl.BlockSpec((32, tm), lambda i: (0, i)),
                   pl.BlockSpec((16, tm), lambda i: (0, i)),
                   pl.BlockSpec((16, tm), lambda i: (0, i))],
        out_shape=[jax.ShapeDtypeStruct((T, 128), F32), jax.ShapeDtypeStruct((T, 128), F32),
                   jax.ShapeDtypeStruct((32, T), F32), jax.ShapeDtypeStruct((32, T), F32),
                   jax.ShapeDtypeStruct((16, T), F32), jax.ShapeDtypeStruct((16, T), F32)],
        compiler_params=_cparams(("arbitrary",)),
        name="rope_tables",
    )(pos.reshape(T, 1), pos.reshape(1, T), inv128, inv64.reshape(32, 1), inv32.reshape(16, 1))


def _store_v_aug(ref, vt, n_heads):
    dv = vt.shape[0] // n_heads
    tm = vt.shape[1]
    ones_row = (lax.broadcasted_iota(jnp.int32, (V_AUG, tm), 0) == 0).astype(ref.dtype)
    for g in range(n_heads):
        r0 = g * (dv + V_AUG)
        ref[r0:r0 + dv, :] = vt[g * dv:(g + 1) * dv].astype(ref.dtype)
        ref[r0 + dv:r0 + dv + V_AUG, :] = ones_row

def _inproj_kernel(x_ref, nrm_ref, wr_ref, wt_ref, c128_ref, s128_ref, c64t_ref, s64t_ref,
                   c32t_ref, s32t_ref,
                   kc_ref, ks_ref, kw_ref, bk_ref, vc_ref, cq_ref, ckv_ref,
                   aqt_ref, bqt_ref, vst_ref, vwt_ref, bvt_ref, gt_ref, kpet_ref, *, n_seq_tiles):
    h = _rms(x_ref[...], nrm_ref[...]).astype(BF16)
    zr = _dot(h, wr_ref[...])
    c = c128_ref[...]
    s = s128_ref[...]
    tm, dh = c.shape[0], HEAD_DIM

    def roped(i):
        return zr[:, 128 * i:128 * (i + 1)] * c + zr[:, _ROW_K + 128 * i:_ROW_K + 128 * (i + 1)] * s

    pos = (pl.program_id(0) % n_seq_tiles) * tm + lax.broadcasted_iota(jnp.int32, (tm, 128), 0)
    lane = lax.broadcasted_iota(jnp.int32, (tm, 128), 1)
    onehot = (lane - dh == (pos // A_SEL_LEN) % BLOCKS_PER_TILE).astype(F32)
    kc = roped(0)
    bk = roped(5)
    o = 2 * _ROW_K
    vc = zr[:, o:o + 128]
    for g in range(A_KV_GROUPS):
        kc_ref[g] = kc[:, g * dh:(g + 1) * dh]
        vc_ref[g] = vc[:, g * dh:(g + 1) * dh]
        bk_ref[g] = bk[:, g * dh:(g + 1) * dh].astype(BF16)
        ks_ref[g] = (roped(1 + g) + onehot).astype(BF16)
        kw_ref[g] = roped(3 + g).astype(BF16)
    cq_ref[...] = zr[:, o + 128:o + 128 + C_Q_RANK]
    ckv_ref[...] = zr[:, o + 128 + C_Q_RANK:o + 128 + C_Q_RANK + C_KV_RANK]

    zt = _dot_nt(wt_ref[...], h)
    ct = c64t_ref[...]
    st = s64t_ref[...]
    qscale = HEAD_DIM ** -0.5 * LOG2E
    for q_ref, base in ((aqt_ref, _T_AQ), (bqt_ref, _T_BQ)):
        for hh in range(8):
            r0 = base + hh * HEAD_DIM
            x1 = zt[r0:r0 + 32]
            x2 = zt[r0 + 32:r0 + 64]
            q_ref[hh * 64:hh * 64 + 32, :] = ((x1 * ct - x2 * st) * qscale).astype(BF16)
            q_ref[hh * 64 + 32:hh * 64 + 64, :] = ((x2 * ct + x1 * st) * qscale).astype(BF16)
    _store_v_aug(vst_ref, zt[_T_VS:_T_VS + 128], A_KV_GROUPS)
    _store_v_aug(vwt_ref, zt[_T_VW:_T_VW + 128], A_KV_GROUPS)
    _store_v_aug(bvt_ref, zt[_T_BV:_T_BV + 128], B_KV_HEADS)
    gt_ref[...] = _sigmoid(zt[_T_G:_T_G + 2 * _GATE_ROWS])
    c3 = c32t_ref[...]
    s3 = s32t_ref[...]
    x1 = zt[_T_KPE:_T_KPE + 16]
    x2 = zt[_T_KPE + 16:_T_KPE + 32]
    kpet_ref[0:16, :] = x1 * c3 - x2 * s3
    kpet_ref[16:32, :] = x2 * c3 + x1 * s3


def _rot_cols(w, dim):
    k, n = w.shape
    w4 = w.reshape(k, n // dim, 2, dim // 2)
    return jnp.concatenate([-w4[:, :, 1:2], w4[:, :, 0:1]], axis=2).reshape(k, n)


def _inproj(x, mix_norm, w_in, tables, tm, B, S):
    T = x.shape[0]
    G, dh = A_KV_GROUPS, HEAD_DIM
    c128, s128, c64t, s64t, c32t, s32t = tables
    sizes = (512, 128, 128, 128, 128, 128, 128, 24, 512, 128, 128, 256, 256, 32)
    offs = np.concatenate([[0], np.cumsum(sizes)])
    (w_aq, w_kc, w_vc, w_ks, w_vs, w_kw, w_vw, w_g, w_bq, w_bk, w_bv, w_cq, w_ckv, w_kpe) = [
        w_in[:, int(offs[i]):int(offs[i + 1])] for i in range(len(sizes))]

    def spread(w):
        return jnp.pad(w.reshape(D_MODEL, G, dh), ((0, 0), (0, 0), (0, dh))).reshape(D_MODEL, G * 2 * dh)

    kmain = jnp.concatenate([w_kc, spread(w_ks), spread(w_kw), w_bk], axis=1)
    w_row = jnp.concatenate([kmain, _rot_cols(kmain, HEAD_DIM), w_vc, w_cq, w_ckv], axis=1).astype(BF16)
    wg = w_g.reshape(D_MODEL, A_KV_GROUPS, A_GROUP_HEADS * 3)
    wg = jnp.pad(wg, ((0, 0), (0, 0), (0, _GATE_ROWS - A_GROUP_HEADS * 3))).reshape(D_MODEL, 2 * _GATE_ROWS)
    w_t = jnp.concatenate([w_aq, w_bq, w_vs, w_vw, w_bv, wg, w_kpe], axis=1).T.astype(BF16)

    row = lambda n: pl.BlockSpec((tm, n), lambda i: (i, 0))
    colT = lambda n: pl.BlockSpec((n, tm), lambda i: (0, i))
    full = lambda a: pl.BlockSpec(a.shape, lambda i: (0,) * a.ndim)
    nrm = mix_norm.reshape(1, D_MODEL)
    vrows = A_KV_GROUPS * (HEAD_DIM + V_AUG)
    nst = S // tm
    grp = lambda w: pl.BlockSpec((G, tm, w), lambda i: (i // nst, i % nst, 0))
    outs = pl.pallas_call(
        functools.partial(_inproj_kernel, n_seq_tiles=nst),
        grid=(T // tm,),
        in_specs=[row(D_MODEL), full(nrm), full(w_row), full(w_t), row(128), row(128),
                  colT(32), colT(32), colT(16), colT(16)],
        out_specs=[grp(dh), grp(dh + K_AUG), grp(dh + K_AUG), grp(dh), grp(dh), row(C_Q_RANK), row(C_KV_RANK),
                   colT(512), colT(512), colT(vrows), colT(vrows), colT(vrows), colT(2 * _GATE_ROWS), colT(32)],
        out_shape=[jax.ShapeDtypeStruct((B * G, S, dh), F32), jax.ShapeDtypeStruct((B * G, S, dh + K_AUG), BF16),
                   jax.ShapeDtypeStruct((B * G, S, dh + K_AUG), BF16), jax.ShapeDtypeStruct((B * G, S, dh), BF16),
                   jax.ShapeDtypeStruct((B * G, S, dh), F32), jax.ShapeDtypeStruct((T, C_Q_RANK), F32),
                   jax.ShapeDtypeStruct((T, C_KV_RANK), F32),
                   jax.ShapeDtypeStruct((512, T), BF16), jax.ShapeDtypeStruct((512, T), BF16),
                   jax.ShapeDtypeStruct((vrows, T), BF16), jax.ShapeDtypeStruct((vrows, T), BF16),
                   jax.ShapeDtypeStruct((vrows, T), BF16), jax.ShapeDtypeStruct((2 * _GATE_ROWS, T), F32),
                   jax.ShapeDtypeStruct((32, T), F32)],
        compiler_params=_cparams(("arbitrary",)),
        name="in_projection",
    )(x, nrm, w_row, w_t, c128, s128, c64t, s64t, c32t, s32t)
    return outs


def _gelu_tanh(x):
    return 0.5 * x * (1.0 + jnp.tanh(math.sqrt(2.0 / math.pi) * (x + 0.044715 * (x * x * x))))


def _cmp_kernel(seg_ref, pos_ref, w1_ref, w2_ref, w2t_ref, o_ref, ot_ref):
    seg = seg_ref[0, 0]
    n_seg = seg.shape[0]
    half = seg.shape[1]
    a = (seg + pos_ref[0, 0:1, :]).astype(BF16)
    b = (seg + pos_ref[0, 1:2, :]).astype(BF16)
    u = _dot(a, w1_ref[0, 0:half, :])
    v = _dot(b, w1_ref[0, half:2 * half, :])
    pre = u + pltpu.roll(v, n_seg - 1, 0)
    g = _gelu_tanh(pre).astype(BF16)
    o_ref[0, 0] = _dot(g, w2_ref[0]).astype(BF16)
    ot_ref[0, 0] = _dot_nt(w2t_ref[0], g).astype(BF16)


def _compress(kc, vc, pos_k, w1_k, w2_k, pos_v, w1_v, w2_v, B, S):
    G, dh = A_KV_GROUPS, HEAD_DIM
    n_seg = S // A_CMP_STRIDE
    seg_w = A_CMP_STRIDE * dh

    seg = jnp.stack([kc.reshape(B * G, n_seg, seg_w), vc.reshape(B * G, n_seg, seg_w)])
    pos = jnp.stack([pos_k.reshape(2, seg_w), pos_v.reshape(2, seg_w)])
    w1 = jnp.stack([w1_k, w1_v]).astype(BF16)
    w2 = jnp.stack([w2_k, w2_v]).astype(BF16)
    w2t = jnp.stack([w2_k.T, w2_v.T]).astype(BF16)
    per = lambda shape: pl.BlockSpec((1,) + shape, lambda w, i: (w,) + (0,) * len(shape))
    o, ot = pl.pallas_call(
        _cmp_kernel,
        grid=(2, B * G),
        in_specs=[pl.BlockSpec((1, 1, n_seg, seg_w), lambda w, i: (w, i, 0, 0)),
                  per((2, seg_w)), per((2 * seg_w, A_CMP_HIDDEN)), per((A_CMP_HIDDEN, dh)),
                  per((dh, A_CMP_HIDDEN))],
        out_specs=[pl.BlockSpec((1, 1, n_seg, dh), lambda w, i: (w, i, 0, 0)),
                   pl.BlockSpec((1, 1, dh, n_seg), lambda w, i: (w, i, 0, 0))],
        out_shape=[jax.ShapeDtypeStruct((2, B * G, n_seg, dh), BF16),
                   jax.ShapeDtypeStruct((2, B * G, dh, n_seg), BF16)],
        compiler_params=_cparams(("arbitrary", "arbitrary")),
        name="nsa_compress",
    )(seg, pos, w1, w2, w2t)
    return o[0], ot[1]


def _lane_heads(blk, dh, n):
    return jnp.concatenate([blk[r * dh:(r + 1) * dh] for r in range(n)], axis=1)


def _argmax_lowest(val, idx):
    n = val.shape[0]
    v = [val[i:i + 8] for i in range(0, n, 8)]
    j = [idx[i:i + 8] for i in range(0, n, 8)]
    while len(v) > 1:
        nv, nj = [], []
        for a in range(0, len(v) - 1, 2):
            take_b = v[a + 1] > v[a]
            nv.append(jnp.where(take_b, v[a + 1], v[a]))
            nj.append(jnp.where(take_b, j[a + 1], j[a]))
        if len(v) % 2:
            nv.append(v[-1])
            nj.append(j[-1])
        v, j = nv, nj
    vmax = jnp.max(v[0], axis=0, keepdims=True)
    return jnp.min(jnp.where(v[0] == vmax, j[0], float(n)), axis=0, keepdims=True)


def _nsa_select(g, q_ref, kc_ref, vct_ref, smt_ref, selb_ref, qa_ref, tq, tq1, n_seg, n_sel):
    QC, R, dh = A_Q_CHUNK, A_GROUP_HEADS, HEAD_DIM
    NQ = R * QC
    qT = _lane_heads(q_ref[g * R * dh:(g + 1) * R * dh, :], dh, R)
    flag_rows = jnp.where(lax.broadcasted_iota(jnp.int32, (K_AUG, NQ), 0) == K_PAD_LANE, NEG_INF, 0.0)
    qa_ref[g, 0:dh, :] = qT
    qa_ref[g, dh:dh + K_AUG, :] = flag_rows.astype(BF16)

    s = _dot(kc_ref[g, 0:n_seg, :], qT)
    cend = lax.broadcasted_iota(jnp.int32, (n_seg, 1), 0) * A_CMP_STRIDE + (A_CMP_LEN - 1)
    s = jnp.where(cend <= tq, s, NEG_INF)
    m = jnp.max(s, axis=0, keepdims=True)
    e = jnp.exp2(s - m)
    l = jnp.sum(e, axis=0, keepdims=True)
    inv_l = jnp.where(tq >= A_CMP_LEN - 1, 1.0 / l, 0.0)
    p = e * inv_l
    o_cmp = _dot(vct_ref[g, :, 0:n_seg], p.astype(BF16))

    psum = p[:, 0:QC]
    for r in range(1, R):
        psum = psum + p[:, r * QC:(r + 1) * QC]
    hi = psum.astype(BF16)
    lo = (psum - hi.astype(F32)).astype(BF16)
    smt = smt_ref[0:n_sel, 0:n_seg]
    imp = _dot(smt, hi) + _dot(smt, lo)
    jid =lax.broadcasted_iota(jnp.int32, (n_sel, QC), 0).astype(F32)
    cur = (tq1 // A_SEL_LEN).astype(F32)
    valid = jid <= cur
    forced = (jid == 0.0) | (jid == cur) | (jid == cur - 1.0)

    ineligible, picked = -(2.0 ** 127), -(2.0 ** 126)
    val = jnp.where(valid & jnp.logical_not(forced), imp, ineligible)
    for _ in range(min(A_TOPK, selb_ref.shape[1]) - 3):
        first = _argmax_lowest(val, jid)
        val = jnp.where(jid == first, picked, val)
    selb_ref[g, 0:n_sel, :] = jnp.where((forced | (val == picked)) & valid, 0.0, NEG_INF)
    return o_cmp


def _nsa_kernel(q_ref, kc_ref, vct_ref, smt_ref, ks_ref, vst_ref, kw_ref, vwt_ref, g_ref,
                o_ref, selb_ref, qa_ref, sa_ref, sb_ref):
    QC, R, dh, G = A_Q_CHUNK, A_GROUP_HEADS, HEAD_DIM, A_KV_GROUPS
    NQ = R * QC
    VR = dh + V_AUG
    c = pl.program_id(1)
    t0 = c * QC
    tq = t0 + (lax.broadcasted_iota(jnp.int32, (1, NQ), 1) & (QC - 1))
    tq1 = t0 + lax.broadcasted_iota(jnp.int32, (1, QC), 1)
    gt = g_ref[...]

    def before_key_loop(n_seg, n_sel):
        o_cmp = [_nsa_select(g, q_ref, kc_ref, vct_ref, smt_ref, selb_ref, qa_ref, tq, tq1, n_seg, n_sel)
                 for g in range(G)]
        WL = A_WINDOW + QC
        io = lax.broadcasted_iota(jnp.int32, (QC, QC), 0)
        qo = lax.broadcasted_iota(jnp.int32, (QC, QC), 1)
        lo_bias = jnp.concatenate([jnp.where(io > qo, 0.0, NEG_INF)] * R, axis=1)
        hi_bias = jnp.concatenate([jnp.where(io <= qo, 0.0, NEG_INF)] * R, axis=1)
        o_cw = []
        for g in range(G):
            s_w = _dot(kw_ref[g, pl.ds(pl.multiple_of(t0, QC), WL), :], qa_ref[g])
            s_w = jnp.concatenate([s_w[0:QC] + lo_bias, s_w[QC:WL - QC], s_w[WL - QC:WL] + hi_bias], axis=0)
            m_w = jnp.max(s_w, axis=0, keepdims=True)
            p_w = jnp.exp2(s_w - m_w)
            acc_w = _dot(vwt_ref[g * VR:(g + 1) * VR, pl.ds(pl.multiple_of(t0, QC), WL)], p_w.astype(BF16))
            o_win = acc_w[0:dh] * (1.0 / acc_w[dh:dh + 1])
            pieces = []
            for r in range(R):
                sl = slice(r * QC, (r + 1) * QC)
                gr = g * _GATE_ROWS + 3 * r
                pieces.append(gt[gr:gr + 1] * o_cmp[g][:, sl] + gt[gr + 2:gr + 3] * o_win[:, sl])
            o_cw.append(jnp.concatenate(pieces, axis=1))
        return tuple(o_cw)

    n_seg_all, n_sel_all = kc_ref.shape[1], selb_ref.shape[1]
    n_cls = max(1, min(NSA_PREFIX_CLASSES, n_seg_all // LANES))
    chunks_per_cls = (n_seg_all // (A_Q_CHUNK // A_CMP_STRIDE)) // n_cls
    classes = [functools.partial(before_key_loop, (k + 1) * n_seg_all // n_cls, (k + 1) * n_sel_all // n_cls)
               for k in range(n_cls)]
    o_cw = lax.switch(c // chunks_per_cls, classes) if n_cls > 1 else classes[0]()

    bpt = BLOCKS_PER_TILE
    KT = sa_ref.shape[1]
    kiota = lax.broadcasted_iota(jnp.int32, (KT, 1), 0)
    pad_rows = jnp.where(lax.broadcasted_iota(jnp.int32, (16 - bpt, NQ), 0) == K_PAD_LANE - bpt,
                         NEG_INF, 0.0)

    zero_rows = jnp.zeros((K_AUG - 16, NQ), BF16)

    def scores(t, dst):
        k0 = pl.multiple_of(t * KT, KT)
        for g in range(G):
            b8 = selb_ref[g, pl.ds(pl.multiple_of((t // 2) * bpt, bpt), bpt), :]
            rows = jnp.concatenate([jnp.concatenate([b8] * R, axis=1), pad_rows], axis=0).astype(BF16)
            rhs = jnp.concatenate([qa_ref[g, 0:dh, :], rows, zero_rows], axis=0)
            dst[g] = _dot(ks_ref[g, pl.ds(k0, KT), :], rhs)

    def update(t, src, carry, causal):
        k0 = pl.multiple_of(t * KT, KT)
        out = []
        for g in range(G):
            m_i, acc = carry[2 * g], carry[2 * g + 1]
            s_i = src[g]
            if causal:
                cb = jnp.where((k0 + kiota) <= tq1, 0.0, NEG_INF)
                s_i = s_i + jnp.concatenate([cb] * R, axis=1)
            m_new = jnp.maximum(m_i, jnp.max(s_i, axis=0, keepdims=True))
            alpha = jnp.exp2(m_i - m_new)
            p_i = jnp.exp2(s_i - m_new)
            acc = alpha * acc + _dot(vst_ref[g * VR:(g + 1) * VR, pl.ds(k0, KT)], p_i.astype(BF16))
            out.extend((m_new, acc))
        return tuple(out)

    def sel_step(j, carry):
        scores(2 * j + 1, sb_ref)
        carry = update(2 * j, sa_ref, carry, False)
        scores(2 * j + 2, sa_ref)
        return update(2 * j + 1, sb_ref, carry, False)

    n_tiles = (t0 + QC + 2 * KT - 1) // (2 * KT)
    scores(0, sa_ref)
    init = (jnp.full((1, NQ), NEG_INF, F32), jnp.zeros((VR, NQ), F32)) * G
    carry = _tile_loop(n_tiles - 1, sel_step, init)
    last = n_tiles - 1
    scores(2 * last + 1, sb_ref)
    carry = update(2 * last, sa_ref, carry, True)
    carry = update(2 * last + 1, sb_ref, carry, True)

    outs = []
    for g in range(G):
        acc_s = carry[2 * g + 1]
        o_sel = acc_s[0:dh] * (1.0 / acc_s[dh:dh + 1])
        for r in range(R):
            sl = slice(r * QC, (r + 1) * QC)
            gr = g * _GATE_ROWS + 3 * r
            outs.append(o_cw[g][:, sl] + gt[gr + 1:gr + 2] * o_sel[:, sl])
    o_ref[...] = jnp.concatenate(outs, axis=0).astype(o_ref.dtype)


def _sel_map_t(S):
    n_cmp = (S - A_CMP_LEN) // A_CMP_STRIDE + 1
    n_seg = S // A_CMP_STRIDE
    n_sel = S // A_SEL_LEN
    tok = np.arange(n_cmp)[:, None] * A_CMP_STRIDE + np.arange(A_CMP_LEN)[None, :]
    sm = np.zeros((n_seg, n_sel), np.float32)
    np.add.at(sm, (np.repeat(np.arange(n_cmp), A_CMP_LEN), (tok // A_SEL_LEN).reshape(-1)), 1.0 / A_CMP_LEN)
    return jnp.asarray(sm.T, BF16)


def _window_keys(kw, B):
    G, dh, W = A_KV_GROUPS, HEAD_DIM, A_WINDOW
    front = jnp.asarray(np.arange(dh + K_AUG) == dh + K_PAD_LANE, BF16)
    return jnp.concatenate([jnp.broadcast_to(front, (B * G, W, dh + K_AUG)), kw], axis=1)


def _nsa(aqt, k_cmp, v_cmp_t, ks_aug, vst, kw_aug, vwt, gt, B, S):
    G, R, dh, QC, W = A_KV_GROUPS, A_GROUP_HEADS, HEAD_DIM, A_Q_CHUNK, A_WINDOW
    T = B * S
    nq = S // QC
    n_seg = S // A_CMP_STRIDE
    n_sel = S // A_SEL_LEN
    VR = dh + V_AUG
    smt = _sel_map_t(S)
    vwt_pad = jnp.pad(vwt.reshape(G * VR, B, S), ((0, 0), (0, 0), (W, 0))).reshape(G * VR, B * (S + W))
    qspec = pl.BlockSpec((G * R * dh, QC), lambda b, c: (0, b * nq + c))
    return pl.pallas_call(
        _nsa_kernel,
        grid=(B, nq),
        in_specs=[qspec,
                  pl.BlockSpec((G, n_seg, dh), lambda b, c: (b, 0, 0)),
                  pl.BlockSpec((G, dh, n_seg), lambda b, c: (b, 0, 0)),
                  pl.BlockSpec((n_sel, n_seg), lambda b, c: (0, 0)),
                  pl.BlockSpec((G, S, dh + K_AUG), lambda b, c: (b, 0, 0)),
                  pl.BlockSpec((G * VR, S), lambda b, c: (0, b)),
                  pl.BlockSpec((G, S + W, dh + K_AUG), lambda b, c: (b, 0, 0)),
                  pl.BlockSpec((G * VR, S + W), lambda b, c: (0, b)),
                  pl.BlockSpec((G * _GATE_ROWS, QC), lambda b, c: (0, b * nq + c))],
        out_specs=qspec,
        out_shape=jax.ShapeDtypeStruct((G * R * dh, T), BF16),
        scratch_shapes=[pltpu.VMEM((G, n_sel, QC), F32), pltpu.VMEM((G, dh + K_AUG, R * QC), BF16),
                        pltpu.VMEM((G, KV_TILE // 2, R * QC), F32), pltpu.VMEM((G, KV_TILE // 2, R * QC), F32)],
        compiler_params=_cparams(("arbitrary", "arbitrary")),
        name="nsa_attention",
    )(aqt, k_cmp, v_cmp_t, smt, ks_aug, vst, kw_aug, vwt_pad, gt)


SWA_BLOCKS_PER_STEP = 4


def _swa_kernel(q_ref, kp_ref, kc_ref, vpt_ref, vct_ref, sink_ref, o_ref):
    BLK, G, R, dh, NB = B_BLOCK, B_KV_HEADS, B_HEADS // B_KV_HEADS, HEAD_DIM, SWA_BLOCKS_PER_STEP
    NQ = R * BLK
    VR = dh + V_AUG
    n = pl.program_id(1)
    kj = lax.broadcasted_iota(jnp.int32, (2 * BLK, 1), 0) - BLK
    qi = lax.broadcasted_iota(jnp.int32, (1, NQ), 1) & (BLK - 1)
    band = (kj <= qi) & (kj > qi - B_WINDOW)
    for g in range(G):
        sink = sink_ref[g] * LOG2E
        for j in range(NB):
            cols = slice(j * BLK, (j + 1) * BLK)
            qT = _lane_heads(q_ref[g * R * dh:(g + 1) * R * dh, cols], dh, R)
            if j == 0:
                k = jnp.concatenate([kp_ref[g], kc_ref[g, 0:BLK, :]], axis=0)
                vT = jnp.concatenate([vpt_ref[g * VR:(g + 1) * VR, :], vct_ref[g * VR:(g + 1) * VR, cols]], axis=1)
                ok = band & ((n > 0) | (kj >= 0))
            else:
                k = kc_ref[g, (j - 1) * BLK:(j + 1) * BLK, :]
                vT = vct_ref[g * VR:(g + 1) * VR, (j - 1) * BLK:(j + 1) * BLK]
                ok = band
            s = jnp.where(ok, _dot(k, qT), NEG_INF)
            m = jnp.maximum(jnp.max(s, axis=0, keepdims=True), sink)
            p = jnp.exp2(s - m)
            acc = _dot(vT, p.astype(BF16))
            o = acc[0:dh] * (1.0 / (acc[dh:dh + 1] + jnp.exp2(sink - m)))
            for r in range(R):
                row0 = (g * R + r) * dh
                o_ref[row0:row0 + dh, cols] = o[:, r * BLK:(r + 1) * BLK].astype(o_ref.dtype)


def _swa(bqt, bk, bvt, sinks, B, S):
    G, R, dh, BLK = B_KV_HEADS, B_HEADS // B_KV_HEADS, HEAD_DIM, B_BLOCK
    T = B * S
    nb = S // BLK
    sink_cols = jnp.repeat(sinks.astype(F32).reshape(G, R), BLK, axis=1).reshape(G, 1, R * BLK)
    NB = min(SWA_BLOCKS_PER_STEP, nb)
    ns = nb // NB
    prev = lambda n: jnp.maximum(n * NB - 1, 0)
    VR = dh + V_AUG
    qspec = pl.BlockSpec((G * R * dh, NB * BLK), lambda b, n: (0, b * ns + n))
    return pl.pallas_call(
        _swa_kernel,
        grid=(B, ns),
        in_specs=[qspec,
                  pl.BlockSpec((G, BLK, dh), lambda b, n: (b, prev(n), 0)),
                  pl.BlockSpec((G, NB * BLK, dh), lambda b, n: (b, n, 0)),
                  pl.BlockSpec((G * VR, BLK), lambda b, n: (0, b * nb + prev(n))),
                  pl.BlockSpec((G * VR, NB * BLK), lambda b, n: (0, b * ns + n)),
                  pl.BlockSpec((G, 1, R * BLK), lambda b, n: (0, 0, 0))],
        out_specs=qspec,
        out_shape=jax.ShapeDtypeStruct((G * R * dh, T), BF16),
        compiler_params=_cparams(("arbitrary", "arbitrary")),
        name="swa_sink_attention",
    )(bqt, bk, bk, bvt, bvt, sink_cols)


def _mla_proj_kernel(cq_ref, ckv_ref, kpet_ref, qn_ref, kvn_ref, wqt_ref, wkvt_ref, c_ref, s_ref,
                     qf_ref, kf_ref, vt_ref):
    hq = _rms(cq_ref[...], qn_ref[...]).astype(BF16)
    hkv = _rms(ckv_ref[...], kvn_ref[...]).astype(BF16)
    qt = _dot_nt(wqt_ref[...], hq)
    kvt = _dot_nt(wkvt_ref[...], hkv)
    c = c_ref[...]
    s = s_ref[...]
    kpe = kpet_ref[...]
    scale = (C_NOPE + C_ROPE) ** -0.5 * LOG2E
    half = C_ROPE // 2
    tm = qt.shape[1]
    zpad = jnp.zeros((C_QK_PAD - C_NOPE - C_ROPE, tm), BF16)
    zpad_f32 = jnp.zeros((C_QK_PAD - C_NOPE - C_ROPE, tm), F32)
    ones_row = (lax.broadcasted_iota(jnp.int32, (V_AUG, tm), 0) == 0).astype(BF16)
    for h in range(C_HEADS):
        b = h * C_QK_PAD
        qf_ref[h, 0:C_NOPE, :] = (qt[b:b + C_NOPE] * scale).astype(BF16)
        x1 = qt[b + C_NOPE:b + C_NOPE + half]
        x2 = qt[b + C_NOPE + half:b + C_NOPE + C_ROPE]
        qf_ref[h, C_NOPE:C_NOPE + half, :] = ((x1 * c - x2 * s) * scale).astype(BF16)
        qf_ref[h, C_NOPE + half:C_NOPE + C_ROPE, :] = ((x2 * c + x1 * s) * scale).astype(BF16)
        qf_ref[h, C_NOPE + C_ROPE:C_QK_PAD, :] = zpad
        kft = jnp.concatenate([kvt[b:b + C_NOPE], kpe, zpad_f32], axis=0)
        kf_ref[h] = kft.T.astype(BF16)
        vt_ref[h, 0:C_V, :] = kvt[b + C_NOPE:b + C_NOPE + C_V].astype(BF16)
        vt_ref[h, C_V:C_V + V_AUG, :] = ones_row


def _mla_proj(cq, ckv, kpet, q_norm, w_q_up, kv_norm, w_kv_up, c32t, s32t, tm):
    T = cq.shape[0]
    H = C_HEADS
    wq = w_q_up.reshape(C_Q_RANK, H, C_NOPE + C_ROPE)
    wq = jnp.pad(wq, ((0, 0), (0, 0), (0, C_QK_PAD - C_NOPE - C_ROPE))).reshape(C_Q_RANK, H * C_QK_PAD)
    wqt = wq.T.astype(BF16)
    wkvt = w_kv_up.T.astype(BF16)
    full = lambda a: pl.BlockSpec(a.shape, lambda i: (0,) * a.ndim)
    qn = q_norm.reshape(1, C_Q_RANK)
    kvn = kv_norm.reshape(1, C_KV_RANK)
    return pl.pallas_call(
        _mla_proj_kernel,
        grid=(T // tm,),
        in_specs=[pl.BlockSpec((tm, C_Q_RANK), lambda i: (i, 0)),
                  pl.BlockSpec((tm, C_KV_RANK), lambda i: (i, 0)),
                  pl.BlockSpec((C_ROPE, tm), lambda i: (0, i)),
                  full(qn), full(kvn), full(wqt), full(wkvt),
                  pl.BlockSpec((C_ROPE // 2, tm), lambda i: (0, i)),
                  pl.BlockSpec((C_ROPE // 2, tm), lambda i: (0, i))],
        out_specs=[pl.BlockSpec((H, C_QK_PAD, tm), lambda i: (0, 0, i)),
                   pl.BlockSpec((H, tm, C_QK_PAD), lambda i: (0, i, 0)),
                   pl.BlockSpec((H, C_V + V_AUG, tm), lambda i: (0, 0, i))],
        out_shape=[jax.ShapeDtypeStruct((H, C_QK_PAD, T), BF16),
                   jax.ShapeDtypeStruct((H, T, C_QK_PAD), BF16),
                   jax.ShapeDtypeStruct((H, C_V + V_AUG, T), BF16)],
        compiler_params=_cparams(("arbitrary",)),
        name="mla_projection",
    )(cq, ckv, kpet, qn, kvn, wqt, wkvt, c32t, s32t)


MLA_HEADS_PER_STEP = 2


def _mla_kernel(q_ref, kf_ref, vt_ref, o_ref, sa_ref, sb_ref):
    HB = q_ref.shape[0]
    S = q_ref.shape[2]
    KT, QT = sa_ref.shape[1], sa_ref.shape[2]
    nq = S // QT
    lane = lax.broadcasted_iota(jnp.int32, (1, QT), 1)
    kiota = lax.broadcasted_iota(jnp.int32, (KT, 1), 0)

    def scores(qi, t, dst):
        k0 = pl.multiple_of(t * KT, KT)
        q0 = pl.multiple_of(qi * QT, QT)
        for h in range(HB):
            dst[h] = _dot(kf_ref[h, pl.ds(k0, KT), :], q_ref[h, :, pl.ds(q0, QT)])

    def update(qi, t, src, carry, causal):
        k0 = pl.multiple_of(t * KT, KT)
        out = []
        for h in range(HB):
            m_i, acc = carry[2 * h], carry[2 * h + 1]
            s = src[h]
            if causal:
                s = jnp.where((k0 + kiota) <= qi * QT + lane, s, NEG_INF)
            m_new = jnp.maximum(m_i, jnp.max(s, axis=0, keepdims=True))
            alpha = jnp.exp2(m_i - m_new)
            p = jnp.exp2(s - m_new)
            acc = alpha * acc + _dot(vt_ref[h, :, pl.ds(k0, KT)], p.astype(BF16))
            out.extend((m_new, acc))
        return tuple(out)

    def q_tile(qi, _):
        def step(j, carry):
            scores(qi, 2 * j + 1, sb_ref)
            carry = update(qi, 2 * j, sa_ref, carry, False)
            scores(qi, 2 * j + 2, sa_ref)
            return update(qi, 2 * j + 1, sb_ref, carry, False)

        init = (jnp.full((1, QT), NEG_INF, F32), jnp.zeros((C_V + V_AUG, QT), F32)) * HB
        carry = _tile_loop(qi, step, init)
        scores(qi, 2 * qi + 1, sb_ref)
        carry = update(qi, 2 * qi, sa_ref, carry, True)
        scores(jnp.minimum(qi + 1, nq - 1), 0, sa_ref)
        carry = update(qi, 2 * qi + 1, sb_ref, carry, True)
        q0 = pl.multiple_of(qi * QT, QT)
        for h in range(HB):
            acc = carry[2 * h + 1]
            o_ref[h * C_V:(h + 1) * C_V, pl.ds(q0, QT)] = (acc[0:C_V] * (1.0 / acc[C_V:C_V + 1])).astype(o_ref.dtype)
        return 0

    scores(0, 0, sa_ref)
    lax.fori_loop(0, nq, q_tile, 0)


def _mla(qft, kf, vt, B, S):
    H, HB = C_HEADS, MLA_HEADS_PER_STEP
    T = B * S
    QT = min(KV_TILE, S)
    nq = S // QT
    return pl.pallas_call(
        _mla_kernel,
        grid=(B, H // HB),
        in_specs=[pl.BlockSpec((HB, C_QK_PAD, S), lambda b, h: (h, 0, b)),
                  pl.BlockSpec((HB, S, C_QK_PAD), lambda b, h: (h, b, 0)),
                  pl.BlockSpec((HB, C_V + V_AUG, S), lambda b, h: (h, 0, b))],
        out_specs=pl.BlockSpec((HB * C_V, S), lambda b, h: (h, b)),
        out_shape=jax.ShapeDtypeStruct((H * C_V, T), BF16),
        scratch_shapes=[pltpu.VMEM((HB, QT // 2, QT), F32), pltpu.VMEM((HB, QT // 2, QT), F32)],
        compiler_params=_cparams(("arbitrary", "arbitrary")),
        name="mla_attention",
    )(qft, kf, vt)


def _merge_kernel(x_ref, oa_ref, ob_ref, oc_ref, nrm_ref, wg_ref, wa_ref, wb_ref, wc_ref, wo_ref, o_ref):
    x = x_ref[...]
    h = _rms(x, nrm_ref[...]).astype(BF16)
    g = _sigmoid(_dot(h, wg_ref[...]))
    D = D_MODEL
    m = (g[:, 0:D] * _dot_tn(oa_ref[...], wa_ref[...]) + g[:, D:2 * D] * _dot_tn(ob_ref[...], wb_ref[...])
         + g[:, 2 * D:3 * D] * _dot_tn(oc_ref[...], wc_ref[...]))
    o_ref[...] = x + _dot(m.astype(BF16), wo_ref[...])


def _merge(x, oat, obt, oct, mix_norm, w_gate, w_a, w_b, w_c, w_out, tm):
    T = x.shape[0]
    row = lambda n: pl.BlockSpec((tm, n), lambda i: (i, 0))
    colT = lambda n: pl.BlockSpec((n, tm), lambda i: (0, i))
    full = lambda a: pl.BlockSpec(a.shape, lambda i: (0,) * a.ndim)
    nrm = mix_norm.reshape(1, D_MODEL)
    ws = [w.astype(BF16) for w in (w_gate, w_a, w_b, w_c, w_out)]
    return pl.pallas_call(
        _merge_kernel,
        grid=(T // tm,),
        in_specs=[row(D_MODEL), colT(512), colT(512), colT(512), full(nrm)] + [full(w) for w in ws],
        out_specs=row(D_MODEL),
        out_shape=jax.ShapeDtypeStruct((T, D_MODEL), F32),
        compiler_params=_cparams(("arbitrary",)),
        name="branch_merge",
    )(x, oat, obt, oct, nrm, *ws)


def _ffn_kernel(x_ref, p_ref, fn_ref, wg_ref, wu_ref, wd_ref, pn_ref, wpg_ref, wpe_ref, fin_ref, o_ref,
                *, final):
    x = x_ref[...]
    h = _rms(x, fn_ref[...]).astype(BF16)
    a = _dot(h, wg_ref[...])
    u = _dot(h, wu_ref[...])
    y = (a * _sigmoid(a) * u).astype(BF16)
    x = x + _dot(y, wd_ref[...])
    gate = _sigmoid(_dot(_rms(x, pn_ref[...]).astype(BF16), wpg_ref[...]))
    x = x + gate * _dot(p_ref[...].astype(BF16), wpe_ref[...])
    if final:
        x = _rms(x, fin_ref[...])
    o_ref[...] = x


def _ffn_ple(x, p_i, ffn_norm, w_gate, w_up, w_down, ple_norm, w_ple_gate, w_ple_proj, final_norm, final, tm):
    T = x.shape[0]
    row = lambda n: pl.BlockSpec((tm, n), lambda i: (i, 0))
    full = lambda a: pl.BlockSpec(a.shape, lambda i: (0,) * a.ndim, pipeline_mode=pl.Buffered(1))
    fn = ffn_norm.reshape(1, D_MODEL)
    pn = ple_norm.reshape(1, D_MODEL)
    fin = final_norm.reshape(1, D_MODEL)
    wg, wu, wd, wpg, wpe = [w.astype(BF16) for w in (w_gate, w_up, w_down, w_ple_gate, w_ple_proj)]
    return pl.pallas_call(
        functools.partial(_ffn_kernel, final=final),
        grid=(T // tm,),
        in_specs=[row(D_MODEL), row(PLE_DIM), full(fn), full(wg), full(wu), full(wd), full(pn),
                  full(wpg), full(wpe), full(fin)],
        out_specs=row(D_MODEL),
        out_shape=jax.ShapeDtypeStruct((T, D_MODEL), F32),
        compiler_params=_cparams(("arbitrary",)),
        name="ffn_ple",
    )(x, p_i, fn, wg, wu, wd, pn, wpg, wpe, fin)


def _layer(x, p_i, tables, B, S, final_norm, final, mix_norm, w_in, a_cmp_pos_k, a_cmp_w1_k, a_cmp_w2_k,
           a_cmp_pos_v, a_cmp_w1_v, a_cmp_w2_v, b_sinks, c_q_norm, c_w_q_up, c_kv_norm, c_w_kv_up,
           w_branch_gate, w_branch_a, w_branch_b, w_branch_c, w_out, ffn_norm, w_ffn_gate, w_ffn_up,
           w_ffn_down, ple_norm, w_ple_proj, w_ple_gate):
    T = B * S
    tm = min(512, T)
    (kc, ks_aug, kw, bk, vc, cq, ckv, aqt, bqt, vst, vwt, bvt, gt, kpet) = _inproj(
        x, mix_norm, w_in, tables, tm, B, S)

    k_cmp, v_cmp_t = _compress(kc, vc, a_cmp_pos_k, a_cmp_w1_k, a_cmp_w2_k,
                               a_cmp_pos_v, a_cmp_w1_v, a_cmp_w2_v, B, S)
    oat = _nsa(aqt, k_cmp, v_cmp_t, ks_aug, vst, _window_keys(kw, B), vwt, gt, B, S)
    obt = _swa(bqt, bk, bvt, b_sinks, B, S)

    qft, kf, vt = _mla_proj(cq, ckv, kpet, c_q_norm, c_w_q_up, c_kv_norm, c_w_kv_up,
                            tables[4], tables[5], tm)
    oct = _mla(qft, kf, vt, B, S)

    x = _merge(x, oat, obt, oct, mix_norm, w_branch_gate, w_branch_a, w_branch_b, w_branch_c,
               w_out, min(256, T))
    return _ffn_ple(x, p_i, ffn_norm, w_ffn_gate, w_ffn_up, w_ffn_down, ple_norm, w_ple_gate, w_ple_proj,
                    final_norm, final, min(512, T))


def kernel(x, p, positions, mix_norm, w_in, a_cmp_pos_k, a_cmp_w1_k, a_cmp_w2_k, a_cmp_pos_v, a_cmp_w1_v, a_cmp_w2_v, b_sinks, c_q_norm, c_w_q_up, c_kv_norm, c_w_kv_up, w_branch_gate, w_branch_a, w_branch_b, w_branch_c, w_out, ffn_norm, w_ffn_gate, w_ffn_up, w_ffn_down, ple_norm, w_ple_proj, w_ple_gate, final_norm):
    B, S, D = x.shape
    depth = p.shape[0]
    T = B * S
    tables = _rope_tables(positions, min(512, T))
    xf = x.reshape(T, D)
    per_layer = (mix_norm, w_in, a_cmp_pos_k, a_cmp_w1_k, a_cmp_w2_k, a_cmp_pos_v, a_cmp_w1_v, a_cmp_w2_v,
                 b_sinks, c_q_norm, c_w_q_up, c_kv_norm, c_w_kv_up, w_branch_gate, w_branch_a, w_branch_b,
                 w_branch_c, w_out, ffn_norm, w_ffn_gate, w_ffn_up, w_ffn_down, ple_norm, w_ple_proj,
                 w_ple_gate)
    for i in range(depth):
        xf = _layer(xf, p[i].reshape(T, PLE_DIM), tables, B, S, final_norm, i == depth - 1,
                    *[w[i] for w in per_layer])
    return xf.reshape(B, S, D)
```

```python
import functools
import math

import numpy as np
import jax
import jax.numpy as jnp
from jax import lax
from jax.experimental import pallas as pl
from jax.experimental.pallas import tpu as pltpu

F32 = jnp.float32
BF16 = jnp.bfloat16

D_MODEL = 1024
PLE_DIM = 256
ROPE_THETA = 10000.0
EPS = 1e-6
NEG_INF = -1e30
HEAD_DIM = 64

A_HEADS = 8
A_KV_GROUPS = 2
A_GROUP_HEADS = A_HEADS // A_KV_GROUPS
A_CMP_LEN = 32
A_CMP_STRIDE = 16
A_CMP_HIDDEN = 256
A_SEL_LEN = 64
A_TOPK = 16
A_WINDOW = 512
A_Q_CHUNK = 128
A_FORCE_BONUS = 1e4

B_HEADS = 8
B_KV_HEADS = 2
B_WINDOW = 128
B_BLOCK = 128

C_HEADS = 8
C_Q_RANK = 256
C_KV_RANK = 256
C_NOPE = 64
C_ROPE = 32
C_V = 64
C_QK_PAD = 128

D_FF = int(math.ceil(8 * D_MODEL / 3 / 256)) * 256

LANES = 128
KV_TILE = 512
LOG2E = math.log2(math.e)
V_AUG = 16
K_AUG = 64
BLOCKS_PER_TILE = KV_TILE // A_SEL_LEN
NSA_PREFIX_CLASSES = 4
VMEM_LIMIT = 56 * 1024 * 1024

_ROW_K = 5 * 128
_ROW_COLS = 2 * _ROW_K + 128 + C_Q_RANK + C_KV_RANK
_T_AQ, _T_BQ = 0, 512
_T_VS, _T_VW, _T_BV = 1024, 1152, 1280
_T_G, _T_KPE = 1408, 1440
_T_ROWS = 1472
_GATE_ROWS = 16


def _cparams(sem):
    return pltpu.CompilerParams(dimension_semantics=sem, vmem_limit_bytes=VMEM_LIMIT)


def _rms(x, g):
    return x * lax.rsqrt(jnp.mean(x * x, axis=-1, keepdims=True) + EPS) * g


def _sigmoid(x):
    return 1.0 / (1.0 + jnp.exp(-x))


def _dot(a, b):
    return jnp.dot(a, b, preferred_element_type=F32)


def _dot_nt(a, b):
    return lax.dot_general(a, b, (((1,), (1,)), ((), ())), preferred_element_type=F32)


TILE_UNROLL = 4


def _tile_loop(n, step, carry):
    main = n // TILE_UNROLL

    def trip(j, c):
        for u in range(TILE_UNROLL):
            c = step(j * TILE_UNROLL + u, c)
        return c

    carry = lax.fori_loop(0, main, trip, carry)
    return lax.fori_loop(main * TILE_UNROLL, n, step, carry)


def _dot_tn(a, b):
    return lax.dot_general(a, b, (((0,), (0,)), ((), ())), preferred_element_type=F32)


def _rope_kernel(pc_ref, pr_ref, i128_ref, i64_ref, i32_ref,
                 c128_ref, s128_ref, c64t_ref, s64t_ref, c32t_ref, s32t_ref):
    ang = pc_ref[...].astype(F32) * i128_ref[...]
    c128_ref[...] = jnp.cos(ang)
    s128_ref[...] = jnp.sin(ang)
    pr = pr_ref[...].astype(F32)
    a64 = i64_ref[...] * pr
    c64t_ref[...] = jnp.cos(a64)
    s64t_ref[...] = jnp.sin(a64)
    a32 = i32_ref[...] * pr
    c32t_ref[...] = jnp.cos(a32)
    s32t_ref[...] = jnp.sin(a32)


def _rope_tables(positions, tm):
    T = positions.size
    pos = positions.reshape(T)
    inv64 = jnp.power(F32(ROPE_THETA), -jnp.arange(0, HEAD_DIM, 2, dtype=F32) / HEAD_DIM)
    inv32 = jnp.power(F32(ROPE_THETA), -jnp.arange(0, C_ROPE, 2, dtype=F32) / C_ROPE)
    inv128 = jnp.tile(inv64, 4).reshape(1, 128)
    full = lambda shape: pl.BlockSpec(shape, lambda i: (0,) * len(shape))
    return pl.pallas_call(
        _rope_kernel,
        grid=(T // tm,),
        in_specs=[pl.BlockSpec((tm, 1), lambda i: (i, 0)),
                  pl.BlockSpec((1, tm), lambda i: (0, i)),
                  full((1, 128)), full((32, 1)), full((16, 1))],
        out_specs=[pl.BlockSpec((tm, 128), lambda i: (i, 0)),
                   pl.BlockSpec((tm, 128), lambda i: (i, 0)),
                   pl.BlockSpec((32, tm), lambda i: (0, i)),
                   pl.BlockSpec((32, tm), lambda i: (0, i)),
                   pl.BlockSpec((16, tm), lambda i: (0, i)),
                   pl.BlockSpec((16, tm), lambda i: (0, i))],
        out_shape=[jax.ShapeDtypeStruct((T, 128), F32), jax.ShapeDtypeStruct((T, 128), F32),
                   jax.ShapeDtypeStruct((32, T), F32), jax.ShapeDtypeStruct((32, T), F32),
                   jax.ShapeDtypeStruct((16, T), F32), jax.ShapeDtypeStruct((16, T), F32)],
        compiler_params=_cparams(("arbitrary",)),
        name="rope_tables",
    )(pos.reshape(T, 1), pos.reshape(1, T), inv128, inv64.reshape(32, 1), inv32.reshape(16, 1))


def _store_v_aug(ref, vt, n_heads):
    dv = vt.shape[0] // n_heads
    tm = vt.shape[1]
    ones_row = (lax.broadcasted_iota(jnp.int32, (V_AUG, tm), 0) == 0).astype(ref.dtype)
    for g in range(n_heads):
        r0 = g * (dv + V_AUG)
        ref[r0:r0 + dv, :] = vt[g * dv:(g + 1) * dv].astype(ref.dtype)
        ref[r0 + dv:r0 + dv + V_AUG, :] = ones_row

def _inproj_kernel(x_ref, nrm_ref, wr_ref, wt_ref, c128_ref, s128_ref, c64t_ref, s64t_ref,
                   c32t_ref, s32t_ref,
                   kc_ref, ks_ref, kw_ref, bk_ref, vc_ref, cq_ref, ckv_ref,
                   aqt_ref, bqt_ref, vst_ref, vwt_ref, bvt_ref, gt_ref, kpet_ref, *, n_seq_tiles):
    h = _rms(x_ref[...], nrm_ref[...]).astype(BF16)
    zr = _dot(h, wr_ref[...])
    c = c128_ref[...]
    s = s128_ref[...]
    tm, dh = c.shape[0], HEAD_DIM

    def roped(i):
        return zr[:, 128 * i:128 * (i + 1)] * c + zr[:, _ROW_K + 128 * i:_ROW_K + 128 * (i + 1)] * s

    pos = (pl.program_id(0) % n_seq_tiles) * tm + lax.broadcasted_iota(jnp.int32, (tm, 128), 0)
    lane = lax.broadcasted_iota(jnp.int32, (tm, 128), 1)
    onehot = (lane - dh == (pos // A_SEL_LEN) % BLOCKS_PER_TILE).astype(F32)
    kc = roped(0)
    kw = roped(3)
    bk = roped(4)
    o = 2 * _ROW_K
    vc = zr[:, o:o + 128]
    for g in range(A_KV_GROUPS):
        kc_ref[g] = kc[:, g * dh:(g + 1) * dh]
        vc_ref[g] = vc[:, g * dh:(g + 1) * dh]
        bk_ref[g] = bk[:, g * dh:(g + 1) * dh].astype(BF16)
        ks_ref[g] = (roped(1 + g) + onehot).astype(BF16)
        kw_ref[g] = kw[:, g * dh:(g + 1) * dh].astype(BF16)
    cq_ref[...] = zr[:, o + 128:o + 128 + C_Q_RANK]
    ckv_ref[...] = zr[:, o + 128 + C_Q_RANK:o + 128 + C_Q_RANK + C_KV_RANK]

    zt = _dot_nt(wt_ref[...], h)
    ct = c64t_ref[...]
    st = s64t_ref[...]
    qscale = HEAD_DIM ** -0.5 * LOG2E
    for q_ref, base in ((aqt_ref, _T_AQ), (bqt_ref, _T_BQ)):
        for hh in range(8):
            r0 = base + hh * HEAD_DIM
            x1 = zt[r0:r0 + 32]
            x2 = zt[r0 + 32:r0 + 64]
            q_ref[hh * 64:hh * 64 + 32, :] = ((x1 * ct - x2 * st) * qscale).astype(BF16)
            q_ref[hh * 64 + 32:hh * 64 + 64, :] = ((x2 * ct + x1 * st) * qscale).astype(BF16)
    _store_v_aug(vst_ref, zt[_T_VS:_T_VS + 128], A_KV_GROUPS)
    _store_v_aug(vwt_ref, zt[_T_VW:_T_VW + 128], A_KV_GROUPS)
    _store_v_aug(bvt_ref, zt[_T_BV:_T_BV + 128], B_KV_HEADS)
    gt_ref[...] = _sigmoid(zt[_T_G:_T_G + 2 * _GATE_ROWS])
    c3 = c32t_ref[...]
    s3 = s32t_ref[...]
    x1 = zt[_T_KPE:_T_KPE + 16]
    x2 = zt[_T_KPE + 16:_T_KPE + 32]
    kpet_ref[0:16, :] = x1 * c3 - x2 * s3
    kpet_ref[16:32, :] = x2 * c3 + x1 * s3


def _rot_cols(w, dim):
    k, n = w.shape
    w4 = w.reshape(k, n // dim, 2, dim // 2)
    return jnp.concatenate([-w4[:, :, 1:2], w4[:, :, 0:1]], axis=2).reshape(k, n)


def _inproj(x, mix_norm, w_in, tables, tm, B, S):
    T = x.shape[0]
    G, dh = A_KV_GROUPS, HEAD_DIM
    c128, s128, c64t, s64t, c32t, s32t = tables
    sizes = (512, 128, 128, 128, 128, 128, 128, 24, 512, 128, 128, 256, 256, 32)
    offs = np.concatenate([[0], np.cumsum(sizes)])
    (w_aq, w_kc, w_vc, w_ks, w_vs, w_kw, w_vw, w_g, w_bq, w_bk, w_bv, w_cq, w_ckv, w_kpe) = [
        w_in[:, int(offs[i]):int(offs[i + 1])] for i in range(len(sizes))]

    def spread(w):
        return jnp.pad(w.reshape(D_MODEL, G, dh), ((0, 0), (0, 0), (0, dh))).reshape(D_MODEL, G * 2 * dh)

    kmain = jnp.concatenate([w_kc, spread(w_ks), w_kw, w_bk], axis=1)
    w_row = jnp.concatenate([kmain, _rot_cols(kmain, HEAD_DIM), w_vc, w_cq, w_ckv], axis=1).astype(BF16)
    wg = w_g.reshape(D_MODEL, A_KV_GROUPS, A_GROUP_HEADS * 3)
    wg = jnp.pad(wg, ((0, 0), (0, 0), (0, _GATE_ROWS - A_GROUP_HEADS * 3))).reshape(D_MODEL, 2 * _GATE_ROWS)
    w_t = jnp.concatenate([w_aq, w_bq, w_vs, w_vw, w_bv, wg, w_kpe], axis=1).T.astype(BF16)

    row = lambda n: pl.BlockSpec((tm, n), lambda i: (i, 0))
    colT = lambda n: pl.BlockSpec((n, tm), lambda i: (0, i))
    full = lambda a: pl.BlockSpec(a.shape, lambda i: (0,) * a.ndim)
    nrm = mix_norm.reshape(1, D_MODEL)
    vrows = A_KV_GROUPS * (HEAD_DIM + V_AUG)
    nst = S // tm
    grp = lambda w: pl.BlockSpec((G, tm, w), lambda i: (i // nst, i % nst, 0))
    outs = pl.pallas_call(
        functools.partial(_inproj_kernel, n_seq_tiles=nst),
        grid=(T // tm,),
        in_specs=[row(D_MODEL), full(nrm), full(w_row), full(w_t), row(128), row(128),
                  colT(32), colT(32), colT(16), colT(16)],
        out_specs=[grp(dh), grp(dh + K_AUG), grp(dh), grp(dh), grp(dh), row(C_Q_RANK), row(C_KV_RANK),
                   colT(512), colT(512), colT(vrows), colT(vrows), colT(vrows), colT(2 * _GATE_ROWS), colT(32)],
        out_shape=[jax.ShapeDtypeStruct((B * G, S, dh), F32), jax.ShapeDtypeStruct((B * G, S, dh + K_AUG), BF16),
                   jax.ShapeDtypeStruct((B * G, S, dh), BF16), jax.ShapeDtypeStruct((B * G, S, dh), BF16),
                   jax.ShapeDtypeStruct((B * G, S, dh), F32), jax.ShapeDtypeStruct((T, C_Q_RANK), F32),
                   jax.ShapeDtypeStruct((T, C_KV_RANK), F32),
                   jax.ShapeDtypeStruct((512, T), BF16), jax.ShapeDtypeStruct((512, T), BF16),
                   jax.ShapeDtypeStruct((vrows, T), BF16), jax.ShapeDtypeStruct((vrows, T), BF16),
                   jax.ShapeDtypeStruct((vrows, T), BF16), jax.ShapeDtypeStruct((2 * _GATE_ROWS, T), F32),
                   jax.ShapeDtypeStruct((32, T), F32)],
        compiler_params=_cparams(("arbitrary",)),
        name="in_projection",
    )(x, nrm, w_row, w_t, c128, s128, c64t, s64t, c32t, s32t)
    return outs


def _gelu_tanh(x):
    return 0.5 * x * (1.0 + jnp.tanh(math.sqrt(2.0 / math.pi) * (x + 0.044715 * (x * x * x))))


def _cmp_kernel(seg_ref, pos_ref, w1_ref, w2_ref, w2t_ref, o_ref, ot_ref):
    seg = seg_ref[0, 0]
    n_seg = seg.shape[0]
    half = seg.shape[1]
    a = (seg + pos_ref[0, 0:1, :]).astype(BF16)
    b = (seg + pos_ref[0, 1:2, :]).astype(BF16)
    u = _dot(a, w1_ref[0, 0:half, :])
    v = _dot(b, w1_ref[0, half:2 * half, :])
    pre = u + pltpu.roll(v, n_seg - 1, 0)
    g = _gelu_tanh(pre).astype(BF16)
    o_ref[0, 0] = _dot(g, w2_ref[0]).astype(BF16)
    ot_ref[0, 0] = _dot_nt(w2t_ref[0], g).astype(BF16)


def _compress(kc, vc, pos_k, w1_k, w2_k, pos_v, w1_v, w2_v, B, S):
    G, dh = A_KV_GROUPS, HEAD_DIM
    n_seg = S // A_CMP_STRIDE
    seg_w = A_CMP_STRIDE * dh

    seg = jnp.stack([kc.reshape(B * G, n_seg, seg_w), vc.reshape(B * G, n_seg, seg_w)])
    pos = jnp.stack([pos_k.reshape(2, seg_w), pos_v.reshape(2, seg_w)])
    w1 = jnp.stack([w1_k, w1_v]).astype(BF16)
    w2 = jnp.stack([w2_k, w2_v]).astype(BF16)
    w2t = jnp.stack([w2_k.T, w2_v.T]).astype(BF16)
    per = lambda shape: pl.BlockSpec((1,) + shape, lambda w, i: (w,) + (0,) * len(shape))
    o, ot = pl.pallas_call(
        _cmp_kernel,
        grid=(2, B * G),
        in_specs=[pl.BlockSpec((1, 1, n_seg, seg_w), lambda w, i: (w, i, 0, 0)),
                  per((2, seg_w)), per((2 * seg_w, A_CMP_HIDDEN)), per((A_CMP_HIDDEN, dh)),
                  per((dh, A_CMP_HIDDEN))],
        out_specs=[pl.BlockSpec((1, 1, n_seg, dh), lambda w, i: (w, i, 0, 0)),
                   pl.BlockSpec((1, 1, dh, n_seg), lambda w, i: (w, i, 0, 0))],
        out_shape=[jax.ShapeDtypeStruct((2, B * G, n_seg, dh), BF16),
                   jax.ShapeDtypeStruct((2, B * G, dh, n_seg), BF16)],
        compiler_params=_cparams(("arbitrary", "arbitrary")),
        name="nsa_compress",
    )(seg, pos, w1, w2, w2t)
    return o[0], ot[1]


def _lane_heads(blk, dh, n):
    return jnp.concatenate([blk[r * dh:(r + 1) * dh] for r in range(n)], axis=1)


def _argmax_lowest(val, idx):
    n = val.shape[0]
    v = [val[i:i + 8] for i in range(0, n, 8)]
    j = [idx[i:i + 8] for i in range(0, n, 8)]
    while len(v) > 1:
        nv, nj = [], []
        for a in range(0, len(v) - 1, 2):
            take_b = v[a + 1] > v[a]
            nv.append(jnp.where(take_b, v[a + 1], v[a]))
            nj.append(jnp.where(take_b, j[a + 1], j[a]))
        if len(v) % 2:
            nv.append(v[-1])
            nj.append(j[-1])
        v, j = nv, nj
    vmax = jnp.max(v[0], axis=0, keepdims=True)
    return jnp.min(jnp.where(v[0] == vmax, j[0], float(n)), axis=0, keepdims=True)


def _nsa_select(g, q_ref, kc_ref, vct_ref, smt_ref, selb_ref, qa_ref, tq, tq1, n_seg, n_sel):
    QC, R, dh = A_Q_CHUNK, A_GROUP_HEADS, HEAD_DIM
    NQ = R * QC
    qT = _lane_heads(q_ref[g * R * dh:(g + 1) * R * dh, :], dh, R)
    qa_ref[g] = qT

    s = _dot(kc_ref[g, 0:n_seg, :], qT)
    cend = lax.broadcasted_iota(jnp.int32, (n_seg, 1), 0) * A_CMP_STRIDE + (A_CMP_LEN - 1)
    s = jnp.where(cend <= tq, s, NEG_INF)
    m = jnp.max(s, axis=0, keepdims=True)
    e = jnp.exp2(s - m)
    l = jnp.sum(e, axis=0, keepdims=True)
    inv_l = jnp.where(tq >= A_CMP_LEN - 1, 1.0 / l, 0.0)
    p = e * inv_l
    o_cmp = _dot(vct_ref[g, :, 0:n_seg], p.astype(BF16))

    psum = p[:, 0:QC]
    for r in range(1, R):
        psum = psum + p[:, r * QC:(r + 1) * QC]
    hi = psum.astype(BF16)
    lo = (psum - hi.astype(F32)).astype(BF16)
    smt = smt_ref[0:n_sel, 0:n_seg]
    imp = _dot(smt, hi) + _dot(smt, lo)
    jid =lax.broadcasted_iota(jnp.int32, (n_sel, QC), 0).astype(F32)
    cur = (tq1 // A_SEL_LEN).astype(F32)
    valid = jid <= cur
    forced = (jid == 0.0) | (jid == cur) | (jid == cur - 1.0)

    ineligible, picked = -(2.0 ** 127), -(2.0 ** 126)
    val = jnp.where(valid & jnp.logical_not(forced), imp, ineligible)
    for _ in range(min(A_TOPK, selb_ref.shape[1]) - 3):
        first = _argmax_lowest(val, jid)
        val = jnp.where(jid == first, picked, val)
    selb_ref[g, 0:n_sel, :] = jnp.where((forced | (val == picked)) & valid, 0.0, NEG_INF)
    return o_cmp


def _nsa_kernel(q_ref, kc_ref, vct_ref, smt_ref, ks_ref, vst_ref, kw_ref, vwt_ref, g_ref,
                o_ref, selb_ref, qa_ref, sa_ref, sb_ref):
    QC, R, dh, G = A_Q_CHUNK, A_GROUP_HEADS, HEAD_DIM, A_KV_GROUPS
    NQ = R * QC
    VR = dh + V_AUG
    c = pl.program_id(1)
    t0 = c * QC
    tq = t0 + (lax.broadcasted_iota(jnp.int32, (1, NQ), 1) & (QC - 1))
    tq1 = t0 + lax.broadcasted_iota(jnp.int32, (1, QC), 1)
    gt = g_ref[...]

    def before_key_loop(n_seg, n_sel, window_may_clip):
        o_cmp = [_nsa_select(g, q_ref, kc_ref, vct_ref, smt_ref, selb_ref, qa_ref, tq, tq1, n_seg, n_sel)
                 for g in range(G)]
        WL = A_WINDOW + QC
        if window_may_clip:
            start = pl.multiple_of(jnp.maximum(t0 - A_WINDOW, 0), QC)
            kp = start + lax.broadcasted_iota(jnp.int32, (WL, 1), 0)
            visible = (kp <= tq) & (kp > tq - A_WINDOW)
        else:
            start = pl.multiple_of(t0 - A_WINDOW, QC)
            io = lax.broadcasted_iota(jnp.int32, (QC, QC), 0)
            qo = lax.broadcasted_iota(jnp.int32, (QC, QC), 1)
            lo_bias = jnp.concatenate([jnp.where(io > qo, 0.0, NEG_INF)] * R, axis=1)
            hi_bias = jnp.concatenate([jnp.where(io <= qo, 0.0, NEG_INF)] * R, axis=1)
        o_cw = []
        for g in range(G):
            s_w = _dot(kw_ref[g, pl.ds(start, WL), :], qa_ref[g])
            if window_may_clip:
                s_w = jnp.where(visible, s_w, NEG_INF)
            else:
                s_w = jnp.concatenate([s_w[0:QC] + lo_bias, s_w[QC:WL - QC], s_w[WL - QC:WL] + hi_bias], axis=0)
            m_w = jnp.max(s_w, axis=0, keepdims=True)
            p_w = jnp.exp2(s_w - m_w)
            acc_w = _dot(vwt_ref[g * VR:(g + 1) * VR, pl.ds(start, WL)], p_w.astype(BF16))
            o_win = acc_w[0:dh] * (1.0 / acc_w[dh:dh + 1])
            pieces = []
            for r in range(R):
                sl = slice(r * QC, (r + 1) * QC)
                gr = g * _GATE_ROWS + 3 * r
                pieces.append(gt[gr:gr + 1] * o_cmp[g][:, sl] + gt[gr + 2:gr + 3] * o_win[:, sl])
            o_cw.append(jnp.concatenate(pieces, axis=1))
        return tuple(o_cw)

    n_seg_all, n_sel_all = kc_ref.shape[1], selb_ref.shape[1]
    n_cls = max(1, min(NSA_PREFIX_CLASSES, n_seg_all // LANES))
    chunks_per_cls = (n_seg_all // (A_Q_CHUNK // A_CMP_STRIDE)) // n_cls
    assert n_cls == 1 or chunks_per_cls * QC >= A_WINDOW
    classes = [functools.partial(before_key_loop, (k + 1) * n_seg_all // n_cls, (k + 1) * n_sel_all // n_cls, k == 0)
               for k in range(n_cls)]
    o_cw = lax.switch(c // chunks_per_cls, classes) if n_cls > 1 else classes[0]()

    bpt = BLOCKS_PER_TILE
    KT = sa_ref.shape[1]
    kiota = lax.broadcasted_iota(jnp.int32, (KT, 1), 0)
    pad_rows = jnp.zeros((16 - bpt, NQ), F32)

    zero_rows = jnp.zeros((K_AUG - 16, NQ), BF16)

    def scores(t, dst):
        k0 = pl.multiple_of(t * KT, KT)
        for g in range(G):
            b8 = selb_ref[g, pl.ds(pl.multiple_of((t // 2) * bpt, bpt), bpt), :]
            rows = jnp.concatenate([jnp.concatenate([b8] * R, axis=1), pad_rows], axis=0).astype(BF16)
            rhs = jnp.concatenate([qa_ref[g], rows, zero_rows], axis=0)
            dst[g] = _dot(ks_ref[g, pl.ds(k0, KT), :], rhs)

    def update(t, src, carry, causal):
        k0 = pl.multiple_of(t * KT, KT)
        out = []
        for g in range(G):
            m_i, acc = carry[2 * g], carry[2 * g + 1]
            s_i = src[g]
            if causal:
                cb = jnp.where((k0 + kiota) <= tq1, 0.0, NEG_INF)
                s_i = s_i + jnp.concatenate([cb] * R, axis=1)
            m_new = jnp.maximum(m_i, jnp.max(s_i, axis=0, keepdims=True))
            alpha = jnp.exp2(m_i - m_new)
            p_i = jnp.exp2(s_i - m_new)
            acc = alpha * acc + _dot(vst_ref[g * VR:(g + 1) * VR, pl.ds(k0, KT)], p_i.astype(BF16))
            out.extend((m_new, acc))
        return tuple(out)

    def sel_step(j, carry):
        scores(2 * j + 1, sb_ref)
        carry = update(2 * j, sa_ref, carry, False)
        scores(2 * j + 2, sa_ref)
        return update(2 * j + 1, sb_ref, carry, False)

    n_tiles = (t0 + QC + 2 * KT - 1) // (2 * KT)
    scores(0, sa_ref)
    init = (jnp.full((1, NQ), NEG_INF, F32), jnp.zeros((VR, NQ), F32)) * G
    carry = _tile_loop(n_tiles - 1, sel_step, init)
    last = n_tiles - 1
    scores(2 * last + 1, sb_ref)
    carry = update(2 * last, sa_ref, carry, True)
    carry = update(2 * last + 1, sb_ref, carry, True)

    outs = []
    for g in range(G):
        acc_s = carry[2 * g + 1]
        o_sel = acc_s[0:dh] * (1.0 / acc_s[dh:dh + 1])
        for r in range(R):
            sl = slice(r * QC, (r + 1) * QC)
            gr = g * _GATE_ROWS + 3 * r
            outs.append(o_cw[g][:, sl] + gt[gr + 1:gr + 2] * o_sel[:, sl])
    o_ref[...] = jnp.concatenate(outs, axis=0).astype(o_ref.dtype)


def _sel_map_t(S):
    n_cmp = (S - A_CMP_LEN) // A_CMP_STRIDE + 1
    n_seg = S // A_CMP_STRIDE
    n_sel = S // A_SEL_LEN
    tok = np.arange(n_cmp)[:, None] * A_CMP_STRIDE + np.arange(A_CMP_LEN)[None, :]
    sm = np.zeros((n_seg, n_sel), np.float32)
    np.add.at(sm, (np.repeat(np.arange(n_cmp), A_CMP_LEN), (tok // A_SEL_LEN).reshape(-1)), 1.0 / A_CMP_LEN)
    return jnp.asarray(sm.T, BF16)


def _nsa(aqt, k_cmp, v_cmp_t, ks_aug, vst, kw, vwt, gt, B, S):
    G, R, dh, QC = A_KV_GROUPS, A_GROUP_HEADS, HEAD_DIM, A_Q_CHUNK
    T = B * S
    nq = S // QC
    n_seg = S // A_CMP_STRIDE
    n_sel = S // A_SEL_LEN
    VR = dh + V_AUG
    assert S >= A_WINDOW + QC
    smt = _sel_map_t(S)
    qspec = pl.BlockSpec((G * R * dh, QC), lambda b, c: (0, b * nq + c))
    return pl.pallas_call(
        _nsa_kernel,
        grid=(B, nq),
        in_specs=[qspec,
                  pl.BlockSpec((G, n_seg, dh), lambda b, c: (b, 0, 0)),
                  pl.BlockSpec((G, dh, n_seg), lambda b, c: (b, 0, 0)),
                  pl.BlockSpec((n_sel, n_seg), lambda b, c: (0, 0)),
                  pl.BlockSpec((G, S, dh + K_AUG), lambda b, c: (b, 0, 0)),
                  pl.BlockSpec((G * VR, S), lambda b, c: (0, b)),
                  pl.BlockSpec((G, S, dh), lambda b, c: (b, 0, 0)),
                  pl.BlockSpec((G * VR, S), lambda b, c: (0, b)),
                  pl.BlockSpec((G * _GATE_ROWS, QC), lambda b, c: (0, b * nq + c))],
        out_specs=qspec,
        out_shape=jax.ShapeDtypeStruct((G * R * dh, T), BF16),
        scratch_shapes=[pltpu.VMEM((G, n_sel, QC), F32), pltpu.VMEM((G, dh, R * QC), BF16),
                        pltpu.VMEM((G, KV_TILE // 2, R * QC), F32), pltpu.VMEM((G, KV_TILE // 2, R * QC), F32)],
        compiler_params=_cparams(("arbitrary", "arbitrary")),
        name="nsa_attention",
    )(aqt, k_cmp, v_cmp_t, smt, ks_aug, vst, kw, vwt, gt)


SWA_BLOCKS_PER_STEP = 4


def _swa_kernel(q_ref, kp_ref, kc_ref, vpt_ref, vct_ref, sink_ref, o_ref):
    BLK, G, R, dh, NB = B_BLOCK, B_KV_HEADS, B_HEADS // B_KV_HEADS, HEAD_DIM, SWA_BLOCKS_PER_STEP
    NQ = R * BLK
    VR = dh + V_AUG
    n = pl.program_id(1)
    kj = lax.broadcasted_iota(jnp.int32, (2 * BLK, 1), 0) - BLK
    qi = lax.broadcasted_iota(jnp.int32, (1, NQ), 1) & (BLK - 1)
    band = (kj <= qi) & (kj > qi - B_WINDOW)
    for g in range(G):
        sink = sink_ref[g] * LOG2E
        for j in range(NB):
            cols = slice(j * BLK, (j + 1) * BLK)
            qT = _lane_heads(q_ref[g * R * dh:(g + 1) * R * dh, cols], dh, R)
            if j == 0:
                k = jnp.concatenate([kp_ref[g], kc_ref[g, 0:BLK, :]], axis=0)
                vT = jnp.concatenate([vpt_ref[g * VR:(g + 1) * VR, :], vct_ref[g * VR:(g + 1) * VR, cols]], axis=1)
                ok = band & ((n > 0) | (kj >= 0))
            else:
                k = kc_ref[g, (j - 1) * BLK:(j + 1) * BLK, :]
                vT = vct_ref[g * VR:(g + 1) * VR, (j - 1) * BLK:(j + 1) * BLK]
                ok = band
            s = jnp.where(ok, _dot(k, qT), NEG_INF)
            m = jnp.maximum(jnp.max(s, axis=0, keepdims=True), sink)
            p = jnp.exp2(s - m)
            acc = _dot(vT, p.astype(BF16))
            o = acc[0:dh] * (1.0 / (acc[dh:dh + 1] + jnp.exp2(sink - m)))
            for r in range(R):
                row0 = (g * R + r) * dh
                o_ref[row0:row0 + dh, cols] = o[:, r * BLK:(r + 1) * BLK].astype(o_ref.dtype)


def _swa(bqt, bk, bvt, sinks, B, S):
    G, R, dh, BLK = B_KV_HEADS, B_HEADS // B_KV_HEADS, HEAD_DIM, B_BLOCK
    T = B * S
    nb = S // BLK
    sink_cols = jnp.repeat(sinks.astype(F32).reshape(G, R), BLK, axis=1).reshape(G, 1, R * BLK)
    NB = min(SWA_BLOCKS_PER_STEP, nb)
    ns = nb // NB
    prev = lambda n: jnp.maximum(n * NB - 1, 0)
    VR = dh + V_AUG
    qspec = pl.BlockSpec((G * R * dh, NB * BLK), lambda b, n: (0, b * ns + n))
    return pl.pallas_call(
        _swa_kernel,
        grid=(B, ns),
        in_specs=[qspec,
                  pl.BlockSpec((G, BLK, dh), lambda b, n: (b, prev(n), 0)),
                  pl.BlockSpec((G, NB * BLK, dh), lambda b, n: (b, n, 0)),
                  pl.BlockSpec((G * VR, BLK), lambda b, n: (0, b * nb + prev(n))),
                  pl.BlockSpec((G * VR, NB * BLK), lambda b, n: (0, b * ns + n)),
                  pl.BlockSpec((G, 1, R * BLK), lambda b, n: (0, 0, 0))],
        out_specs=qspec,
        out_shape=jax.ShapeDtypeStruct((G * R * dh, T), BF16),
        compiler_params=_cparams(("arbitrary", "arbitrary")),
        name="swa_sink_attention",
    )(bqt, bk, bk, bvt, bvt, sink_cols)


def _mla_proj_kernel(cq_ref, ckv_ref, kpet_ref, qn_ref, kvn_ref, wqt_ref, wkvt_ref, c_ref, s_ref,
                     qf_ref, kf_ref, vt_ref):
    hq = _rms(cq_ref[...], qn_ref[...]).astype(BF16)
    hkv = _rms(ckv_ref[...], kvn_ref[...]).astype(BF16)
    qt = _dot_nt(wqt_ref[...], hq)
    kvt = _dot_nt(wkvt_ref[...], hkv)
    c = c_ref[...]
    s = s_ref[...]
    kpe = kpet_ref[...]
    scale = (C_NOPE + C_ROPE) ** -0.5 * LOG2E
    half = C_ROPE // 2
    tm = qt.shape[1]
    zpad = jnp.zeros((C_QK_PAD - C_NOPE - C_ROPE, tm), BF16)
    zpad_f32 = jnp.zeros((C_QK_PAD - C_NOPE - C_ROPE, tm), F32)
    ones_row = (lax.broadcasted_iota(jnp.int32, (V_AUG, tm), 0) == 0).astype(BF16)
    for h in range(C_HEADS):
        b = h * C_QK_PAD
        qf_ref[h, 0:C_NOPE, :] = (qt[b:b + C_NOPE] * scale).astype(BF16)
        x1 = qt[b + C_NOPE:b + C_NOPE + half]
        x2 = qt[b + C_NOPE + half:b + C_NOPE + C_ROPE]
        qf_ref[h, C_NOPE:C_NOPE + half, :] = ((x1 * c - x2 * s) * scale).astype(BF16)
        qf_ref[h, C_NOPE + half:C_NOPE + C_ROPE, :] = ((x2 * c + x1 * s) * scale).astype(BF16)
        qf_ref[h, C_NOPE + C_ROPE:C_QK_PAD, :] = zpad
        kft = jnp.concatenate([kvt[b:b + C_NOPE], kpe, zpad_f32], axis=0)
        kf_ref[h] = kft.T.astype(BF16)
        vt_ref[h, 0:C_V, :] = kvt[b + C_NOPE:b + C_NOPE + C_V].astype(BF16)
        vt_ref[h, C_V:C_V + V_AUG, :] = ones_row


def _mla_proj(cq, ckv, kpet, q_norm, w_q_up, kv_norm, w_kv_up, c32t, s32t, tm):
    T = cq.shape[0]
    H = C_HEADS
    wq = w_q_up.reshape(C_Q_RANK, H, C_NOPE + C_ROPE)
    wq = jnp.pad(wq, ((0, 0), (0, 0), (0, C_QK_PAD - C_NOPE - C_ROPE))).reshape(C_Q_RANK, H * C_QK_PAD)
    wqt = wq.T.astype(BF16)
    wkvt = w_kv_up.T.astype(BF16)
    full = lambda a: pl.BlockSpec(a.shape, lambda i: (0,) * a.ndim)
    qn = q_norm.reshape(1, C_Q_RANK)
    kvn = kv_norm.reshape(1, C_KV_RANK)
    return pl.pallas_call(
        _mla_proj_kernel,
        grid=(T // tm,),
        in_specs=[pl.BlockSpec((tm, C_Q_RANK), lambda i: (i, 0)),
                  pl.BlockSpec((tm, C_KV_RANK), lambda i: (i, 0)),
                  pl.BlockSpec((C_ROPE, tm), lambda i: (0, i)),
                  full(qn), full(kvn), full(wqt), full(wkvt),
                  pl.BlockSpec((C_ROPE // 2, tm), lambda i: (0, i)),
                  pl.BlockSpec((C_ROPE // 2, tm), lambda i: (0, i))],
        out_specs=[pl.BlockSpec((H, C_QK_PAD, tm), lambda i: (0, 0, i)),
                   pl.BlockSpec((H, tm, C_QK_PAD), lambda i: (0, i, 0)),
                   pl.BlockSpec((H, C_V + V_AUG, tm), lambda i: (0, 0, i))],
        out_shape=[jax.ShapeDtypeStruct((H, C_QK_PAD, T), BF16),
                   jax.ShapeDtypeStruct((H, T, C_QK_PAD), BF16),
                   jax.ShapeDtypeStruct((H, C_V + V_AUG, T), BF16)],
        compiler_params=_cparams(("arbitrary",)),
        name="mla_projection",
    )(cq, ckv, kpet, qn, kvn, wqt, wkvt, c32t, s32t)


MLA_HEADS_PER_STEP = 2


def _mla_kernel(q_ref, kf_ref, vt_ref, o_ref, sa_ref, sb_ref):
    HB = q_ref.shape[0]
    S = q_ref.shape[2]
    KT, QT = sa_ref.shape[1], sa_ref.shape[2]
    nq = S // QT
    lane = lax.broadcasted_iota(jnp.int32, (1, QT), 1)
    kiota = lax.broadcasted_iota(jnp.int32, (KT, 1), 0)

    def scores(qi, t, dst):
        k0 = pl.multiple_of(t * KT, KT)
        q0 = pl.multiple_of(qi * QT, QT)
        for h in range(HB):
            dst[h] = _dot(kf_ref[h, pl.ds(k0, KT), :], q_ref[h, :, pl.ds(q0, QT)])

    def update(qi, t, src, carry, causal):
        k0 = pl.multiple_of(t * KT, KT)
        out = []
        for h in range(HB):
            m_i, acc = carry[2 * h], carry[2 * h + 1]
            s = src[h]
            if causal:
                s = jnp.where((k0 + kiota) <= qi * QT + lane, s, NEG_INF)
            m_new = jnp.maximum(m_i, jnp.max(s, axis=0, keepdims=True))
            alpha = jnp.exp2(m_i - m_new)
            p = jnp.exp2(s - m_new)
            acc = alpha * acc + _dot(vt_ref[h, :, pl.ds(k0, KT)], p.astype(BF16))
            out.extend((m_new, acc))
        return tuple(out)

    def q_tile(qi, _):
        def step(j, carry):
            scores(qi, 2 * j + 1, sb_ref)
            carry = update(qi, 2 * j, sa_ref, carry, False)
            scores(qi, 2 * j + 2, sa_ref)
            return update(qi, 2 * j + 1, sb_ref, carry, False)

        init = (jnp.full((1, QT), NEG_INF, F32), jnp.zeros((C_V + V_AUG, QT), F32)) * HB
        carry = _tile_loop(qi, step, init)
        scores(qi, 2 * qi + 1, sb_ref)
        carry = update(qi, 2 * qi, sa_ref, carry, True)
        scores(jnp.minimum(qi + 1, nq - 1), 0, sa_ref)
        carry = update(qi, 2 * qi + 1, sb_ref, carry, True)
        q0 = pl.multiple_of(qi * QT, QT)
        for h in range(HB):
            acc = carry[2 * h + 1]
            o_ref[h * C_V:(h + 1) * C_V, pl.ds(q0, QT)] = (acc[0:C_V] * (1.0 / acc[C_V:C_V + 1])).astype(o_ref.dtype)
        return 0

    scores(0, 0, sa_ref)
    lax.fori_loop(0, nq, q_tile, 0)


def _mla(qft, kf, vt, B, S):
    H, HB = C_HEADS, MLA_HEADS_PER_STEP
    T = B * S
    QT = min(KV_TILE, S)
    nq = S // QT
    return pl.pallas_call(
        _mla_kernel,
        grid=(B, H // HB),
        in_specs=[pl.BlockSpec((HB, C_QK_PAD, S), lambda b, h: (h, 0, b)),
                  pl.BlockSpec((HB, S, C_QK_PAD), lambda b, h: (h, b, 0)),
                  pl.BlockSpec((HB, C_V + V_AUG, S), lambda b, h: (h, 0, b))],
        out_specs=pl.BlockSpec((HB * C_V, S), lambda b, h: (h, b)),
        out_shape=jax.ShapeDtypeStruct((H * C_V, T), BF16),
        scratch_shapes=[pltpu.VMEM((HB, QT // 2, QT), F32), pltpu.VMEM((HB, QT // 2, QT), F32)],
        compiler_params=_cparams(("arbitrary", "arbitrary")),
        name="mla_attention",
    )(qft, kf, vt)


def _merge_kernel(x_ref, oa_ref, ob_ref, oc_ref, nrm_ref, wg_ref, wa_ref, wb_ref, wc_ref, wo_ref, o_ref):
    x = x_ref[...]
    h = _rms(x, nrm_ref[...]).astype(BF16)
    g = _sigmoid(_dot(h, wg_ref[...]))
    D = D_MODEL
    m = (g[:, 0:D] * _dot_tn(oa_ref[...], wa_ref[...]) + g[:, D:2 * D] * _dot_tn(ob_ref[...], wb_ref[...])
         + g[:, 2 * D:3 * D] * _dot_tn(oc_ref[...], wc_ref[...]))
    o_ref[...] = x + _dot(m.astype(BF16), wo_ref[...])


def _merge(x, oat, obt, oct, mix_norm, w_gate, w_a, w_b, w_c, w_out, tm):
    T = x.shape[0]
    row = lambda n: pl.BlockSpec((tm, n), lambda i: (i, 0))
    colT = lambda n: pl.BlockSpec((n, tm), lambda i: (0, i))
    full = lambda a: pl.BlockSpec(a.shape, lambda i: (0,) * a.ndim)
    nrm = mix_norm.reshape(1, D_MODEL)
    ws = [w.astype(BF16) for w in (w_gate, w_a, w_b, w_c, w_out)]
    return pl.pallas_call(
        _merge_kernel,
        grid=(T // tm,),
        in_specs=[row(D_MODEL), colT(512), colT(512), colT(512), full(nrm)] + [full(w) for w in ws],
        out_specs=row(D_MODEL),
        out_shape=jax.ShapeDtypeStruct((T, D_MODEL), F32),
        compiler_params=_cparams(("arbitrary",)),
        name="branch_merge",
    )(x, oat, obt, oct, nrm, *ws)


def _ffn_kernel(x_ref, p_ref, fn_ref, wg_ref, wu_ref, wd_ref, pn_ref, wpg_ref, wpe_ref, fin_ref, o_ref,
                *, final):
    x = x_ref[...]
    h = _rms(x, fn_ref[...]).astype(BF16)
    a = _dot(h, wg_ref[...])
    u = _dot(h, wu_ref[...])
    y = (a * _sigmoid(a) * u).astype(BF16)
    x = x + _dot(y, wd_ref[...])
    gate = _sigmoid(_dot(_rms(x, pn_ref[...]).astype(BF16), wpg_ref[...]))
    x = x + gate * _dot(p_ref[...].astype(BF16), wpe_ref[...])
    if final:
        x = _rms(x, fin_ref[...])
    o_ref[...] = x


def _ffn_ple(x, p_i, ffn_norm, w_gate, w_up, w_down, ple_norm, w_ple_gate, w_ple_proj, final_norm, final, tm):
    T = x.shape[0]
    row = lambda n: pl.BlockSpec((tm, n), lambda i: (i, 0))
    full = lambda a: pl.BlockSpec(a.shape, lambda i: (0,) * a.ndim, pipeline_mode=pl.Buffered(1))
    fn = ffn_norm.reshape(1, D_MODEL)
    pn = ple_norm.reshape(1, D_MODEL)
    fin = final_norm.reshape(1, D_MODEL)
    wg, wu, wd, wpg, wpe = [w.astype(BF16) for w in (w_gate, w_up, w_down, w_ple_gate, w_ple_proj)]
    return pl.pallas_call(
        functools.partial(_ffn_kernel, final=final),
        grid=(T // tm,),
        in_specs=[row(D_MODEL), row(PLE_DIM), full(fn), full(wg), full(wu), full(wd), full(pn),
                  full(wpg), full(wpe), full(fin)],
        out_specs=row(D_MODEL),
        out_shape=jax.ShapeDtypeStruct((T, D_MODEL), F32),
        compiler_params=_cparams(("arbitrary",)),
        name="ffn_ple",
    )(x, p_i, fn, wg, wu, wd, pn, wpg, wpe, fin)


def _layer(x, p_i, tables, B, S, final_norm, final, mix_norm, w_in, a_cmp_pos_k, a_cmp_w1_k, a_cmp_w2_k,
           a_cmp_pos_v, a_cmp_w1_v, a_cmp_w2_v, b_sinks, c_q_norm, c_w_q_up, c_kv_norm, c_w_kv_up,
           w_branch_gate, w_branch_a, w_branch_b, w_branch_c, w_out, ffn_norm, w_ffn_gate, w_ffn_up,
           w_ffn_down, ple_norm, w_ple_proj, w_ple_gate):
    T = B * S
    tm = min(512, T)
    (kc, ks_aug, kw, bk, vc, cq, ckv, aqt, bqt, vst, vwt, bvt, gt, kpet) = _inproj(
        x, mix_norm, w_in, tables, tm, B, S)

    k_cmp, v_cmp_t = _compress(kc, vc, a_cmp_pos_k, a_cmp_w1_k, a_cmp_w2_k,
                               a_cmp_pos_v, a_cmp_w1_v, a_cmp_w2_v, B, S)
    oat = _nsa(aqt, k_cmp, v_cmp_t, ks_aug, vst, kw, vwt, gt, B, S)
    obt = _swa(bqt, bk, bvt, b_sinks, B, S)

    qft, kf, vt = _mla_proj(cq, ckv, kpet, c_q_norm, c_w_q_up, c_kv_norm, c_w_kv_up,
                            tables[4], tables[5], tm)
    oct = _mla(qft, kf, vt, B, S)

    x = _merge(x, oat, obt, oct, mix_norm, w_branch_gate, w_branch_a, w_branch_b, w_branch_c,
               w_out, min(256, T))
    return _ffn_ple(x, p_i, ffn_norm, w_ffn_gate, w_ffn_up, w_ffn_down, ple_norm, w_ple_gate, w_ple_proj,
                    final_norm, final, min(512, T))


def kernel(x, p, positions, mix_norm, w_in, a_cmp_pos_k, a_cmp_w1_k, a_cmp_w2_k, a_cmp_pos_v, a_cmp_w1_v, a_cmp_w2_v, b_sinks, c_q_norm, c_w_q_up, c_kv_norm, c_w_kv_up, w_branch_gate, w_branch_a, w_branch_b, w_branch_c, w_out, ffn_norm, w_ffn_gate, w_ffn_up, w_ffn_down, ple_norm, w_ple_proj, w_ple_gate, final_norm):
    B, S, D = x.shape
    depth = p.shape[0]
    T = B * S
    tables = _rope_tables(positions, min(512, T))
    xf = x.reshape(T, D)
    per_layer = (mix_norm, w_in, a_cmp_pos_k, a_cmp_w1_k, a_cmp_w2_k, a_cmp_pos_v, a_cmp_w1_v, a_cmp_w2_v,
                 b_sinks, c_q_norm, c_w_q_up, c_kv_norm, c_w_kv_up, w_branch_gate, w_branch_a, w_branch_b,
                 w_branch_c, w_out, ffn_norm, w_ffn_gate, w_ffn_up, w_ffn_down, ple_norm, w_ple_proj,
                 w_ple_gate)
    for i in range(depth):
        xf = _layer(xf, p[i].reshape(T, PLE_DIM), tables, B, S, final_norm, i == depth - 1,
                    *[w[i] for w in per_layer])
    return xf.reshape(B, S, D)
```

```python
import functools
import math

import numpy as np
import jax
import jax.numpy as jnp
from jax import lax
from jax.experimental import pallas as pl
from jax.experimental.pallas import tpu as pltpu

F32 = jnp.float32
BF16 = jnp.bfloat16

D_MODEL = 1024
PLE_DIM = 256
ROPE_THETA = 10000.0
EPS = 1e-6
NEG_INF = -1e30
HEAD_DIM = 64

A_HEADS = 8
A_KV_GROUPS = 2
A_GROUP_HEADS = A_HEADS // A_KV_GROUPS
A_CMP_LEN = 32
A_CMP_STRIDE = 16
A_CMP_HIDDEN = 256
A_SEL_LEN = 64
A_TOPK = 16
A_WINDOW = 512
A_Q_CHUNK = 128
A_FORCE_BONUS = 1e4

B_HEADS = 8
B_KV_HEADS = 2
B_WINDOW = 128
B_BLOCK = 128

C_HEADS = 8
C_Q_RANK = 256
C_KV_RANK = 256
C_NOPE = 64
C_ROPE = 32
C_V = 64
C_QK_PAD = 128

D_FF = int(math.ceil(8 * D_MODEL / 3 / 256)) * 256

LANES = 128
KV_TILE = 512
LOG2E = math.log2(math.e)
V_AUG = 16
K_AUG = 64
BLOCKS_PER_TILE = KV_TILE // A_SEL_LEN
NSA_PREFIX_CLASSES = 4
VMEM_LIMIT = 56 * 1024 * 1024

_ROW_COLS = 128 + C_Q_RANK + C_KV_RANK
_T_AQ, _T_BQ = 0, 512
_T_VS, _T_VW, _T_BV = 1024, 1152, 1280
_T_G, _T_KPE = 1408, 1440
_T_KC, _T_KS, _T_KW, _T_BK = 1472, 1600, 1728, 1856
_T_ROWS = 1984
_GATE_ROWS = 16


def _cparams(sem):
    return pltpu.CompilerParams(dimension_semantics=sem, vmem_limit_bytes=VMEM_LIMIT)


def _rms(x, g):
    return x * lax.rsqrt(jnp.mean(x * x, axis=-1, keepdims=True) + EPS) * g


def _sigmoid(x):
    return 1.0 / (1.0 + jnp.exp(-x))


def _dot(a, b):
    return jnp.dot(a, b, preferred_element_type=F32)


def _dot_nt(a, b):
    return lax.dot_general(a, b, (((1,), (1,)), ((), ())), preferred_element_type=F32)


TILE_UNROLL = 4


def _tile_loop(n, step, carry):
    main = n // TILE_UNROLL

    def trip(j, c):
        for u in range(TILE_UNROLL):
            c = step(j * TILE_UNROLL + u, c)
        return c

    carry = lax.fori_loop(0, main, trip, carry)
    return lax.fori_loop(main * TILE_UNROLL, n, step, carry)


def _dot_tn(a, b):
    return lax.dot_general(a, b, (((0,), (0,)), ((), ())), preferred_element_type=F32)


def _rope_kernel(pr_ref, i64_ref, i32_ref, c64t_ref, s64t_ref, c32t_ref, s32t_ref):
    pr = pr_ref[...].astype(F32)
    a64 = i64_ref[...] * pr
    c64t_ref[...] = jnp.cos(a64)
    s64t_ref[...] = jnp.sin(a64)
    a32 = i32_ref[...] * pr
    c32t_ref[...] = jnp.cos(a32)
    s32t_ref[...] = jnp.sin(a32)


def _rope_tables(positions, tm):
    T = positions.size
    pos = positions.reshape(T)
    inv64 = jnp.power(F32(ROPE_THETA), -jnp.arange(0, HEAD_DIM, 2, dtype=F32) / HEAD_DIM)
    inv32 = jnp.power(F32(ROPE_THETA), -jnp.arange(0, C_ROPE, 2, dtype=F32) / C_ROPE)
    full = lambda shape: pl.BlockSpec(shape, lambda i: (0,) * len(shape))
    return pl.pallas_call(
        _rope_kernel,
        grid=(T // tm,),
        in_specs=[pl.BlockSpec((1, tm), lambda i: (0, i)), full((32, 1)), full((16, 1))],
        out_specs=[pl.BlockSpec((32, tm), lambda i: (0, i)),
                   pl.BlockSpec((32, tm), lambda i: (0, i)),
                   pl.BlockSpec((16, tm), lambda i: (0, i)),
                   pl.BlockSpec((16, tm), lambda i: (0, i))],
        out_shape=[jax.ShapeDtypeStruct((32, T), F32), jax.ShapeDtypeStruct((32, T), F32),
                   jax.ShapeDtypeStruct((16, T), F32), jax.ShapeDtypeStruct((16, T), F32)],
        compiler_params=_cparams(("arbitrary",)),
        name="rope_tables",
    )(pos.reshape(1, T), inv64.reshape(32, 1), inv32.reshape(16, 1))


def _store_v_aug(ref, vt, n_heads):
    dv = vt.shape[0] // n_heads
    tm = vt.shape[1]
    ones_row = (lax.broadcasted_iota(jnp.int32, (V_AUG, tm), 0) == 0).astype(ref.dtype)
    for g in range(n_heads):
        r0 = g * (dv + V_AUG)
        ref[r0:r0 + dv, :] = vt[g * dv:(g + 1) * dv].astype(ref.dtype)
        ref[r0 + dv:r0 + dv + V_AUG, :] = ones_row

def _inproj_kernel(x_ref, nrm_ref, wr_ref, wt_ref, c64t_ref, s64t_ref, c32t_ref, s32t_ref,
                   kc_ref, ks_ref, kw_ref, bk_ref, vc_ref, cq_ref, ckv_ref,
                   aqt_ref, bqt_ref, vst_ref, vwt_ref, bvt_ref, gt_ref, kpet_ref, *, n_seq_tiles):
    h = _rms(x_ref[...], nrm_ref[...]).astype(BF16)
    zr = _dot(h, wr_ref[...])
    tm, dh = zr.shape[0], HEAD_DIM
    for g in range(A_KV_GROUPS):
        vc_ref[g] = zr[:, g * dh:(g + 1) * dh]
    cq_ref[...] = zr[:, 128:128 + C_Q_RANK]
    ckv_ref[...] = zr[:, 128 + C_Q_RANK:128 + C_Q_RANK + C_KV_RANK]

    zt = _dot_nt(wt_ref[...], h)
    ct = c64t_ref[...]
    st = s64t_ref[...]

    def roped_t(r0):
        x1 = zt[r0:r0 + 32]
        x2 = zt[r0 + 32:r0 + 64]
        return jnp.concatenate([x1 * ct - x2 * st, x2 * ct + x1 * st], axis=0)

    kc = jnp.concatenate([roped_t(_T_KC), roped_t(_T_KC + dh)], axis=0).T
    kw = jnp.concatenate([roped_t(_T_KW), roped_t(_T_KW + dh)], axis=0).T
    bk = jnp.concatenate([roped_t(_T_BK), roped_t(_T_BK + dh)], axis=0).T
    pos = (pl.program_id(0) % n_seq_tiles) * tm + lax.broadcasted_iota(jnp.int32, (tm, 128), 0)
    lane = lax.broadcasted_iota(jnp.int32, (tm, 128), 1)
    onehot = (lane - dh == (pos // A_SEL_LEN) % BLOCKS_PER_TILE).astype(F32)
    zero_rows = jnp.zeros((K_AUG, tm), F32)
    for g in range(A_KV_GROUPS):
        kc_ref[g] = kc[:, g * dh:(g + 1) * dh]
        kw_ref[g] = kw[:, g * dh:(g + 1) * dh].astype(BF16)
        bk_ref[g] = bk[:, g * dh:(g + 1) * dh].astype(BF16)
        ks = jnp.concatenate([roped_t(_T_KS + g * dh), zero_rows], axis=0).T
        ks_ref[g] = (ks + onehot).astype(BF16)

    qscale = HEAD_DIM ** -0.5 * LOG2E
    for q_ref, base in ((aqt_ref, _T_AQ), (bqt_ref, _T_BQ)):
        for hh in range(8):
            r0 = base + hh * HEAD_DIM
            x1 = zt[r0:r0 + 32]
            x2 = zt[r0 + 32:r0 + 64]
            q_ref[hh * 64:hh * 64 + 32, :] = ((x1 * ct - x2 * st) * qscale).astype(BF16)
            q_ref[hh * 64 + 32:hh * 64 + 64, :] = ((x2 * ct + x1 * st) * qscale).astype(BF16)
    _store_v_aug(vst_ref, zt[_T_VS:_T_VS + 128], A_KV_GROUPS)
    _store_v_aug(vwt_ref, zt[_T_VW:_T_VW + 128], A_KV_GROUPS)
    _store_v_aug(bvt_ref, zt[_T_BV:_T_BV + 128], B_KV_HEADS)
    gt_ref[...] = _sigmoid(zt[_T_G:_T_G + 2 * _GATE_ROWS])
    c3 = c32t_ref[...]
    s3 = s32t_ref[...]
    x1 = zt[_T_KPE:_T_KPE + 16]
    x2 = zt[_T_KPE + 16:_T_KPE + 32]
    kpet_ref[0:16, :] = x1 * c3 - x2 * s3
    kpet_ref[16:32, :] = x2 * c3 + x1 * s3


def _inproj(x, mix_norm, w_in, tables, tm, B, S):
    T = x.shape[0]
    G, dh = A_KV_GROUPS, HEAD_DIM
    c64t, s64t, c32t, s32t = tables
    sizes = (512, 128, 128, 128, 128, 128, 128, 24, 512, 128, 128, 256, 256, 32)
    offs = np.concatenate([[0], np.cumsum(sizes)])
    (w_aq, w_kc, w_vc, w_ks, w_vs, w_kw, w_vw, w_g, w_bq, w_bk, w_bv, w_cq, w_ckv, w_kpe) = [
        w_in[:, int(offs[i]):int(offs[i + 1])] for i in range(len(sizes))]
    w_row = jnp.concatenate([w_vc, w_cq, w_ckv], axis=1).astype(BF16)
    wg = w_g.reshape(D_MODEL, A_KV_GROUPS, A_GROUP_HEADS * 3)
    wg = jnp.pad(wg, ((0, 0), (0, 0), (0, _GATE_ROWS - A_GROUP_HEADS * 3))).reshape(D_MODEL, 2 * _GATE_ROWS)
    w_t = jnp.concatenate([w_aq, w_bq, w_vs, w_vw, w_bv, wg, w_kpe, w_kc, w_ks, w_kw, w_bk],
                          axis=1).T.astype(BF16)

    row = lambda n: pl.BlockSpec((tm, n), lambda i: (i, 0))
    colT = lambda n: pl.BlockSpec((n, tm), lambda i: (0, i))
    full = lambda a: pl.BlockSpec(a.shape, lambda i: (0,) * a.ndim)
    nrm = mix_norm.reshape(1, D_MODEL)
    vrows = A_KV_GROUPS * (HEAD_DIM + V_AUG)
    nst = S // tm
    grp = lambda w: pl.BlockSpec((G, tm, w), lambda i: (i // nst, i % nst, 0))
    outs = pl.pallas_call(
        functools.partial(_inproj_kernel, n_seq_tiles=nst),
        grid=(T // tm,),
        in_specs=[row(D_MODEL), full(nrm), full(w_row), full(w_t), colT(32), colT(32), colT(16), colT(16)],
        out_specs=[grp(dh), grp(dh + K_AUG), grp(dh), grp(dh), grp(dh), row(C_Q_RANK), row(C_KV_RANK),
                   colT(512), colT(512), colT(vrows), colT(vrows), colT(vrows), colT(2 * _GATE_ROWS), colT(32)],
        out_shape=[jax.ShapeDtypeStruct((B * G, S, dh), F32), jax.ShapeDtypeStruct((B * G, S, dh + K_AUG), BF16),
                   jax.ShapeDtypeStruct((B * G, S, dh), BF16), jax.ShapeDtypeStruct((B * G, S, dh), BF16),
                   jax.ShapeDtypeStruct((B * G, S, dh), F32), jax.ShapeDtypeStruct((T, C_Q_RANK), F32),
                   jax.ShapeDtypeStruct((T, C_KV_RANK), F32),
                   jax.ShapeDtypeStruct((512, T), BF16), jax.ShapeDtypeStruct((512, T), BF16),
                   jax.ShapeDtypeStruct((vrows, T), BF16), jax.ShapeDtypeStruct((vrows, T), BF16),
                   jax.ShapeDtypeStruct((vrows, T), BF16), jax.ShapeDtypeStruct((2 * _GATE_ROWS, T), F32),
                   jax.ShapeDtypeStruct((32, T), F32)],
        compiler_params=_cparams(("arbitrary",)),
        name="in_projection",
    )(x, nrm, w_row, w_t, c64t, s64t, c32t, s32t)
    return outs


def _gelu_tanh(x):
    return 0.5 * x * (1.0 + jnp.tanh(math.sqrt(2.0 / math.pi) * (x + 0.044715 * (x * x * x))))


def _cmp_kernel(seg_ref, pos_ref, w1_ref, w2_ref, w2t_ref, o_ref, ot_ref):
    seg = seg_ref[0, 0]
    n_seg = seg.shape[0]
    half = seg.shape[1]
    a = (seg + pos_ref[0, 0:1, :]).astype(BF16)
    b = (seg + pos_ref[0, 1:2, :]).astype(BF16)
    u = _dot(a, w1_ref[0, 0:half, :])
    v = _dot(b, w1_ref[0, half:2 * half, :])
    pre = u + pltpu.roll(v, n_seg - 1, 0)
    g = _gelu_tanh(pre).astype(BF16)
    o_ref[0, 0] = _dot(g, w2_ref[0]).astype(BF16)
    ot_ref[0, 0] = _dot_nt(w2t_ref[0], g).astype(BF16)


def _compress(kc, vc, pos_k, w1_k, w2_k, pos_v, w1_v, w2_v, B, S):
    G, dh = A_KV_GROUPS, HEAD_DIM
    n_seg = S // A_CMP_STRIDE
    seg_w = A_CMP_STRIDE * dh

    seg = jnp.stack([kc.reshape(B * G, n_seg, seg_w), vc.reshape(B * G, n_seg, seg_w)])
    pos = jnp.stack([pos_k.reshape(2, seg_w), pos_v.reshape(2, seg_w)])
    w1 = jnp.stack([w1_k, w1_v]).astype(BF16)
    w2 = jnp.stack([w2_k, w2_v]).astype(BF16)
    w2t = jnp.stack([w2_k.T, w2_v.T]).astype(BF16)
    per = lambda shape: pl.BlockSpec((1,) + shape, lambda w, i: (w,) + (0,) * len(shape))
    o, ot = pl.pallas_call(
        _cmp_kernel,
        grid=(2, B * G),
        in_specs=[pl.BlockSpec((1, 1, n_seg, seg_w), lambda w, i: (w, i, 0, 0)),
                  per((2, seg_w)), per((2 * seg_w, A_CMP_HIDDEN)), per((A_CMP_HIDDEN, dh)),
                  per((dh, A_CMP_HIDDEN))],
        out_specs=[pl.BlockSpec((1, 1, n_seg, dh), lambda w, i: (w, i, 0, 0)),
                   pl.BlockSpec((1, 1, dh, n_seg), lambda w, i: (w, i, 0, 0))],
        out_shape=[jax.ShapeDtypeStruct((2, B * G, n_seg, dh), BF16),
                   jax.ShapeDtypeStruct((2, B * G, dh, n_seg), BF16)],
        compiler_params=_cparams(("arbitrary", "arbitrary")),
        name="nsa_compress",
    )(seg, pos, w1, w2, w2t)
    return o[0], ot[1]


def _lane_heads(blk, dh, n):
    return jnp.concatenate([blk[r * dh:(r + 1) * dh] for r in range(n)], axis=1)


def _argmax_lowest(val, idx):
    n = val.shape[0]
    v = [val[i:i + 8] for i in range(0, n, 8)]
    j = [idx[i:i + 8] for i in range(0, n, 8)]
    while len(v) > 1:
        nv, nj = [], []
        for a in range(0, len(v) - 1, 2):
            take_b = v[a + 1] > v[a]
            nv.append(jnp.where(take_b, v[a + 1], v[a]))
            nj.append(jnp.where(take_b, j[a + 1], j[a]))
        if len(v) % 2:
            nv.append(v[-1])
            nj.append(j[-1])
        v, j = nv, nj
    vmax = jnp.max(v[0], axis=0, keepdims=True)
    return jnp.min(jnp.where(v[0] == vmax, j[0], float(n)), axis=0, keepdims=True)


def _nsa_select(g, q_ref, kc_ref, vct_ref, smt_ref, selb_ref, qa_ref, tq, tq1, n_seg, n_sel):
    QC, R, dh = A_Q_CHUNK, A_GROUP_HEADS, HEAD_DIM
    NQ = R * QC
    qT = _lane_heads(q_ref[g * R * dh:(g + 1) * R * dh, :], dh, R)
    qa_ref[g] = qT

    s = _dot(kc_ref[g, 0:n_seg, :], qT)
    cend = lax.broadcasted_iota(jnp.int32, (n_seg, 1), 0) * A_CMP_STRIDE + (A_CMP_LEN - 1)
    s = jnp.where(cend <= tq, s, NEG_INF)
    m = jnp.max(s, axis=0, keepdims=True)
    e = jnp.exp2(s - m)
    l = jnp.sum(e, axis=0, keepdims=True)
    inv_l = jnp.where(tq >= A_CMP_LEN - 1, 1.0 / l, 0.0)
    p = e * inv_l
    o_cmp = _dot(vct_ref[g, :, 0:n_seg], p.astype(BF16))

    psum = p[:, 0:QC]
    for r in range(1, R):
        psum = psum + p[:, r * QC:(r + 1) * QC]
    hi = psum.astype(BF16)
    lo = (psum - hi.astype(F32)).astype(BF16)
    smt = smt_ref[0:n_sel, 0:n_seg]
    imp = _dot(smt, hi) + _dot(smt, lo)
    jid = lax.broadcasted_iota(jnp.int32, (n_sel, QC), 0).astype(F32)
    cur = (tq1 // A_SEL_LEN).astype(F32)
    valid = jid <= cur
    forced = (jid == 0.0) | (jid == cur) | (jid == cur - 1.0)

    ineligible, picked = -(2.0 ** 127), -(2.0 ** 126)
    val = jnp.where(valid & jnp.logical_not(forced), imp, ineligible)
    for _ in range(min(A_TOPK, selb_ref.shape[1]) - 3):
        first = _argmax_lowest(val, jid)
        val = jnp.where(jid == first, picked, val)
    selb_ref[g, 0:n_sel, :] = jnp.where((forced | (val == picked)) & valid, 0.0, NEG_INF)
    return o_cmp


def _nsa_kernel(q_ref, kc_ref, vct_ref, smt_ref, ks_ref, vst_ref, kw_ref, vwt_ref, g_ref,
                o_ref, selb_ref, qa_ref, sa_ref, sb_ref):
    QC, R, dh, G = A_Q_CHUNK, A_GROUP_HEADS, HEAD_DIM, A_KV_GROUPS
    NQ = R * QC
    VR = dh + V_AUG
    c = pl.program_id(1)
    t0 = c * QC
    tq = t0 + (lax.broadcasted_iota(jnp.int32, (1, NQ), 1) & (QC - 1))
    tq1 = t0 + lax.broadcasted_iota(jnp.int32, (1, QC), 1)
    gt = g_ref[...]

    def before_key_loop(n_seg, n_sel, window_may_clip):
        o_cmp = [_nsa_select(g, q_ref, kc_ref, vct_ref, smt_ref, selb_ref, qa_ref, tq, tq1, n_seg, n_sel)
                 for g in range(G)]
        WL = A_WINDOW + QC
        if window_may_clip:
            start = pl.multiple_of(jnp.maximum(t0 - A_WINDOW, 0), QC)
            kp = start + lax.broadcasted_iota(jnp.int32, (WL, 1), 0)
            visible = (kp <= tq) & (kp > tq - A_WINDOW)
        else:
            start = pl.multiple_of(t0 - A_WINDOW, QC)
            io = lax.broadcasted_iota(jnp.int32, (QC, QC), 0)
            qo = lax.broadcasted_iota(jnp.int32, (QC, QC), 1)
            lo_bias = jnp.concatenate([jnp.where(io > qo, 0.0, NEG_INF)] * R, axis=1)
            hi_bias = jnp.concatenate([jnp.where(io <= qo, 0.0, NEG_INF)] * R, axis=1)
        o_cw = []
        for g in range(G):
            s_w = _dot(kw_ref[g, pl.ds(start, WL), :], qa_ref[g])
            if window_may_clip:
                s_w = jnp.where(visible, s_w, NEG_INF)
            else:
                s_w = jnp.concatenate([s_w[0:QC] + lo_bias, s_w[QC:WL - QC], s_w[WL - QC:WL] + hi_bias], axis=0)
            m_w = jnp.max(s_w, axis=0, keepdims=True)
            p_w = jnp.exp2(s_w - m_w)
            acc_w = _dot(vwt_ref[g * VR:(g + 1) * VR, pl.ds(start, WL)], p_w.astype(BF16))
            o_win = acc_w[0:dh] * (1.0 / acc_w[dh:dh + 1])
            pieces = []
            for r in range(R):
                sl = slice(r * QC, (r + 1) * QC)
                gr = g * _GATE_ROWS + 3 * r
                pieces.append(gt[gr:gr + 1] * o_cmp[g][:, sl] + gt[gr + 2:gr + 3] * o_win[:, sl])
            o_cw.append(jnp.concatenate(pieces, axis=1))
        return tuple(o_cw)

    n_seg_all, n_sel_all = kc_ref.shape[1], selb_ref.shape[1]
    n_cls = max(1, min(NSA_PREFIX_CLASSES, n_seg_all // LANES))
    chunks_per_cls = (n_seg_all // (A_Q_CHUNK // A_CMP_STRIDE)) // n_cls
    assert n_cls == 1 or chunks_per_cls * QC >= A_WINDOW
    classes = [functools.partial(before_key_loop, (k + 1) * n_seg_all // n_cls, (k + 1) * n_sel_all // n_cls, k == 0)
               for k in range(n_cls)]
    o_cw = lax.switch(c // chunks_per_cls, classes) if n_cls > 1 else classes[0]()

    bpt = BLOCKS_PER_TILE
    KT = sa_ref.shape[1]
    kiota = lax.broadcasted_iota(jnp.int32, (KT, 1), 0)
    pad_rows = jnp.zeros((16 - bpt, NQ), F32)

    zero_rows = jnp.zeros((K_AUG - 16, NQ), BF16)

    def scores(t, dst):
        k0 = pl.multiple_of(t * KT, KT)
        for g in range(G):
            b8 = selb_ref[g, pl.ds(pl.multiple_of((t // 2) * bpt, bpt), bpt), :]
            rows = jnp.concatenate([jnp.concatenate([b8] * R, axis=1), pad_rows], axis=0).astype(BF16)
            rhs = jnp.concatenate([qa_ref[g], rows, zero_rows], axis=0)
            dst[g] = _dot(ks_ref[g, pl.ds(k0, KT), :], rhs)

    def update(t, src, carry, causal):
        k0 = pl.multiple_of(t * KT, KT)
        out = []
        for g in range(G):
            m_i, acc = carry[2 * g], carry[2 * g + 1]
            s_i = src[g]
            if causal:
                cb = jnp.where((k0 + kiota) <= tq1, 0.0, NEG_INF)
                s_i = s_i + jnp.concatenate([cb] * R, axis=1)
            m_new = jnp.maximum(m_i, jnp.max(s_i, axis=0, keepdims=True))
            alpha = jnp.exp2(m_i - m_new)
            p_i = jnp.exp2(s_i - m_new)
            acc = alpha * acc + _dot(vst_ref[g * VR:(g + 1) * VR, pl.ds(k0, KT)], p_i.astype(BF16))
            out.extend((m_new, acc))
        return tuple(out)

    def sel_step(j, carry):
        scores(2 * j + 1, sb_ref)
        carry = update(2 * j, sa_ref, carry, False)
        scores(2 * j + 2, sa_ref)
        return update(2 * j + 1, sb_ref, carry, False)

    n_tiles = (t0 + QC + 2 * KT - 1) // (2 * KT)
    scores(0, sa_ref)
    init = (jnp.full((1, NQ), NEG_INF, F32), jnp.zeros((VR, NQ), F32)) * G
    carry = _tile_loop(n_tiles - 1, sel_step, init)
    last = n_tiles - 1
    scores(2 * last + 1, sb_ref)
    carry = update(2 * last, sa_ref, carry, True)
    carry = update(2 * last + 1, sb_ref, carry, True)

    outs = []
    for g in range(G):
        acc_s = carry[2 * g + 1]
        o_sel = acc_s[0:dh] * (1.0 / acc_s[dh:dh + 1])
        for r in range(R):
            sl = slice(r * QC, (r + 1) * QC)
            gr = g * _GATE_ROWS + 3 * r
            outs.append(o_cw[g][:, sl] + gt[gr + 1:gr + 2] * o_sel[:, sl])
    o_ref[...] = jnp.concatenate(outs, axis=0).astype(o_ref.dtype)


def _sel_map_t(S):
    n_cmp = (S - A_CMP_LEN) // A_CMP_STRIDE + 1
    n_seg = S // A_CMP_STRIDE
    n_sel = S // A_SEL_LEN
    tok = np.arange(n_cmp)[:, None] * A_CMP_STRIDE + np.arange(A_CMP_LEN)[None, :]
    sm = np.zeros((n_seg, n_sel), np.float32)
    np.add.at(sm, (np.repeat(np.arange(n_cmp), A_CMP_LEN), (tok // A_SEL_LEN).reshape(-1)), 1.0 / A_CMP_LEN)
    return jnp.asarray(sm.T, BF16)


def _nsa(aqt, k_cmp, v_cmp_t, ks_aug, vst, kw, vwt, gt, B, S):
    G, R, dh, QC = A_KV_GROUPS, A_GROUP_HEADS, HEAD_DIM, A_Q_CHUNK
    T = B * S
    nq = S // QC
    n_seg = S // A_CMP_STRIDE
    n_sel = S // A_SEL_LEN
    VR = dh + V_AUG
    assert S >= A_WINDOW + QC
    smt = _sel_map_t(S)
    qspec = pl.BlockSpec((G * R * dh, QC), lambda b, c: (0, b * nq + c))
    return pl.pallas_call(
        _nsa_kernel,
        grid=(B, nq),
        in_specs=[qspec,
                  pl.BlockSpec((G, n_seg, dh), lambda b, c: (b, 0, 0)),
                  pl.BlockSpec((G, dh, n_seg), lambda b, c: (b, 0, 0)),
                  pl.BlockSpec((n_sel, n_seg), lambda b, c: (0, 0)),
                  pl.BlockSpec((G, S, dh + K_AUG), lambda b, c: (b, 0, 0)),
                  pl.BlockSpec((G * VR, S), lambda b, c: (0, b)),
                  pl.BlockSpec((G, S, dh), lambda b, c: (b, 0, 0)),
                  pl.BlockSpec((G * VR, S), lambda b, c: (0, b)),
                  pl.BlockSpec((G * _GATE_ROWS, QC), lambda b, c: (0, b * nq + c))],
        out_specs=qspec,
        out_shape=jax.ShapeDtypeStruct((G * R * dh, T), BF16),
        scratch_shapes=[pltpu.VMEM((G, n_sel, QC), F32), pltpu.VMEM((G, dh, R * QC), BF16),
                        pltpu.VMEM((G, KV_TILE // 2, R * QC), F32), pltpu.VMEM((G, KV_TILE // 2, R * QC), F32)],
        compiler_params=_cparams(("arbitrary", "arbitrary")),
        name="nsa_attention",
    )(aqt, k_cmp, v_cmp_t, smt, ks_aug, vst, kw, vwt, gt)


SWA_BLOCKS_PER_STEP = 4


def _swa_kernel(q_ref, kp_ref, kc_ref, vpt_ref, vct_ref, sink_ref, o_ref):
    BLK, G, R, dh, NB = B_BLOCK, B_KV_HEADS, B_HEADS // B_KV_HEADS, HEAD_DIM, SWA_BLOCKS_PER_STEP
    NQ = R * BLK
    VR = dh + V_AUG
    n = pl.program_id(1)
    kj = lax.broadcasted_iota(jnp.int32, (2 * BLK, 1), 0) - BLK
    qi = lax.broadcasted_iota(jnp.int32, (1, NQ), 1) & (BLK - 1)
    band = (kj <= qi) & (kj > qi - B_WINDOW)
    for g in range(G):
        sink = sink_ref[g] * LOG2E
        for j in range(NB):
            cols = slice(j * BLK, (j + 1) * BLK)
            qT = _lane_heads(q_ref[g * R * dh:(g + 1) * R * dh, cols], dh, R)
            if j == 0:
                k = jnp.concatenate([kp_ref[g], kc_ref[g, 0:BLK, :]], axis=0)
                vT = jnp.concatenate([vpt_ref[g * VR:(g + 1) * VR, :], vct_ref[g * VR:(g + 1) * VR, cols]], axis=1)
                ok = band & ((n > 0) | (kj >= 0))
            else:
                k = kc_ref[g, (j - 1) * BLK:(j + 1) * BLK, :]
                vT = vct_ref[g * VR:(g + 1) * VR, (j - 1) * BLK:(j + 1) * BLK]
                ok = band
            s = jnp.where(ok, _dot(k, qT), NEG_INF)
            m = jnp.maximum(jnp.max(s, axis=0, keepdims=True), sink)
            p = jnp.exp2(s - m)
            acc = _dot(vT, p.astype(BF16))
            o = acc[0:dh] * (1.0 / (acc[dh:dh + 1] + jnp.exp2(sink - m)))
            for r in range(R):
                row0 = (g * R + r) * dh
                o_ref[row0:row0 + dh, cols] = o[:, r * BLK:(r + 1) * BLK].astype(o_ref.dtype)


def _swa(bqt, bk, bvt, sinks, B, S):
    G, R, dh, BLK = B_KV_HEADS, B_HEADS // B_KV_HEADS, HEAD_DIM, B_BLOCK
    T = B * S
    nb = S // BLK
    sink_cols = jnp.repeat(sinks.astype(F32).reshape(G, R), BLK, axis=1).reshape(G, 1, R * BLK)
    NB = min(SWA_BLOCKS_PER_STEP, nb)
    ns = nb // NB
    prev = lambda n: jnp.maximum(n * NB - 1, 0)
    VR = dh + V_AUG
    qspec = pl.BlockSpec((G * R * dh, NB * BLK), lambda b, n: (0, b * ns + n))
    return pl.pallas_call(
        _swa_kernel,
        grid=(B, ns),
        in_specs=[qspec,
                  pl.BlockSpec((G, BLK, dh), lambda b, n: (b, prev(n), 0)),
                  pl.BlockSpec((G, NB * BLK, dh), lambda b, n: (b, n, 0)),
                  pl.BlockSpec((G * VR, BLK), lambda b, n: (0, b * nb + prev(n))),
                  pl.BlockSpec((G * VR, NB * BLK), lambda b, n: (0, b * ns + n)),
                  pl.BlockSpec((G, 1, R * BLK), lambda b, n: (0, 0, 0))],
        out_specs=qspec,
        out_shape=jax.ShapeDtypeStruct((G * R * dh, T), BF16),
        compiler_params=_cparams(("arbitrary", "arbitrary")),
        name="swa_sink_attention",
    )(bqt, bk, bk, bvt, bvt, sink_cols)


def _mla_proj_kernel(cq_ref, ckv_ref, kpet_ref, qn_ref, kvn_ref, wqt_ref, wkvt_ref, c_ref, s_ref,
                     qf_ref, kf_ref, vt_ref):
    hq = _rms(cq_ref[...], qn_ref[...]).astype(BF16)
    hkv = _rms(ckv_ref[...], kvn_ref[...]).astype(BF16)
    qt = _dot_nt(wqt_ref[...], hq)
    kvt = _dot_nt(wkvt_ref[...], hkv)
    c = c_ref[...]
    s = s_ref[...]
    kpe = kpet_ref[...]
    scale = (C_NOPE + C_ROPE) ** -0.5 * LOG2E
    half = C_ROPE // 2
    tm = qt.shape[1]
    zpad = jnp.zeros((C_QK_PAD - C_NOPE - C_ROPE, tm), BF16)
    zpad_f32 = jnp.zeros((C_QK_PAD - C_NOPE - C_ROPE, tm), F32)
    ones_row = (lax.broadcasted_iota(jnp.int32, (V_AUG, tm), 0) == 0).astype(BF16)
    for h in range(C_HEADS):
        b = h * C_QK_PAD
        qf_ref[h, 0:C_NOPE, :] = (qt[b:b + C_NOPE] * scale).astype(BF16)
        x1 = qt[b + C_NOPE:b + C_NOPE + half]
        x2 = qt[b + C_NOPE + half:b + C_NOPE + C_ROPE]
        qf_ref[h, C_NOPE:C_NOPE + half, :] = ((x1 * c - x2 * s) * scale).astype(BF16)
        qf_ref[h, C_NOPE + half:C_NOPE + C_ROPE, :] = ((x2 * c + x1 * s) * scale).astype(BF16)
        qf_ref[h, C_NOPE + C_ROPE:C_QK_PAD, :] = zpad
        kft = jnp.concatenate([kvt[b:b + C_NOPE], kpe, zpad_f32], axis=0)
        kf_ref[h] = kft.T.astype(BF16)
        vt_ref[h, 0:C_V, :] = kvt[b + C_NOPE:b + C_NOPE + C_V].astype(BF16)
        vt_ref[h, C_V:C_V + V_AUG, :] = ones_row


def _mla_proj(cq, ckv, kpet, q_norm, w_q_up, kv_norm, w_kv_up, c32t, s32t, tm):
    T = cq.shape[0]
    H = C_HEADS
    wq = w_q_up.reshape(C_Q_RANK, H, C_NOPE + C_ROPE)
    wq = jnp.pad(wq, ((0, 0), (0, 0), (0, C_QK_PAD - C_NOPE - C_ROPE))).reshape(C_Q_RANK, H * C_QK_PAD)
    wqt = wq.T.astype(BF16)
    wkvt = w_kv_up.T.astype(BF16)
    full = lambda a: pl.BlockSpec(a.shape, lambda i: (0,) * a.ndim)
    qn = q_norm.reshape(1, C_Q_RANK)
    kvn = kv_norm.reshape(1, C_KV_RANK)
    return pl.pallas_call(
        _mla_proj_kernel,
        grid=(T // tm,),
        in_specs=[pl.BlockSpec((tm, C_Q_RANK), lambda i: (i, 0)),
                  pl.BlockSpec((tm, C_KV_RANK), lambda i: (i, 0)),
                  pl.BlockSpec((C_ROPE, tm), lambda i: (0, i)),
                  full(qn), full(kvn), full(wqt), full(wkvt),
                  pl.BlockSpec((C_ROPE // 2, tm), lambda i: (0, i)),
                  pl.BlockSpec((C_ROPE // 2, tm), lambda i: (0, i))],
        out_specs=[pl.BlockSpec((H, C_QK_PAD, tm), lambda i: (0, 0, i)),
                   pl.BlockSpec((H, tm, C_QK_PAD), lambda i: (0, i, 0)),
                   pl.BlockSpec((H, C_V + V_AUG, tm), lambda i: (0, 0, i))],
        out_shape=[jax.ShapeDtypeStruct((H, C_QK_PAD, T), BF16),
                   jax.ShapeDtypeStruct((H, T, C_QK_PAD), BF16),
                   jax.ShapeDtypeStruct((H, C_V + V_AUG, T), BF16)],
        compiler_params=_cparams(("arbitrary",)),
        name="mla_projection",
    )(cq, ckv, kpet, qn, kvn, wqt, wkvt, c32t, s32t)


MLA_HEADS_PER_STEP = 2


def _mla_kernel(q_ref, kf_ref, vt_ref, o_ref, sa_ref, sb_ref):
    HB = q_ref.shape[0]
    S = q_ref.shape[2]
    KT, QT = sa_ref.shape[1], sa_ref.shape[2]
    nq = S // QT
    lane = lax.broadcasted_iota(jnp.int32, (1, QT), 1)
    kiota = lax.broadcasted_iota(jnp.int32, (KT, 1), 0)

    def scores(qi, t, dst):
        k0 = pl.multiple_of(t * KT, KT)
        q0 = pl.multiple_of(qi * QT, QT)
        for h in range(HB):
            dst[h] = _dot(kf_ref[h, pl.ds(k0, KT), :], q_ref[h, :, pl.ds(q0, QT)])

    def update(qi, t, src, carry, causal):
        k0 = pl.multiple_of(t * KT, KT)
        out = []
        for h in range(HB):
            m_i, acc = carry[2 * h], carry[2 * h + 1]
            s = src[h]
            if causal:
                s = jnp.where((k0 + kiota) <= qi * QT + lane, s, NEG_INF)
            m_new = jnp.maximum(m_i, jnp.max(s, axis=0, keepdims=True))
            alpha = jnp.exp2(m_i - m_new)
            p = jnp.exp2(s - m_new)
            acc = alpha * acc + _dot(vt_ref[h, :, pl.ds(k0, KT)], p.astype(BF16))
            out.extend((m_new, acc))
        return tuple(out)

    def q_tile(qi, _):
        def step(j, carry):
            scores(qi, 2 * j + 1, sb_ref)
            carry = update(qi, 2 * j, sa_ref, carry, False)
            scores(qi, 2 * j + 2, sa_ref)
            return update(qi, 2 * j + 1, sb_ref, carry, False)

        init = (jnp.full((1, QT), NEG_INF, F32), jnp.zeros((C_V + V_AUG, QT), F32)) * HB
        carry = _tile_loop(qi, step, init)
        scores(qi, 2 * qi + 1, sb_ref)
        carry = update(qi, 2 * qi, sa_ref, carry, True)
        scores(jnp.minimum(qi + 1, nq - 1), 0, sa_ref)
        carry = update(qi, 2 * qi + 1, sb_ref, carry, True)
        q0 = pl.multiple_of(qi * QT, QT)
        for h in range(HB):
            acc = carry[2 * h + 1]
            o_ref[h * C_V:(h + 1) * C_V, pl.ds(q0, QT)] = (acc[0:C_V] * (1.0 / acc[C_V:C_V + 1])).astype(o_ref.dtype)
        return 0

    scores(0, 0, sa_ref)
    lax.fori_loop(0, nq, q_tile, 0)


def _mla(qft, kf, vt, B, S):
    H, HB = C_HEADS, MLA_HEADS_PER_STEP
    T = B * S
    QT = min(KV_TILE, S)
    nq = S // QT
    return pl.pallas_call(
        _mla_kernel,
        grid=(B, H // HB),
        in_specs=[pl.BlockSpec((HB, C_QK_PAD, S), lambda b, h: (h, 0, b)),
                  pl.BlockSpec((HB, S, C_QK_PAD), lambda b, h: (h, b, 0)),
                  pl.BlockSpec((HB, C_V + V_AUG, S), lambda b, h: (h, 0, b))],
        out_specs=pl.BlockSpec((HB * C_V, S), lambda b, h: (h, b)),
        out_shape=jax.ShapeDtypeStruct((H * C_V, T), BF16),
        scratch_shapes=[pltpu.VMEM((HB, QT // 2, QT), F32), pltpu.VMEM((HB, QT // 2, QT), F32)],
        compiler_params=_cparams(("arbitrary", "arbitrary")),
        name="mla_attention",
    )(qft, kf, vt)


def _merge_kernel(x_ref, oa_ref, ob_ref, oc_ref, nrm_ref, wg_ref, wa_ref, wb_ref, wc_ref, wo_ref, o_ref):
    x = x_ref[...]
    h = _rms(x, nrm_ref[...]).astype(BF16)
    g = _sigmoid(_dot(h, wg_ref[...]))
    D = D_MODEL
    m = (g[:, 0:D] * _dot_tn(oa_ref[...], wa_ref[...]) + g[:, D:2 * D] * _dot_tn(ob_ref[...], wb_ref[...])
         + g[:, 2 * D:3 * D] * _dot_tn(oc_ref[...], wc_ref[...]))
    o_ref[...] = x + _dot(m.astype(BF16), wo_ref[...])


def _merge(x, oat, obt, oct, mix_norm, w_gate, w_a, w_b, w_c, w_out, tm):
    T = x.shape[0]
    row = lambda n: pl.BlockSpec((tm, n), lambda i: (i, 0))
    colT = lambda n: pl.BlockSpec((n, tm), lambda i: (0, i))
    full = lambda a: pl.BlockSpec(a.shape, lambda i: (0,) * a.ndim)
    nrm = mix_norm.reshape(1, D_MODEL)
    ws = [w.astype(BF16) for w in (w_gate, w_a, w_b, w_c, w_out)]
    return pl.pallas_call(
        _merge_kernel,
        grid=(T // tm,),
        in_specs=[row(D_MODEL), colT(512), colT(512), colT(512), full(nrm)] + [full(w) for w in ws],
        out_specs=row(D_MODEL),
        out_shape=jax.ShapeDtypeStruct((T, D_MODEL), F32),
        compiler_params=_cparams(("arbitrary",)),
        name="branch_merge",
    )(x, oat, obt, oct, nrm, *ws)


def _ffn_kernel(x_ref, p_ref, fn_ref, wg_ref, wu_ref, wd_ref, pn_ref, wpg_ref, wpe_ref, fin_ref, o_ref,
                *, final):
    x = x_ref[...]
    h = _rms(x, fn_ref[...]).astype(BF16)
    a = _dot(h, wg_ref[...])
    u = _dot(h, wu_ref[...])
    y = (a * _sigmoid(a) * u).astype(BF16)
    x = x + _dot(y, wd_ref[...])
    gate = _sigmoid(_dot(_rms(x, pn_ref[...]).astype(BF16), wpg_ref[...]))
    x = x + gate * _dot(p_ref[...].astype(BF16), wpe_ref[...])
    if final:
        x = _rms(x, fin_ref[...])
    o_ref[...] = x


def _ffn_ple(x, p_i, ffn_norm, w_gate, w_up, w_down, ple_norm, w_ple_gate, w_ple_proj, final_norm, final, tm):
    T = x.shape[0]
    row = lambda n: pl.BlockSpec((tm, n), lambda i: (i, 0))
    full = lambda a: pl.BlockSpec(a.shape, lambda i: (0,) * a.ndim, pipeline_mode=pl.Buffered(1))
    fn = ffn_norm.reshape(1, D_MODEL)
    pn = ple_norm.reshape(1, D_MODEL)
    fin = final_norm.reshape(1, D_MODEL)
    wg, wu, wd, wpg, wpe = [w.astype(BF16) for w in (w_gate, w_up, w_down, w_ple_gate, w_ple_proj)]
    return pl.pallas_call(
        functools.partial(_ffn_kernel, final=final),
        grid=(T // tm,),
        in_specs=[row(D_MODEL), row(PLE_DIM), full(fn), full(wg), full(wu), full(wd), full(pn),
                  full(wpg), full(wpe), full(fin)],
        out_specs=row(D_MODEL),
        out_shape=jax.ShapeDtypeStruct((T, D_MODEL), F32),
        compiler_params=_cparams(("arbitrary",)),
        name="ffn_ple",
    )(x, p_i, fn, wg, wu, wd, pn, wpg, wpe, fin)


def _layer(x, p_i, tables, B, S, final_norm, final, mix_norm, w_in, a_cmp_pos_k, a_cmp_w1_k, a_cmp_w2_k,
           a_cmp_pos_v, a_cmp_w1_v, a_cmp_w2_v, b_sinks, c_q_norm, c_w_q_up, c_kv_norm, c_w_kv_up,
           w_branch_gate, w_branch_a, w_branch_b, w_branch_c, w_out, ffn_norm, w_ffn_gate, w_ffn_up,
           w_ffn_down, ple_norm, w_ple_proj, w_ple_gate):
    T = B * S
    tm = min(512, T)
    (kc, ks_aug, kw, bk, vc, cq, ckv, aqt, bqt, vst, vwt, bvt, gt, kpet) = _inproj(
        x, mix_norm, w_in, tables, tm, B, S)

    k_cmp, v_cmp_t = _compress(kc, vc, a_cmp_pos_k, a_cmp_w1_k, a_cmp_w2_k,
                               a_cmp_pos_v, a_cmp_w1_v, a_cmp_w2_v, B, S)
    oat = _nsa(aqt, k_cmp, v_cmp_t, ks_aug, vst, kw, vwt, gt, B, S)
    obt = _swa(bqt, bk, bvt, b_sinks, B, S)

    qft, kf, vt = _mla_proj(cq, ckv, kpet, c_q_norm, c_w_q_up, c_kv_norm, c_w_kv_up,
                            tables[2], tables[3], tm)
    oct = _mla(qft, kf, vt, B, S)

    x = _merge(x, oat, obt, oct, mix_norm, w_branch_gate, w_branch_a, w_branch_b, w_branch_c,
               w_out, min(256, T))
    return _ffn_ple(x, p_i, ffn_norm, w_ffn_gate, w_ffn_up, w_ffn_down, ple_norm, w_ple_gate, w_ple_proj,
                    final_norm, final, min(512, T))


def kernel(x, p, positions, mix_norm, w_in, a_cmp_pos_k, a_cmp_w1_k, a_cmp_w2_k, a_cmp_pos_v, a_cmp_w1_v, a_cmp_w2_v, b_sinks, c_q_norm, c_w_q_up, c_kv_norm, c_w_kv_up, w_branch_gate, w_branch_a, w_branch_b, w_branch_c, w_out, ffn_norm, w_ffn_gate, w_ffn_up, w_ffn_down, ple_norm, w_ple_proj, w_ple_gate, final_norm):
    B, S, D = x.shape
    depth = p.shape[0]
    T = B * S
    tables = _rope_tables(positions, min(512, T))
    xf = x.reshape(T, D)
    per_layer = (mix_norm, w_in, a_cmp_pos_k, a_cmp_w1_k, a_cmp_w2_k, a_cmp_pos_v, a_cmp_w1_v, a_cmp_w2_v,
                 b_sinks, c_q_norm, c_w_q_up, c_kv_norm, c_w_kv_up, w_branch_gate, w_branch_a, w_branch_b,
                 w_branch_c, w_out, ffn_norm, w_ffn_gate, w_ffn_up, w_ffn_down, ple_norm, w_ple_proj,
                 w_ple_gate)
    for i in range(depth):
        xf = _layer(xf, p[i].reshape(T, PLE_DIM), tables, B, S, final_norm, i == depth - 1,
                    *[w[i] for w in per_layer])
    return xf.reshape(B, S, D)
```

```python
import functools
import math

import numpy as np
import jax
import jax.numpy as jnp
from jax import lax
from jax.experimental import pallas as pl
from jax.experimental.pallas import tpu as pltpu

F32 = jnp.float32
BF16 = jnp.bfloat16

D_MODEL = 1024
PLE_DIM = 256
ROPE_THETA = 10000.0
EPS = 1e-6
NEG_INF = -1e30
HEAD_DIM = 64

A_HEADS = 8
A_KV_GROUPS = 2
A_GROUP_HEADS = A_HEADS // A_KV_GROUPS
A_CMP_LEN = 32
A_CMP_STRIDE = 16
A_CMP_HIDDEN = 256
A_SEL_LEN = 64
A_TOPK = 16
A_WINDOW = 512
A_Q_CHUNK = 128
A_FORCE_BONUS = 1e4

B_HEADS = 8
B_KV_HEADS = 2
B_WINDOW = 128
B_BLOCK = 128

C_HEADS = 8
C_Q_RANK = 256
C_KV_RANK = 256
C_NOPE = 64
C_ROPE = 32
C_V = 64
C_QK_PAD = 128

D_FF = int(math.ceil(8 * D_MODEL / 3 / 256)) * 256

LANES = 128
KV_TILE = 512
LOG2E = math.log2(math.e)
V_AUG = 16
K_AUG = 64
BLOCKS_PER_TILE = KV_TILE // A_SEL_LEN
NSA_PREFIX_CLASSES = 4
VMEM_LIMIT = 56 * 1024 * 1024

_ROW_COLS = 128 + C_Q_RANK + C_KV_RANK
_T_AQ, _T_BQ = 0, 512
_T_VS, _T_VW, _T_BV = 1024, 1152, 1280
_T_G, _T_KPE = 1408, 1440
_T_KC, _T_KS, _T_KW, _T_BK = 1472, 1600, 1728, 1856
_T_ROWS = 1984
_GATE_ROWS = 16


def _cparams(sem):
    return pltpu.CompilerParams(dimension_semantics=sem, vmem_limit_bytes=VMEM_LIMIT)


def _rms(x, g):
    return x * lax.rsqrt(jnp.mean(x * x, axis=-1, keepdims=True) + EPS) * g


def _sigmoid(x):
    return 1.0 / (1.0 + jnp.exp(-x))


def _dot(a, b):
    return jnp.dot(a, b, preferred_element_type=F32)


def _dot_nt(a, b):
    return lax.dot_general(a, b, (((1,), (1,)), ((), ())), preferred_element_type=F32)


TILE_UNROLL = 4


def _tile_loop(n, step, carry):
    main = n // TILE_UNROLL

    def trip(j, c):
        for u in range(TILE_UNROLL):
            c = step(j * TILE_UNROLL + u, c)
        return c

    carry = lax.fori_loop(0, main, trip, carry)
    return lax.fori_loop(main * TILE_UNROLL, n, step, carry)


def _dot_tn(a, b):
    return lax.dot_general(a, b, (((0,), (0,)), ((), ())), preferred_element_type=F32)


def _rope_kernel(pr_ref, i64_ref, i32_ref, c64t_ref, s64t_ref, c32t_ref, s32t_ref):
    pr = pr_ref[...].astype(F32)
    a64 = i64_ref[...] * pr
    c64t_ref[...] = jnp.cos(a64)
    s64t_ref[...] = jnp.sin(a64)
    a32 = i32_ref[...] * pr
    c32t_ref[...] = jnp.cos(a32)
    s32t_ref[...] = jnp.sin(a32)


def _rope_tables(positions, tm):
    T = positions.size
    pos = positions.reshape(T)
    inv64 = jnp.power(F32(ROPE_THETA), -jnp.arange(0, HEAD_DIM, 2, dtype=F32) / HEAD_DIM)
    inv32 = jnp.power(F32(ROPE_THETA), -jnp.arange(0, C_ROPE, 2, dtype=F32) / C_ROPE)
    full = lambda shape: pl.BlockSpec(shape, lambda i: (0,) * len(shape))
    return pl.pallas_call(
        _rope_kernel,
        grid=(T // tm,),
        in_specs=[pl.BlockSpec((1, tm), lambda i: (0, i)), full((32, 1)), full((16, 1))],
        out_specs=[pl.BlockSpec((32, tm), lambda i: (0, i)),
                   pl.BlockSpec((32, tm), lambda i: (0, i)),
                   pl.BlockSpec((16, tm), lambda i: (0, i)),
                   pl.BlockSpec((16, tm), lambda i: (0, i))],
        out_shape=[jax.ShapeDtypeStruct((32, T), F32), jax.ShapeDtypeStruct((32, T), F32),
                   jax.ShapeDtypeStruct((16, T), F32), jax.ShapeDtypeStruct((16, T), F32)],
        compiler_params=_cparams(("arbitrary",)),
        name="rope_tables",
    )(pos.reshape(1, T), inv64.reshape(32, 1), inv32.reshape(16, 1))


def _store_v_aug(ref, vt, n_heads):
    dv = vt.shape[0] // n_heads
    tm = vt.shape[1]
    ones_row = (lax.broadcasted_iota(jnp.int32, (V_AUG, tm), 0) == 0).astype(ref.dtype)
    for g in range(n_heads):
        r0 = g * (dv + V_AUG)
        ref[r0:r0 + dv, :] = vt[g * dv:(g + 1) * dv].astype(ref.dtype)
        ref[r0 + dv:r0 + dv + V_AUG, :] = ones_row

def _inproj_kernel(x_ref, nrm_ref, wr_ref, wt_ref, c64t_ref, s64t_ref, c32t_ref, s32t_ref,
                   kc_ref, ks_ref, kw_ref, bk_ref, vc_ref, cq_ref, ckv_ref,
                   aqt_ref, bqt_ref, vst_ref, vwt_ref, bvt_ref, gt_ref, kpet_ref, *, n_seq_tiles):
    h = _rms(x_ref[...], nrm_ref[...]).astype(BF16)
    zr = _dot(h, wr_ref[...])
    tm, dh = zr.shape[0], HEAD_DIM
    for g in range(A_KV_GROUPS):
        vc_ref[g] = zr[:, g * dh:(g + 1) * dh]
    cq_ref[...] = zr[:, 128:128 + C_Q_RANK]
    ckv_ref[...] = zr[:, 128 + C_Q_RANK:128 + C_Q_RANK + C_KV_RANK]

    zt = _dot_nt(wt_ref[...], h)
    ct = c64t_ref[...]
    st = s64t_ref[...]

    def roped_t(r0):
        x1 = zt[r0:r0 + 32]
        x2 = zt[r0 + 32:r0 + 64]
        return jnp.concatenate([x1 * ct - x2 * st, x2 * ct + x1 * st], axis=0)

    kc = jnp.concatenate([roped_t(_T_KC), roped_t(_T_KC + dh)], axis=0).T
    kw = jnp.concatenate([roped_t(_T_KW), roped_t(_T_KW + dh)], axis=0).T
    bk = jnp.concatenate([roped_t(_T_BK), roped_t(_T_BK + dh)], axis=0).T
    pos = (pl.program_id(0) % n_seq_tiles) * tm + lax.broadcasted_iota(jnp.int32, (tm, 128), 0)
    lane = lax.broadcasted_iota(jnp.int32, (tm, 128), 1)
    onehot = (lane - dh == (pos // A_SEL_LEN) % BLOCKS_PER_TILE).astype(F32)
    zero_rows = jnp.zeros((K_AUG, tm), F32)
    for g in range(A_KV_GROUPS):
        kc_ref[g] = kc[:, g * dh:(g + 1) * dh]
        kw_ref[g] = kw[:, g * dh:(g + 1) * dh].astype(BF16)
        bk_ref[g] = bk[:, g * dh:(g + 1) * dh].astype(BF16)
        ks = jnp.concatenate([roped_t(_T_KS + g * dh), zero_rows], axis=0).T
        ks_ref[g] = (ks + onehot).astype(BF16)

    qscale = HEAD_DIM ** -0.5 * LOG2E
    for q_ref, base in ((aqt_ref, _T_AQ), (bqt_ref, _T_BQ)):
        for hh in range(8):
            r0 = base + hh * HEAD_DIM
            x1 = zt[r0:r0 + 32]
            x2 = zt[r0 + 32:r0 + 64]
            q_ref[hh * 64:hh * 64 + 32, :] = ((x1 * ct - x2 * st) * qscale).astype(BF16)
            q_ref[hh * 64 + 32:hh * 64 + 64, :] = ((x2 * ct + x1 * st) * qscale).astype(BF16)
    _store_v_aug(vst_ref, zt[_T_VS:_T_VS + 128], A_KV_GROUPS)
    _store_v_aug(vwt_ref, zt[_T_VW:_T_VW + 128], A_KV_GROUPS)
    _store_v_aug(bvt_ref, zt[_T_BV:_T_BV + 128], B_KV_HEADS)
    gt_ref[...] = _sigmoid(zt[_T_G:_T_G + 2 * _GATE_ROWS])
    c3 = c32t_ref[...]
    s3 = s32t_ref[...]
    x1 = zt[_T_KPE:_T_KPE + 16]
    x2 = zt[_T_KPE + 16:_T_KPE + 32]
    kpet_ref[0:16, :] = x1 * c3 - x2 * s3
    kpet_ref[16:32, :] = x2 * c3 + x1 * s3


def _inproj(x, mix_norm, w_in, tables, tm, B, S):
    T = x.shape[0]
    G, dh = A_KV_GROUPS, HEAD_DIM
    c64t, s64t, c32t, s32t = tables
    sizes = (512, 128, 128, 128, 128, 128, 128, 24, 512, 128, 128, 256, 256, 32)
    offs = np.concatenate([[0], np.cumsum(sizes)])
    (w_aq, w_kc, w_vc, w_ks, w_vs, w_kw, w_vw, w_g, w_bq, w_bk, w_bv, w_cq, w_ckv, w_kpe) = [
        w_in[:, int(offs[i]):int(offs[i + 1])] for i in range(len(sizes))]
    w_row = jnp.concatenate([w_vc, w_cq, w_ckv], axis=1).astype(BF16)
    wg = w_g.reshape(D_MODEL, A_KV_GROUPS, A_GROUP_HEADS * 3)
    wg = jnp.pad(wg, ((0, 0), (0, 0), (0, _GATE_ROWS - A_GROUP_HEADS * 3))).reshape(D_MODEL, 2 * _GATE_ROWS)
    w_t = jnp.concatenate([w_aq, w_bq, w_vs, w_vw, w_bv, wg, w_kpe, w_kc, w_ks, w_kw, w_bk],
                          axis=1).T.astype(BF16)

    row = lambda n: pl.BlockSpec((tm, n), lambda i: (i, 0))
    colT = lambda n: pl.BlockSpec((n, tm), lambda i: (0, i))
    full = lambda a: pl.BlockSpec(a.shape, lambda i: (0,) * a.ndim)
    nrm = mix_norm.reshape(1, D_MODEL)
    vrows = A_KV_GROUPS * (HEAD_DIM + V_AUG)
    nst = S // tm
    grp = lambda w: pl.BlockSpec((G, tm, w), lambda i: (i // nst, i % nst, 0))
    outs = pl.pallas_call(
        functools.partial(_inproj_kernel, n_seq_tiles=nst),
        grid=(T // tm,),
        in_specs=[row(D_MODEL), full(nrm), full(w_row), full(w_t), colT(32), colT(32), colT(16), colT(16)],
        out_specs=[grp(dh), grp(dh + K_AUG), grp(dh), grp(dh), grp(dh), row(C_Q_RANK), row(C_KV_RANK),
                   colT(512), colT(512), colT(vrows), colT(vrows), colT(vrows), colT(2 * _GATE_ROWS), colT(32)],
        out_shape=[jax.ShapeDtypeStruct((B * G, S, dh), F32), jax.ShapeDtypeStruct((B * G, S, dh + K_AUG), BF16),
                   jax.ShapeDtypeStruct((B * G, S, dh), BF16), jax.ShapeDtypeStruct((B * G, S, dh), BF16),
                   jax.ShapeDtypeStruct((B * G, S, dh), F32), jax.ShapeDtypeStruct((T, C_Q_RANK), F32),
                   jax.ShapeDtypeStruct((T, C_KV_RANK), F32),
                   jax.ShapeDtypeStruct((512, T), BF16), jax.ShapeDtypeStruct((512, T), BF16),
                   jax.ShapeDtypeStruct((vrows, T), BF16), jax.ShapeDtypeStruct((vrows, T), BF16),
                   jax.ShapeDtypeStruct((vrows, T), BF16), jax.ShapeDtypeStruct((2 * _GATE_ROWS, T), F32),
                   jax.ShapeDtypeStruct((32, T), F32)],
        compiler_params=_cparams(("arbitrary",)),
        name="in_projection",
    )(x, nrm, w_row, w_t, c64t, s64t, c32t, s32t)
    return outs


def _gelu_tanh(x):
    return 0.5 * x * (1.0 + jnp.tanh(math.sqrt(2.0 / math.pi) * (x + 0.044715 * (x * x * x))))


def _cmp_kernel(seg_ref, pos_ref, w1_ref, w2_ref, o_ref, *, transposed):
    seg = seg_ref[0]
    n_seg = seg.shape[0]
    half = seg.shape[1]
    a = (seg + pos_ref[0:1, :]).astype(BF16)
    b = (seg + pos_ref[1:2, :]).astype(BF16)
    u = _dot(a, w1_ref[0:half, :])
    v = _dot(b, w1_ref[half:2 * half, :])
    pre = u + pltpu.roll(v, n_seg - 1, 0)
    g = _gelu_tanh(pre).astype(BF16)
    if transposed:
        o_ref[0] = _dot_nt(w2_ref[...], g).astype(BF16)
    else:
        o_ref[0] = _dot(g, w2_ref[...]).astype(BF16)


def _compress_one(t, pos, w1, w2, B, S, transposed):
    G, dh = A_KV_GROUPS, HEAD_DIM
    n_seg = S // A_CMP_STRIDE
    seg_w = A_CMP_STRIDE * dh
    seg = t.reshape(B * G, n_seg, seg_w)
    w2 = (w2.T if transposed else w2).astype(BF16)
    out_blk = (1, dh, n_seg) if transposed else (1, n_seg, dh)
    full = lambda a: pl.BlockSpec(a.shape, lambda i: (0,) * a.ndim)
    pos = pos.reshape(2, seg_w)
    w1 = w1.astype(BF16)
    return pl.pallas_call(
        functools.partial(_cmp_kernel, transposed=transposed),
        grid=(B * G,),
        in_specs=[pl.BlockSpec((1, n_seg, seg_w), lambda i: (i, 0, 0)), full(pos), full(w1), full(w2)],
        out_specs=pl.BlockSpec(out_blk, lambda i: (i, 0, 0)),
        out_shape=jax.ShapeDtypeStruct((B * G,) + out_blk[1:], BF16),
        compiler_params=_cparams(("arbitrary",)),
        name="nsa_compress_v" if transposed else "nsa_compress_k",
    )(seg, pos, w1, w2)


def _compress(kc, vc, pos_k, w1_k, w2_k, pos_v, w1_v, w2_v, B, S):
    return (_compress_one(kc, pos_k, w1_k, w2_k, B, S, False),
            _compress_one(vc, pos_v, w1_v, w2_v, B, S, True))


def _lane_heads(blk, dh, n):
    return jnp.concatenate([blk[r * dh:(r + 1) * dh] for r in range(n)], axis=1)


def _argmax_lowest(val, idx):
    n = val.shape[0]
    v = [val[i:i + 8] for i in range(0, n, 8)]
    j = [idx[i:i + 8] for i in range(0, n, 8)]
    while len(v) > 1:
        nv, nj = [], []
        for a in range(0, len(v) - 1, 2):
            take_b = v[a + 1] > v[a]
            nv.append(jnp.where(take_b, v[a + 1], v[a]))
            nj.append(jnp.where(take_b, j[a + 1], j[a]))
        if len(v) % 2:
            nv.append(v[-1])
            nj.append(j[-1])
        v, j = nv, nj
    vmax = jnp.max(v[0], axis=0, keepdims=True)
    return jnp.min(jnp.where(v[0] == vmax, j[0], float(n)), axis=0, keepdims=True)


def _nsa_select(g, q_ref, kc_ref, vct_ref, smt_ref, selb_ref, qa_ref, tq, tq1, n_seg, n_sel):
    QC, R, dh = A_Q_CHUNK, A_GROUP_HEADS, HEAD_DIM
    NQ = R * QC
    qT = _lane_heads(q_ref[g * R * dh:(g + 1) * R * dh, :], dh, R)
    qa_ref[g] = qT

    s = _dot(kc_ref[g, 0:n_seg, :], qT)
    cend = lax.broadcasted_iota(jnp.int32, (n_seg, 1), 0) * A_CMP_STRIDE + (A_CMP_LEN - 1)
    s = jnp.where(cend <= tq, s, NEG_INF)
    m = jnp.max(s, axis=0, keepdims=True)
    e = jnp.exp2(s - m)
    l = jnp.sum(e, axis=0, keepdims=True)
    inv_l = jnp.where(tq >= A_CMP_LEN - 1, 1.0 / l, 0.0)
    p = e * inv_l
    o_cmp = _dot(vct_ref[g, :, 0:n_seg], p.astype(BF16))

    psum = p[:, 0:QC]
    for r in range(1, R):
        psum = psum + p[:, r * QC:(r + 1) * QC]
    hi = psum.astype(BF16)
    lo = (psum - hi.astype(F32)).astype(BF16)
    smt = smt_ref[0:n_sel, 0:n_seg]
    imp = _dot(smt, hi) + _dot(smt, lo)
    jid = lax.broadcasted_iota(jnp.int32, (n_sel, QC), 0).astype(F32)
    cur = (tq1 // A_SEL_LEN).astype(F32)
    valid = jid <= cur
    forced = (jid == 0.0) | (jid == cur) | (jid == cur - 1.0)

    ineligible, picked = -(2.0 ** 127), -(2.0 ** 126)
    val = jnp.where(valid & jnp.logical_not(forced), imp, ineligible)
    for _ in range(min(A_TOPK, selb_ref.shape[1]) - 3):
        first = _argmax_lowest(val, jid)
        val = jnp.where(jid == first, picked, val)
    selb_ref[g, 0:n_sel, :] = jnp.where((forced | (val == picked)) & valid, 0.0, NEG_INF)
    return o_cmp


def _nsa_kernel(q_ref, kc_ref, vct_ref, smt_ref, ks_ref, vst_ref, kw_ref, vwt_ref, g_ref,
                o_ref, selb_ref, qa_ref, sa_ref, sb_ref):
    QC, R, dh, G = A_Q_CHUNK, A_GROUP_HEADS, HEAD_DIM, A_KV_GROUPS
    NQ = R * QC
    VR = dh + V_AUG
    c = pl.program_id(1)
    t0 = c * QC
    tq = t0 + (lax.broadcasted_iota(jnp.int32, (1, NQ), 1) & (QC - 1))
    tq1 = t0 + lax.broadcasted_iota(jnp.int32, (1, QC), 1)
    gt = g_ref[...]

    def before_key_loop(n_seg, n_sel, window_may_clip):
        o_cmp = [_nsa_select(g, q_ref, kc_ref, vct_ref, smt_ref, selb_ref, qa_ref, tq, tq1, n_seg, n_sel)
                 for g in range(G)]
        WL = A_WINDOW + QC
        if window_may_clip:
            start = pl.multiple_of(jnp.maximum(t0 - A_WINDOW, 0), QC)
            kp = start + lax.broadcasted_iota(jnp.int32, (WL, 1), 0)
            visible = (kp <= tq) & (kp > tq - A_WINDOW)
        else:
            start = pl.multiple_of(t0 - A_WINDOW, QC)
            io = lax.broadcasted_iota(jnp.int32, (QC, QC), 0)
            qo = lax.broadcasted_iota(jnp.int32, (QC, QC), 1)
            lo_bias = jnp.concatenate([jnp.where(io > qo, 0.0, NEG_INF)] * R, axis=1)
            hi_bias = jnp.concatenate([jnp.where(io <= qo, 0.0, NEG_INF)] * R, axis=1)
        o_cw = []
        for g in range(G):
            s_w = _dot(kw_ref[g, pl.ds(start, WL), :], qa_ref[g])
            if window_may_clip:
                s_w = jnp.where(visible, s_w, NEG_INF)
            else:
                s_w = jnp.concatenate([s_w[0:QC] + lo_bias, s_w[QC:WL - QC], s_w[WL - QC:WL] + hi_bias], axis=0)
            m_w = jnp.max(s_w, axis=0, keepdims=True)
            p_w = jnp.exp2(s_w - m_w)
            acc_w = _dot(vwt_ref[g * VR:(g + 1) * VR, pl.ds(start, WL)], p_w.astype(BF16))
            o_win = acc_w[0:dh] * (1.0 / acc_w[dh:dh + 1])
            pieces = []
            for r in range(R):
                sl = slice(r * QC, (r + 1) * QC)
                gr = g * _GATE_ROWS + 3 * r
                pieces.append(gt[gr:gr + 1] * o_cmp[g][:, sl] + gt[gr + 2:gr + 3] * o_win[:, sl])
            o_cw.append(jnp.concatenate(pieces, axis=1))
        return tuple(o_cw)

    n_seg_all, n_sel_all = kc_ref.shape[1], selb_ref.shape[1]
    n_cls = max(1, min(NSA_PREFIX_CLASSES, n_seg_all // LANES))
    chunks_per_cls = (n_seg_all // (A_Q_CHUNK // A_CMP_STRIDE)) // n_cls
    assert n_cls == 1 or chunks_per_cls * QC >= A_WINDOW
    classes = [functools.partial(before_key_loop, (k + 1) * n_seg_all // n_cls, (k + 1) * n_sel_all // n_cls, k == 0)
               for k in range(n_cls)]
    o_cw = lax.switch(c // chunks_per_cls, classes) if n_cls > 1 else classes[0]()

    bpt = BLOCKS_PER_TILE
    KT = sa_ref.shape[1]
    kiota = lax.broadcasted_iota(jnp.int32, (KT, 1), 0)
    pad_rows = jnp.zeros((16 - bpt, NQ), F32)

    zero_rows = jnp.zeros((K_AUG - 16, NQ), BF16)

    def scores(t, dst):
        k0 = pl.multiple_of(t * KT, KT)
        for g in range(G):
            b8 = selb_ref[g, pl.ds(pl.multiple_of((t // 2) * bpt, bpt), bpt), :]
            rows = jnp.concatenate([jnp.concatenate([b8] * R, axis=1), pad_rows], axis=0).astype(BF16)
            rhs = jnp.concatenate([qa_ref[g], rows, zero_rows], axis=0)
            dst[g] = _dot(ks_ref[g, pl.ds(k0, KT), :], rhs)

    def update(t, src, carry, causal):
        k0 = pl.multiple_of(t * KT, KT)
        out = []
        for g in range(G):
            m_i, acc = carry[2 * g], carry[2 * g + 1]
            s_i = src[g]
            if causal:
                cb = jnp.where((k0 + kiota) <= tq1, 0.0, NEG_INF)
                s_i = s_i + jnp.concatenate([cb] * R, axis=1)
            m_new = jnp.maximum(m_i, jnp.max(s_i, axis=0, keepdims=True))
            alpha = jnp.exp2(m_i - m_new)
            p_i = jnp.exp2(s_i - m_new)
            acc = alpha * acc + _dot(vst_ref[g * VR:(g + 1) * VR, pl.ds(k0, KT)], p_i.astype(BF16))
            out.extend((m_new, acc))
        return tuple(out)

    def sel_step(j, carry):
        scores(2 * j + 1, sb_ref)
        carry = update(2 * j, sa_ref, carry, False)
        scores(2 * j + 2, sa_ref)
        return update(2 * j + 1, sb_ref, carry, False)

    n_tiles = (t0 + QC + 2 * KT - 1) // (2 * KT)
    scores(0, sa_ref)
    init = (jnp.full((1, NQ), NEG_INF, F32), jnp.zeros((VR, NQ), F32)) * G
    carry = _tile_loop(n_tiles - 1, sel_step, init)
    last = n_tiles - 1
    scores(2 * last + 1, sb_ref)
    carry = update(2 * last, sa_ref, carry, True)
    carry = update(2 * last + 1, sb_ref, carry, True)

    outs = []
    for g in range(G):
        acc_s = carry[2 * g + 1]
        o_sel = acc_s[0:dh] * (1.0 / acc_s[dh:dh + 1])
        for r in range(R):
            sl = slice(r * QC, (r + 1) * QC)
            gr = g * _GATE_ROWS + 3 * r
            outs.append(o_cw[g][:, sl] + gt[gr + 1:gr + 2] * o_sel[:, sl])
    o_ref[...] = jnp.concatenate(outs, axis=0).astype(o_ref.dtype)


def _sel_map_t(S):
    n_cmp = (S - A_CMP_LEN) // A_CMP_STRIDE + 1
    n_seg = S // A_CMP_STRIDE
    n_sel = S // A_SEL_LEN
    tok = np.arange(n_cmp)[:, None] * A_CMP_STRIDE + np.arange(A_CMP_LEN)[None, :]
    sm = np.zeros((n_seg, n_sel), np.float32)
    np.add.at(sm, (np.repeat(np.arange(n_cmp), A_CMP_LEN), (tok // A_SEL_LEN).reshape(-1)), 1.0 / A_CMP_LEN)
    return jnp.asarray(sm.T, BF16)


def _nsa(aqt, k_cmp, v_cmp_t, ks_aug, vst, kw, vwt, gt, B, S):
    G, R, dh, QC = A_KV_GROUPS, A_GROUP_HEADS, HEAD_DIM, A_Q_CHUNK
    T = B * S
    nq = S // QC
    n_seg = S // A_CMP_STRIDE
    n_sel = S // A_SEL_LEN
    VR = dh + V_AUG
    assert S >= A_WINDOW + QC
    smt = _sel_map_t(S)
    qspec = pl.BlockSpec((G * R * dh, QC), lambda b, c: (0, b * nq + c))
    return pl.pallas_call(
        _nsa_kernel,
        grid=(B, nq),
        in_specs=[qspec,
                  pl.BlockSpec((G, n_seg, dh), lambda b, c: (b, 0, 0)),
                  pl.BlockSpec((G, dh, n_seg), lambda b, c: (b, 0, 0)),
                  pl.BlockSpec((n_sel, n_seg), lambda b, c: (0, 0)),
                  pl.BlockSpec((G, S, dh + K_AUG), lambda b, c: (b, 0, 0)),
                  pl.BlockSpec((G * VR, S), lambda b, c: (0, b)),
                  pl.BlockSpec((G, S, dh), lambda b, c: (b, 0, 0)),
                  pl.BlockSpec((G * VR, S), lambda b, c: (0, b)),
                  pl.BlockSpec((G * _GATE_ROWS, QC), lambda b, c: (0, b * nq + c))],
        out_specs=qspec,
        out_shape=jax.ShapeDtypeStruct((G * R * dh, T), BF16),
        scratch_shapes=[pltpu.VMEM((G, n_sel, QC), F32), pltpu.VMEM((G, dh, R * QC), BF16),
                        pltpu.VMEM((G, KV_TILE // 2, R * QC), F32), pltpu.VMEM((G, KV_TILE // 2, R * QC), F32)],
        compiler_params=_cparams(("arbitrary", "arbitrary")),
        name="nsa_attention",
    )(aqt, k_cmp, v_cmp_t, smt, ks_aug, vst, kw, vwt, gt)


SWA_BLOCKS_PER_STEP = 8


def _swa_kernel(q_ref, kp_ref, kc_ref, vpt_ref, vct_ref, sink_ref, o_ref):
    BLK, G, R, dh, NB = B_BLOCK, B_KV_HEADS, B_HEADS // B_KV_HEADS, HEAD_DIM, SWA_BLOCKS_PER_STEP
    NQ = R * BLK
    VR = dh + V_AUG
    n = pl.program_id(1)
    kj = lax.broadcasted_iota(jnp.int32, (2 * BLK, 1), 0) - BLK
    qi = lax.broadcasted_iota(jnp.int32, (1, NQ), 1) & (BLK - 1)
    band = (kj <= qi) & (kj > qi - B_WINDOW)
    for g in range(G):
        sink = sink_ref[g] * LOG2E
        for j in range(NB):
            cols = slice(j * BLK, (j + 1) * BLK)
            qT = _lane_heads(q_ref[g * R * dh:(g + 1) * R * dh, cols], dh, R)
            if j == 0:
                k = jnp.concatenate([kp_ref[g], kc_ref[g, 0:BLK, :]], axis=0)
                vT = jnp.concatenate([vpt_ref[g * VR:(g + 1) * VR, :], vct_ref[g * VR:(g + 1) * VR, cols]], axis=1)
                ok = band & ((n > 0) | (kj >= 0))
            else:
                k = kc_ref[g, (j - 1) * BLK:(j + 1) * BLK, :]
                vT = vct_ref[g * VR:(g + 1) * VR, (j - 1) * BLK:(j + 1) * BLK]
                ok = band
            s = jnp.where(ok, _dot(k, qT), NEG_INF)
            m = jnp.maximum(jnp.max(s, axis=0, keepdims=True), sink)
            p = jnp.exp2(s - m)
            acc = _dot(vT, p.astype(BF16))
            o = acc[0:dh] * (1.0 / (acc[dh:dh + 1] + jnp.exp2(sink - m)))
            for r in range(R):
                row0 = (g * R + r) * dh
                o_ref[row0:row0 + dh, cols] = o[:, r * BLK:(r + 1) * BLK].astype(o_ref.dtype)


def _swa(bqt, bk, bvt, sinks, B, S):
    G, R, dh, BLK = B_KV_HEADS, B_HEADS // B_KV_HEADS, HEAD_DIM, B_BLOCK
    T = B * S
    nb = S // BLK
    sink_cols = jnp.repeat(sinks.astype(F32).reshape(G, R), BLK, axis=1).reshape(G, 1, R * BLK)
    NB = min(SWA_BLOCKS_PER_STEP, nb)
    ns = nb // NB
    prev = lambda n: jnp.maximum(n * NB - 1, 0)
    VR = dh + V_AUG
    qspec = pl.BlockSpec((G * R * dh, NB * BLK), lambda b, n: (0, b * ns + n))
    return pl.pallas_call(
        _swa_kernel,
        grid=(B, ns),
        in_specs=[qspec,
                  pl.BlockSpec((G, BLK, dh), lambda b, n: (b, prev(n), 0)),
                  pl.BlockSpec((G, NB * BLK, dh), lambda b, n: (b, n, 0)),
                  pl.BlockSpec((G * VR, BLK), lambda b, n: (0, b * nb + prev(n))),
                  pl.BlockSpec((G * VR, NB * BLK), lambda b, n: (0, b * ns + n)),
                  pl.BlockSpec((G, 1, R * BLK), lambda b, n: (0, 0, 0))],
        out_specs=qspec,
        out_shape=jax.ShapeDtypeStruct((G * R * dh, T), BF16),
        compiler_params=_cparams(("arbitrary", "arbitrary")),
        name="swa_sink_attention",
    )(bqt, bk, bk, bvt, bvt, sink_cols)


def _mla_proj_kernel(cq_ref, ckv_ref, kpet_ref, qn_ref, kvn_ref, wqt_ref, wkvt_ref, c_ref, s_ref,
                     qf_ref, kf_ref, vt_ref):
    hq = _rms(cq_ref[...], qn_ref[...]).astype(BF16)
    hkv = _rms(ckv_ref[...], kvn_ref[...]).astype(BF16)
    qt = _dot_nt(wqt_ref[...], hq)
    kvt = _dot_nt(wkvt_ref[...], hkv)
    c = c_ref[...]
    s = s_ref[...]
    kpe = kpet_ref[...]
    scale = (C_NOPE + C_ROPE) ** -0.5 * LOG2E
    half = C_ROPE // 2
    tm = qt.shape[1]
    zpad = jnp.zeros((C_QK_PAD - C_NOPE - C_ROPE, tm), BF16)
    zpad_f32 = jnp.zeros((C_QK_PAD - C_NOPE - C_ROPE, tm), F32)
    ones_row = (lax.broadcasted_iota(jnp.int32, (V_AUG, tm), 0) == 0).astype(BF16)
    for h in range(C_HEADS):
        b = h * C_QK_PAD
        qf_ref[h, 0:C_NOPE, :] = (qt[b:b + C_NOPE] * scale).astype(BF16)
        x1 = qt[b + C_NOPE:b + C_NOPE + half]
        x2 = qt[b + C_NOPE + half:b + C_NOPE + C_ROPE]
        qf_ref[h, C_NOPE:C_NOPE + half, :] = ((x1 * c - x2 * s) * scale).astype(BF16)
        qf_ref[h, C_NOPE + half:C_NOPE + C_ROPE, :] = ((x2 * c + x1 * s) * scale).astype(BF16)
        qf_ref[h, C_NOPE + C_ROPE:C_QK_PAD, :] = zpad
        kft = jnp.concatenate([kvt[b:b + C_NOPE], kpe, zpad_f32], axis=0)
        kf_ref[h] = kft.T.astype(BF16)
        vt_ref[h, 0:C_V, :] = kvt[b + C_NOPE:b + C_NOPE + C_V].astype(BF16)
        vt_ref[h, C_V:C_V + V_AUG, :] = ones_row


def _mla_proj(cq, ckv, kpet, q_norm, w_q_up, kv_norm, w_kv_up, c32t, s32t, tm):
    T = cq.shape[0]
    H = C_HEADS
    wq = w_q_up.reshape(C_Q_RANK, H, C_NOPE + C_ROPE)
    wq = jnp.pad(wq, ((0, 0), (0, 0), (0, C_QK_PAD - C_NOPE - C_ROPE))).reshape(C_Q_RANK, H * C_QK_PAD)
    wqt = wq.T.astype(BF16)
    wkvt = w_kv_up.T.astype(BF16)
    full = lambda a: pl.BlockSpec(a.shape, lambda i: (0,) * a.ndim)
    qn = q_norm.reshape(1, C_Q_RANK)
    kvn = kv_norm.reshape(1, C_KV_RANK)
    return pl.pallas_call(
        _mla_proj_kernel,
        grid=(T // tm,),
        in_specs=[pl.BlockSpec((tm, C_Q_RANK), lambda i: (i, 0)),
                  pl.BlockSpec((tm, C_KV_RANK), lambda i: (i, 0)),
                  pl.BlockSpec((C_ROPE, tm), lambda i: (0, i)),
                  full(qn), full(kvn), full(wqt), full(wkvt),
                  pl.BlockSpec((C_ROPE // 2, tm), lambda i: (0, i)),
                  pl.BlockSpec((C_ROPE // 2, tm), lambda i: (0, i))],
        out_specs=[pl.BlockSpec((H, C_QK_PAD, tm), lambda i: (0, 0, i)),
                   pl.BlockSpec((H, tm, C_QK_PAD), lambda i: (0, i, 0)),
                   pl.BlockSpec((H, C_V + V_AUG, tm), lambda i: (0, 0, i))],
        out_shape=[jax.ShapeDtypeStruct((H, C_QK_PAD, T), BF16),
                   jax.ShapeDtypeStruct((H, T, C_QK_PAD), BF16),
                   jax.ShapeDtypeStruct((H, C_V + V_AUG, T), BF16)],
        compiler_params=_cparams(("arbitrary",)),
        name="mla_projection",
    )(cq, ckv, kpet, qn, kvn, wqt, wkvt, c32t, s32t)


MLA_HEADS_PER_STEP = 2


def _mla_kernel(q_ref, kf_ref, vt_ref, o_ref, sa_ref, sb_ref):
    HB = q_ref.shape[0]
    S = q_ref.shape[2]
    KT, QT = sa_ref.shape[1], sa_ref.shape[2]
    nq = S // QT
    lane = lax.broadcasted_iota(jnp.int32, (1, QT), 1)
    kiota = lax.broadcasted_iota(jnp.int32, (KT, 1), 0)

    def scores(qi, t, dst):
        k0 = pl.multiple_of(t * KT, KT)
        q0 = pl.multiple_of(qi * QT, QT)
        for h in range(HB):
            dst[h] = _dot(kf_ref[h, pl.ds(k0, KT), :], q_ref[h, :, pl.ds(q0, QT)])

    def update(qi, t, src, carry, causal):
        k0 = pl.multiple_of(t * KT, KT)
        out = []
        for h in range(HB):
            m_i, acc = carry[2 * h], carry[2 * h + 1]
            s = src[h]
            if causal:
                s = jnp.where((k0 + kiota) <= qi * QT + lane, s, NEG_INF)
            m_new = jnp.maximum(m_i, jnp.max(s, axis=0, keepdims=True))
            alpha = jnp.exp2(m_i - m_new)
            p = jnp.exp2(s - m_new)
            acc = alpha * acc + _dot(vt_ref[h, :, pl.ds(k0, KT)], p.astype(BF16))
            out.extend((m_new, acc))
        return tuple(out)

    def q_tile(qi, _):
        def step(j, carry):
            scores(qi, 2 * j + 1, sb_ref)
            carry = update(qi, 2 * j, sa_ref, carry, False)
            scores(qi, 2 * j + 2, sa_ref)
            return update(qi, 2 * j + 1, sb_ref, carry, False)

        init = (jnp.full((1, QT), NEG_INF, F32), jnp.zeros((C_V + V_AUG, QT), F32)) * HB
        carry = _tile_loop(qi, step, init)
        scores(qi, 2 * qi + 1, sb_ref)
        carry = update(qi, 2 * qi, sa_ref, carry, True)
        scores(jnp.minimum(qi + 1, nq - 1), 0, sa_ref)
        carry = update(qi, 2 * qi + 1, sb_ref, carry, True)
        q0 = pl.multiple_of(qi * QT, QT)
        for h in range(HB):
            acc = carry[2 * h + 1]
            o_ref[h * C_V:(h + 1) * C_V, pl.ds(q0, QT)] = (acc[0:C_V] * (1.0 / acc[C_V:C_V + 1])).astype(o_ref.dtype)
        return 0

    scores(0, 0, sa_ref)
    lax.fori_loop(0, nq, q_tile, 0)


def _mla(qft, kf, vt, B, S):
    H, HB = C_HEADS, MLA_HEADS_PER_STEP
    T = B * S
    QT = min(KV_TILE, S)
    nq = S // QT
    return pl.pallas_call(
        _mla_kernel,
        grid=(B, H // HB),
        in_specs=[pl.BlockSpec((HB, C_QK_PAD, S), lambda b, h: (h, 0, b)),
                  pl.BlockSpec((HB, S, C_QK_PAD), lambda b, h: (h, b, 0)),
                  pl.BlockSpec((HB, C_V + V_AUG, S), lambda b, h: (h, 0, b))],
        out_specs=pl.BlockSpec((HB * C_V, S), lambda b, h: (h, b)),
        out_shape=jax.ShapeDtypeStruct((H * C_V, T), BF16),
        scratch_shapes=[pltpu.VMEM((HB, QT // 2, QT), F32), pltpu.VMEM((HB, QT // 2, QT), F32)],
        compiler_params=_cparams(("arbitrary", "arbitrary")),
        name="mla_attention",
    )(qft, kf, vt)


def _merge_kernel(x_ref, oa_ref, ob_ref, oc_ref, nrm_ref, wg_ref, wa_ref, wb_ref, wc_ref, wo_ref, o_ref):
    x = x_ref[...]
    h = _rms(x, nrm_ref[...]).astype(BF16)
    g = _sigmoid(_dot(h, wg_ref[...]))
    D = D_MODEL
    m = (g[:, 0:D] * _dot_tn(oa_ref[...], wa_ref[...]) + g[:, D:2 * D] * _dot_tn(ob_ref[...], wb_ref[...])
         + g[:, 2 * D:3 * D] * _dot_tn(oc_ref[...], wc_ref[...]))
    o_ref[...] = x + _dot(m.astype(BF16), wo_ref[...])


def _merge(x, oat, obt, oct, mix_norm, w_gate, w_a, w_b, w_c, w_out, tm):
    T = x.shape[0]
    row = lambda n: pl.BlockSpec((tm, n), lambda i: (i, 0))
    colT = lambda n: pl.BlockSpec((n, tm), lambda i: (0, i))
    full = lambda a: pl.BlockSpec(a.shape, lambda i: (0,) * a.ndim, pipeline_mode=pl.Buffered(1))
    nrm = mix_norm.reshape(1, D_MODEL)
    ws = [w.astype(BF16) for w in (w_gate, w_a, w_b, w_c, w_out)]
    return pl.pallas_call(
        _merge_kernel,
        grid=(T // tm,),
        in_specs=[row(D_MODEL), colT(512), colT(512), colT(512), full(nrm)] + [full(w) for w in ws],
        out_specs=row(D_MODEL),
        out_shape=jax.ShapeDtypeStruct((T, D_MODEL), F32),
        compiler_params=_cparams(("arbitrary",)),
        name="branch_merge",
    )(x, oat, obt, oct, nrm, *ws)


def _ffn_kernel(x_ref, p_ref, fn_ref, wg_ref, wu_ref, wd_ref, pn_ref, wpg_ref, wpe_ref, fin_ref, o_ref,
                *, final):
    x = x_ref[...]
    h = _rms(x, fn_ref[...]).astype(BF16)
    a = _dot(h, wg_ref[...])
    u = _dot(h, wu_ref[...])
    y = (a * _sigmoid(a) * u).astype(BF16)
    x = x + _dot(y, wd_ref[...])
    gate = _sigmoid(_dot(_rms(x, pn_ref[...]).astype(BF16), wpg_ref[...]))
    x = x + gate * _dot(p_ref[...].astype(BF16), wpe_ref[...])
    if final:
        x = _rms(x, fin_ref[...])
    o_ref[...] = x


def _ffn_ple(x, p_i, ffn_norm, w_gate, w_up, w_down, ple_norm, w_ple_gate, w_ple_proj, final_norm, final, tm):
    T = x.shape[0]
    row = lambda n: pl.BlockSpec((tm, n), lambda i: (i, 0))
    full = lambda a: pl.BlockSpec(a.shape, lambda i: (0,) * a.ndim, pipeline_mode=pl.Buffered(1))
    fn = ffn_norm.reshape(1, D_MODEL)
    pn = ple_norm.reshape(1, D_MODEL)
    fin = final_norm.reshape(1, D_MODEL)
    wg, wu, wd, wpg, wpe = [w.astype(BF16) for w in (w_gate, w_up, w_down, w_ple_gate, w_ple_proj)]
    return pl.pallas_call(
        functools.partial(_ffn_kernel, final=final),
        grid=(T // tm,),
        in_specs=[row(D_MODEL), row(PLE_DIM), full(fn), full(wg), full(wu), full(wd), full(pn),
                  full(wpg), full(wpe), full(fin)],
        out_specs=row(D_MODEL),
        out_shape=jax.ShapeDtypeStruct((T, D_MODEL), F32),
        compiler_params=_cparams(("arbitrary",)),
        name="ffn_ple",
    )(x, p_i, fn, wg, wu, wd, pn, wpg, wpe, fin)


def _layer(x, p_i, tables, B, S, final_norm, final, mix_norm, w_in, a_cmp_pos_k, a_cmp_w1_k, a_cmp_w2_k,
           a_cmp_pos_v, a_cmp_w1_v, a_cmp_w2_v, b_sinks, c_q_norm, c_w_q_up, c_kv_norm, c_w_kv_up,
           w_branch_gate, w_branch_a, w_branch_b, w_branch_c, w_out, ffn_norm, w_ffn_gate, w_ffn_up,
           w_ffn_down, ple_norm, w_ple_proj, w_ple_gate):
    T = B * S
    tm = min(512, T)
    (kc, ks_aug, kw, bk, vc, cq, ckv, aqt, bqt, vst, vwt, bvt, gt, kpet) = _inproj(
        x, mix_norm, w_in, tables, tm, B, S)

    k_cmp, v_cmp_t = _compress(kc, vc, a_cmp_pos_k, a_cmp_w1_k, a_cmp_w2_k,
                               a_cmp_pos_v, a_cmp_w1_v, a_cmp_w2_v, B, S)
    oat = _nsa(aqt, k_cmp, v_cmp_t, ks_aug, vst, kw, vwt, gt, B, S)
    obt = _swa(bqt, bk, bvt, b_sinks, B, S)

    qft, kf, vt = _mla_proj(cq, ckv, kpet, c_q_norm, c_w_q_up, c_kv_norm, c_w_kv_up,
                            tables[2], tables[3], tm)
    oct = _mla(qft, kf, vt, B, S)

    x = _merge(x, oat, obt, oct, mix_norm, w_branch_gate, w_branch_a, w_branch_b, w_branch_c,
               w_out, min(512, T))
    return _ffn_ple(x, p_i, ffn_norm, w_ffn_gate, w_ffn_up, w_ffn_down, ple_norm, w_ple_gate, w_ple_proj,
                    final_norm, final, min(512, T))


def kernel(x, p, positions, mix_norm, w_in, a_cmp_pos_k, a_cmp_w1_k, a_cmp_w2_k, a_cmp_pos_v, a_cmp_w1_v, a_cmp_w2_v, b_sinks, c_q_norm, c_w_q_up, c_kv_norm, c_w_kv_up, w_branch_gate, w_branch_a, w_branch_b, w_branch_c, w_out, ffn_norm, w_ffn_gate, w_ffn_up, w_ffn_down, ple_norm, w_ple_proj, w_ple_gate, final_norm):
    B, S, D = x.shape
    depth = p.shape[0]
    T = B * S
    tables = _rope_tables(positions, min(512, T))
    xf = x.reshape(T, D)
    per_layer = (mix_norm, w_in, a_cmp_pos_k, a_cmp_w1_k, a_cmp_w2_k, a_cmp_pos_v, a_cmp_w1_v, a_cmp_w2_v,
                 b_sinks, c_q_norm, c_w_q_up, c_kv_norm, c_w_kv_up, w_branch_gate, w_branch_a, w_branch_b,
                 w_branch_c, w_out, ffn_norm, w_ffn_gate, w_ffn_up, w_ffn_down, ple_norm, w_ple_proj,
                 w_ple_gate)
    for i in range(depth):
        xf = _layer(xf, p[i].reshape(T, PLE_DIM), tables, B, S, final_norm, i == depth - 1,
                    *[w[i] for w in per_layer])
    return xf.reshape(B, S, D)
```

```python
import functools
import math

import numpy as np
import jax
import jax.numpy as jnp
from jax import lax
from jax.experimental import pallas as pl
from jax.experimental.pallas import tpu as pltpu

F32 = jnp.float32
BF16 = jnp.bfloat16

D_MODEL = 1024
PLE_DIM = 256
ROPE_THETA = 10000.0
EPS = 1e-6
NEG_INF = -1e30
HEAD_DIM = 64

A_HEADS = 8
A_KV_GROUPS = 2
A_GROUP_HEADS = A_HEADS // A_KV_GROUPS
A_CMP_LEN = 32
A_CMP_STRIDE = 16
A_CMP_HIDDEN = 256
A_SEL_LEN = 64
A_TOPK = 16
A_WINDOW = 512
A_Q_CHUNK = 128
A_FORCE_BONUS = 1e4

B_HEADS = 8
B_KV_HEADS = 2
B_WINDOW = 128
B_BLOCK = 128

C_HEADS = 8
C_Q_RANK = 256
C_KV_RANK = 256
C_NOPE = 64
C_ROPE = 32
C_V = 64
C_QK_PAD = 128

D_FF = int(math.ceil(8 * D_MODEL / 3 / 256)) * 256

LANES = 128
KV_TILE = 512
LOG2E = math.log2(math.e)
V_AUG = 16
K_AUG = 64
BLOCKS_PER_TILE = KV_TILE // A_SEL_LEN
NSA_PREFIX_CLASSES = 4
VMEM_LIMIT = 56 * 1024 * 1024

_ROW_COLS = 128 + C_Q_RANK + C_KV_RANK
_T_AQ, _T_BQ = 0, 512
_T_VS, _T_VW, _T_BV = 1024, 1152, 1280
_T_G, _T_KPE = 1408, 1440
_T_KC, _T_KS, _T_KW, _T_BK = 1472, 1600, 1728, 1856
_T_ROWS = 1984
_GATE_ROWS = 16


def _cparams(sem):
    return pltpu.CompilerParams(dimension_semantics=sem, vmem_limit_bytes=VMEM_LIMIT)


def _rms(x, g):
    return x * lax.rsqrt(jnp.mean(x * x, axis=-1, keepdims=True) + EPS) * g


def _sigmoid(x):
    return 1.0 / (1.0 + jnp.exp(-x))


def _dot(a, b):
    return jnp.dot(a, b, preferred_element_type=F32)


def _dot_nt(a, b):
    return lax.dot_general(a, b, (((1,), (1,)), ((), ())), preferred_element_type=F32)


TILE_UNROLL = 4


def _tile_loop(n, step, carry):
    main = n // TILE_UNROLL

    def trip(j, c):
        for u in range(TILE_UNROLL):
            c = step(j * TILE_UNROLL + u, c)
        return c

    carry = lax.fori_loop(0, main, trip, carry)
    return lax.fori_loop(main * TILE_UNROLL, n, step, carry)


def _dot_tn(a, b):
    return lax.dot_general(a, b, (((0,), (0,)), ((), ())), preferred_element_type=F32)


def _rope_kernel(pr_ref, i64_ref, i32_ref, c64t_ref, s64t_ref, c32t_ref, s32t_ref):
    pr = pr_ref[...].astype(F32)
    a64 = i64_ref[...] * pr
    c64t_ref[...] = jnp.cos(a64)
    s64t_ref[...] = jnp.sin(a64)
    a32 = i32_ref[...] * pr
    c32t_ref[...] = jnp.cos(a32)
    s32t_ref[...] = jnp.sin(a32)


def _rope_tables(positions, tm):
    T = positions.size
    pos = positions.reshape(T)
    inv64 = jnp.power(F32(ROPE_THETA), -jnp.arange(0, HEAD_DIM, 2, dtype=F32) / HEAD_DIM)
    inv32 = jnp.power(F32(ROPE_THETA), -jnp.arange(0, C_ROPE, 2, dtype=F32) / C_ROPE)
    full = lambda shape: pl.BlockSpec(shape, lambda i: (0,) * len(shape))
    return pl.pallas_call(
        _rope_kernel,
        grid=(T // tm,),
        in_specs=[pl.BlockSpec((1, tm), lambda i: (0, i)), full((32, 1)), full((16, 1))],
        out_specs=[pl.BlockSpec((32, tm), lambda i: (0, i)),
                   pl.BlockSpec((32, tm), lambda i: (0, i)),
                   pl.BlockSpec((16, tm), lambda i: (0, i)),
                   pl.BlockSpec((16, tm), lambda i: (0, i))],
        out_shape=[jax.ShapeDtypeStruct((32, T), F32), jax.ShapeDtypeStruct((32, T), F32),
                   jax.ShapeDtypeStruct((16, T), F32), jax.ShapeDtypeStruct((16, T), F32)],
        compiler_params=_cparams(("arbitrary",)),
        name="rope_tables",
    )(pos.reshape(1, T), inv64.reshape(32, 1), inv32.reshape(16, 1))


def _store_v_aug(ref, vt, n_heads):
    dv = vt.shape[0] // n_heads
    tm = vt.shape[1]
    ones_row = (lax.broadcasted_iota(jnp.int32, (V_AUG, tm), 0) == 0).astype(ref.dtype)
    for g in range(n_heads):
        r0 = g * (dv + V_AUG)
        ref[r0:r0 + dv, :] = vt[g * dv:(g + 1) * dv].astype(ref.dtype)
        ref[r0 + dv:r0 + dv + V_AUG, :] = ones_row

def _inproj_kernel(x_ref, nrm_ref, wr_ref, wt_ref, c64t_ref, s64t_ref, c32t_ref, s32t_ref,
                   kc_ref, ks_ref, kw_ref, bk_ref, vc_ref, cq_ref, ckv_ref,
                   aqt_ref, bqt_ref, vst_ref, vwt_ref, bvt_ref, gt_ref, kpet_ref, *, n_seq_tiles):
    h = _rms(x_ref[...], nrm_ref[...]).astype(BF16)
    zr = _dot(h, wr_ref[...])
    tm, dh = zr.shape[0], HEAD_DIM
    for g in range(A_KV_GROUPS):
        vc_ref[g] = zr[:, g * dh:(g + 1) * dh]
    cq_ref[...] = zr[:, 128:128 + C_Q_RANK]
    ckv_ref[...] = zr[:, 128 + C_Q_RANK:128 + C_Q_RANK + C_KV_RANK]

    zt = _dot_nt(wt_ref[...], h)
    ct = c64t_ref[...]
    st = s64t_ref[...]

    def roped_t(r0):
        x1 = zt[r0:r0 + 32]
        x2 = zt[r0 + 32:r0 + 64]
        return jnp.concatenate([x1 * ct - x2 * st, x2 * ct + x1 * st], axis=0)

    kc = jnp.concatenate([roped_t(_T_KC), roped_t(_T_KC + dh)], axis=0).T
    kw = jnp.concatenate([roped_t(_T_KW), roped_t(_T_KW + dh)], axis=0).T
    bk = jnp.concatenate([roped_t(_T_BK), roped_t(_T_BK + dh)], axis=0).T
    pos = (pl.program_id(0) % n_seq_tiles) * tm + lax.broadcasted_iota(jnp.int32, (tm, 128), 0)
    lane = lax.broadcasted_iota(jnp.int32, (tm, 128), 1)
    onehot = (lane - dh == (pos // A_SEL_LEN) % BLOCKS_PER_TILE).astype(F32)
    zero_rows = jnp.zeros((K_AUG, tm), F32)
    for g in range(A_KV_GROUPS):
        kc_ref[g] = kc[:, g * dh:(g + 1) * dh]
        kw_ref[g] = kw[:, g * dh:(g + 1) * dh].astype(BF16)
        bk_ref[g] = bk[:, g * dh:(g + 1) * dh].astype(BF16)
        ks = jnp.concatenate([roped_t(_T_KS + g * dh), zero_rows], axis=0).T
        ks_ref[g] = (ks + onehot).astype(BF16)

    qscale = HEAD_DIM ** -0.5 * LOG2E
    for q_ref, base in ((aqt_ref, _T_AQ), (bqt_ref, _T_BQ)):
        for hh in range(8):
            r0 = base + hh * HEAD_DIM
            x1 = zt[r0:r0 + 32]
            x2 = zt[r0 + 32:r0 + 64]
            q_ref[hh * 64:hh * 64 + 32, :] = ((x1 * ct - x2 * st) * qscale).astype(BF16)
            q_ref[hh * 64 + 32:hh * 64 + 64, :] = ((x2 * ct + x1 * st) * qscale).astype(BF16)
    _store_v_aug(vst_ref, zt[_T_VS:_T_VS + 128], A_KV_GROUPS)
    _store_v_aug(vwt_ref, zt[_T_VW:_T_VW + 128], A_KV_GROUPS)
    _store_v_aug(bvt_ref, zt[_T_BV:_T_BV + 128], B_KV_HEADS)
    gt_ref[...] = _sigmoid(zt[_T_G:_T_G + 2 * _GATE_ROWS])
    c3 = c32t_ref[...]
    s3 = s32t_ref[...]
    x1 = zt[_T_KPE:_T_KPE + 16]
    x2 = zt[_T_KPE + 16:_T_KPE + 32]
    kpet_ref[0:16, :] = x1 * c3 - x2 * s3
    kpet_ref[16:32, :] = x2 * c3 + x1 * s3


def _inproj(x, mix_norm, w_in, tables, tm, B, S):
    T = x.shape[0]
    G, dh = A_KV_GROUPS, HEAD_DIM
    c64t, s64t, c32t, s32t = tables
    sizes = (512, 128, 128, 128, 128, 128, 128, 24, 512, 128, 128, 256, 256, 32)
    offs = np.concatenate([[0], np.cumsum(sizes)])
    (w_aq, w_kc, w_vc, w_ks, w_vs, w_kw, w_vw, w_g, w_bq, w_bk, w_bv, w_cq, w_ckv, w_kpe) = [
        w_in[:, int(offs[i]):int(offs[i + 1])] for i in range(len(sizes))]
    w_row = jnp.concatenate([w_vc, w_cq, w_ckv], axis=1).astype(BF16)
    wg = w_g.reshape(D_MODEL, A_KV_GROUPS, A_GROUP_HEADS * 3)
    wg = jnp.pad(wg, ((0, 0), (0, 0), (0, _GATE_ROWS - A_GROUP_HEADS * 3))).reshape(D_MODEL, 2 * _GATE_ROWS)
    w_t = jnp.concatenate([w_aq, w_bq, w_vs, w_vw, w_bv, wg, w_kpe, w_kc, w_ks, w_kw, w_bk],
                          axis=1).T.astype(BF16)

    row = lambda n: pl.BlockSpec((tm, n), lambda i: (i, 0))
    colT = lambda n: pl.BlockSpec((n, tm), lambda i: (0, i))
    full = lambda a: pl.BlockSpec(a.shape, lambda i: (0,) * a.ndim)
    nrm = mix_norm.reshape(1, D_MODEL)
    vrows = A_KV_GROUPS * (HEAD_DIM + V_AUG)
    nst = S // tm
    grp = lambda w: pl.BlockSpec((G, tm, w), lambda i: (i // nst, i % nst, 0))
    outs = pl.pallas_call(
        functools.partial(_inproj_kernel, n_seq_tiles=nst),
        grid=(T // tm,),
        in_specs=[row(D_MODEL), full(nrm), full(w_row), full(w_t), colT(32), colT(32), colT(16), colT(16)],
        out_specs=[grp(dh), grp(dh + K_AUG), grp(dh), grp(dh), grp(dh), row(C_Q_RANK), row(C_KV_RANK),
                   colT(512), colT(512), colT(vrows), colT(vrows), colT(vrows), colT(2 * _GATE_ROWS), colT(32)],
        out_shape=[jax.ShapeDtypeStruct((B * G, S, dh), F32), jax.ShapeDtypeStruct((B * G, S, dh + K_AUG), BF16),
                   jax.ShapeDtypeStruct((B * G, S, dh), BF16), jax.ShapeDtypeStruct((B * G, S, dh), BF16),
                   jax.ShapeDtypeStruct((B * G, S, dh), F32), jax.ShapeDtypeStruct((T, C_Q_RANK), F32),
                   jax.ShapeDtypeStruct((T, C_KV_RANK), F32),
                   jax.ShapeDtypeStruct((512, T), BF16), jax.ShapeDtypeStruct((512, T), BF16),
                   jax.ShapeDtypeStruct((vrows, T), BF16), jax.ShapeDtypeStruct((vrows, T), BF16),
                   jax.ShapeDtypeStruct((vrows, T), BF16), jax.ShapeDtypeStruct((2 * _GATE_ROWS, T), F32),
                   jax.ShapeDtypeStruct((32, T), F32)],
        compiler_params=_cparams(("arbitrary",)),
        name="in_projection",
    )(x, nrm, w_row, w_t, c64t, s64t, c32t, s32t)
    return outs


def _gelu_tanh(x):
    return 0.5 * x * (1.0 + jnp.tanh(math.sqrt(2.0 / math.pi) * (x + 0.044715 * (x * x * x))))


def _cmp_kernel(seg_ref, pos_ref, w1_ref, w2_ref, o_ref, *, transposed):
    seg = seg_ref[0]
    n_seg = seg.shape[0]
    half = seg.shape[1]
    a = (seg + pos_ref[0:1, :]).astype(BF16)
    b = (seg + pos_ref[1:2, :]).astype(BF16)
    u = _dot(a, w1_ref[0:half, :])
    v = _dot(b, w1_ref[half:2 * half, :])
    pre = u + pltpu.roll(v, n_seg - 1, 0)
    g = _gelu_tanh(pre).astype(BF16)
    if transposed:
        o_ref[0] = _dot_nt(w2_ref[...], g).astype(BF16)
    else:
        o_ref[0] = _dot(g, w2_ref[...]).astype(BF16)


def _compress_one(t, pos, w1, w2, B, S, transposed):
    G, dh = A_KV_GROUPS, HEAD_DIM
    n_seg = S // A_CMP_STRIDE
    seg_w = A_CMP_STRIDE * dh
    seg = t.reshape(B * G, n_seg, seg_w)
    w2 = (w2.T if transposed else w2).astype(BF16)
    out_blk = (1, dh, n_seg) if transposed else (1, n_seg, dh)
    full = lambda a: pl.BlockSpec(a.shape, lambda i: (0,) * a.ndim)
    pos = pos.reshape(2, seg_w)
    w1 = w1.astype(BF16)
    return pl.pallas_call(
        functools.partial(_cmp_kernel, transposed=transposed),
        grid=(B * G,),
        in_specs=[pl.BlockSpec((1, n_seg, seg_w), lambda i: (i, 0, 0)), full(pos), full(w1), full(w2)],
        out_specs=pl.BlockSpec(out_blk, lambda i: (i, 0, 0)),
        out_shape=jax.ShapeDtypeStruct((B * G,) + out_blk[1:], BF16),
        compiler_params=_cparams(("arbitrary",)),
        name="nsa_compress_v" if transposed else "nsa_compress_k",
    )(seg, pos, w1, w2)


def _compress(kc, vc, pos_k, w1_k, w2_k, pos_v, w1_v, w2_v, B, S):
    return (_compress_one(kc, pos_k, w1_k, w2_k, B, S, False),
            _compress_one(vc, pos_v, w1_v, w2_v, B, S, True))


def _lane_heads(blk, dh, n):
    return jnp.concatenate([blk[r * dh:(r + 1) * dh] for r in range(n)], axis=1)


def _argmax_lowest(val, idx):
    n = val.shape[0]
    v = [val[i:i + 8] for i in range(0, n, 8)]
    j = [idx[i:i + 8] for i in range(0, n, 8)]
    while len(v) > 1:
        nv, nj = [], []
        for a in range(0, len(v) - 1, 2):
            take_b = v[a + 1] > v[a]
            nv.append(jnp.where(take_b, v[a + 1], v[a]))
            nj.append(jnp.where(take_b, j[a + 1], j[a]))
        if len(v) % 2:
            nv.append(v[-1])
            nj.append(j[-1])
        v, j = nv, nj
    vmax = jnp.max(v[0], axis=0, keepdims=True)
    return jnp.min(jnp.where(v[0] == vmax, j[0], float(n)), axis=0, keepdims=True)


def _nsa_select(g, q_ref, kc_ref, vct_ref, smt_ref, selb_ref, qa_ref, tq, tq1, n_seg, n_sel):
    QC, R, dh = A_Q_CHUNK, A_GROUP_HEADS, HEAD_DIM
    NQ = R * QC
    qT = _lane_heads(q_ref[g * R * dh:(g + 1) * R * dh, :], dh, R)
    qa_ref[g] = qT

    s = _dot(kc_ref[g, 0:n_seg, :], qT)
    cend = lax.broadcasted_iota(jnp.int32, (n_seg, 1), 0) * A_CMP_STRIDE + (A_CMP_LEN - 1)
    s = jnp.where(cend <= tq, s, NEG_INF)
    m = jnp.max(s, axis=0, keepdims=True)
    e = jnp.exp2(s - m)
    l = jnp.sum(e, axis=0, keepdims=True)
    inv_l = jnp.where(tq >= A_CMP_LEN - 1, 1.0 / l, 0.0)
    p = e * inv_l
    o_cmp = _dot(vct_ref[g, :, 0:n_seg], p.astype(BF16))

    psum = p[:, 0:QC]
    for r in range(1, R):
        psum = psum + p[:, r * QC:(r + 1) * QC]
    hi = psum.astype(BF16)
    lo = (psum - hi.astype(F32)).astype(BF16)
    smt = smt_ref[0:n_sel, 0:n_seg]
    imp = _dot(smt, hi) + _dot(smt, lo)
    jid = lax.broadcasted_iota(jnp.int32, (n_sel, QC), 0).astype(F32)
    cur = (tq1 // A_SEL_LEN).astype(F32)
    valid = jid <= cur
    forced = (jid == 0.0) | (jid == cur) | (jid == cur - 1.0)

    ineligible, picked = -(2.0 ** 127), -(2.0 ** 126)
    val = jnp.where(valid & jnp.logical_not(forced), imp, ineligible)
    for _ in range(min(A_TOPK, selb_ref.shape[1]) - 3):
        first = _argmax_lowest(val, jid)
        val = jnp.where(jid == first, picked, val)
    selb_ref[g, 0:n_sel, :] = jnp.where((forced | (val == picked)) & valid, 0.0, NEG_INF)
    return o_cmp


def _nsa_kernel(q_ref, kc_ref, vct_ref, smt_ref, ks_ref, vst_ref, kw_ref, vwt_ref, g_ref,
                o_ref, selb_ref, qa_ref, sa_ref, sb_ref):
    QC, R, dh, G = A_Q_CHUNK, A_GROUP_HEADS, HEAD_DIM, A_KV_GROUPS
    NQ = R * QC
    VR = dh + V_AUG
    c = pl.program_id(1)
    t0 = c * QC
    tq = t0 + (lax.broadcasted_iota(jnp.int32, (1, NQ), 1) & (QC - 1))
    tq1 = t0 + lax.broadcasted_iota(jnp.int32, (1, QC), 1)
    gt = g_ref[...]

    def before_key_loop(n_seg, n_sel, window_may_clip):
        o_cmp = [_nsa_select(g, q_ref, kc_ref, vct_ref, smt_ref, selb_ref, qa_ref, tq, tq1, n_seg, n_sel)
                 for g in range(G)]
        WL = A_WINDOW + QC
        if window_may_clip:
            start = pl.multiple_of(jnp.maximum(t0 - A_WINDOW, 0), QC)
            kp = start + lax.broadcasted_iota(jnp.int32, (WL, 1), 0)
            visible = (kp <= tq) & (kp > tq - A_WINDOW)
        else:
            start = pl.multiple_of(t0 - A_WINDOW, QC)
            io = lax.broadcasted_iota(jnp.int32, (QC, QC), 0)
            qo = lax.broadcasted_iota(jnp.int32, (QC, QC), 1)
            lo_bias = jnp.concatenate([jnp.where(io > qo, 0.0, NEG_INF)] * R, axis=1)
            hi_bias = jnp.concatenate([jnp.where(io <= qo, 0.0, NEG_INF)] * R, axis=1)
        o_cw = []
        for g in range(G):
            s_w = _dot(kw_ref[g, pl.ds(start, WL), :], qa_ref[g])
            if window_may_clip:
                s_w = jnp.where(visible, s_w, NEG_INF)
            else:
                s_w = jnp.concatenate([s_w[0:QC] + lo_bias, s_w[QC:WL - QC], s_w[WL - QC:WL] + hi_bias], axis=0)
            m_w = jnp.max(s_w, axis=0, keepdims=True)
            p_w = jnp.exp2(s_w - m_w)
            acc_w = _dot(vwt_ref[g * VR:(g + 1) * VR, pl.ds(start, WL)], p_w.astype(BF16))
            o_win = acc_w[0:dh] * (1.0 / acc_w[dh:dh + 1])
            pieces = []
            for r in range(R):
                sl = slice(r * QC, (r + 1) * QC)
                gr = g * _GATE_ROWS + 3 * r
                pieces.append(gt[gr:gr + 1] * o_cmp[g][:, sl] + gt[gr + 2:gr + 3] * o_win[:, sl])
            o_cw.append(jnp.concatenate(pieces, axis=1))
        return tuple(o_cw)

    n_seg_all, n_sel_all = kc_ref.shape[1], selb_ref.shape[1]
    n_cls = max(1, min(NSA_PREFIX_CLASSES, n_seg_all // LANES))
    chunks_per_cls = (n_seg_all // (A_Q_CHUNK // A_CMP_STRIDE)) // n_cls
    assert n_cls == 1 or chunks_per_cls * QC >= A_WINDOW
    classes = [functools.partial(before_key_loop, (k + 1) * n_seg_all // n_cls, (k + 1) * n_sel_all // n_cls, k == 0)
               for k in range(n_cls)]
    o_cw = lax.switch(c // chunks_per_cls, classes) if n_cls > 1 else classes[0]()

    bpt = BLOCKS_PER_TILE
    KT = sa_ref.shape[1]
    kiota = lax.broadcasted_iota(jnp.int32, (KT, 1), 0)
    pad_rows = jnp.zeros((16 - bpt, NQ), F32)

    zero_rows = jnp.zeros((K_AUG - 16, NQ), BF16)

    def scores(t, dst):
        k0 = pl.multiple_of(t * KT, KT)
        for g in range(G):
            b8 = selb_ref[g, pl.ds(pl.multiple_of((t // 2) * bpt, bpt), bpt), :]
            rows = jnp.concatenate([jnp.concatenate([b8] * R, axis=1), pad_rows], axis=0).astype(BF16)
            rhs = jnp.concatenate([qa_ref[g], rows, zero_rows], axis=0)
            dst[g] = _dot(ks_ref[g, pl.ds(k0, KT), :], rhs)

    def update(t, src, carry, causal):
        k0 = pl.multiple_of(t * KT, KT)
        out = []
        for g in range(G):
            m_i, acc = carry[2 * g], carry[2 * g + 1]
            s_i = src[g]
            if causal:
                cb = jnp.where((k0 + kiota) <= tq1, 0.0, NEG_INF)
                s_i = s_i + jnp.concatenate([cb] * R, axis=1)
            m_new = jnp.maximum(m_i, jnp.max(s_i, axis=0, keepdims=True))
            alpha = jnp.exp2(m_i - m_new)
            p_i = jnp.exp2(s_i - m_new)
            acc = alpha * acc + _dot(vst_ref[g * VR:(g + 1) * VR, pl.ds(k0, KT)], p_i.astype(BF16))
            out.extend((m_new, acc))
        return tuple(out)

    def sel_step(j, carry):
        scores(2 * j + 1, sb_ref)
        carry = update(2 * j, sa_ref, carry, False)
        scores(2 * j + 2, sa_ref)
        return update(2 * j + 1, sb_ref, carry, False)

    n_tiles = (t0 + QC + 2 * KT - 1) // (2 * KT)
    scores(0, sa_ref)
    init = (jnp.full((1, NQ), NEG_INF, F32), jnp.zeros((VR, NQ), F32)) * G
    carry = _tile_loop(n_tiles - 1, sel_step, init)
    last = n_tiles - 1
    scores(2 * last + 1, sb_ref)
    carry = update(2 * last, sa_ref, carry, True)
    carry = update(2 * last + 1, sb_ref, carry, True)

    outs = []
    for g in range(G):
        acc_s = carry[2 * g + 1]
        o_sel = acc_s[0:dh] * (1.0 / acc_s[dh:dh + 1])
        for r in range(R):
            sl = slice(r * QC, (r + 1) * QC)
            gr = g * _GATE_ROWS + 3 * r
            outs.append(o_cw[g][:, sl] + gt[gr + 1:gr + 2] * o_sel[:, sl])
    o_ref[...] = jnp.concatenate(outs, axis=0).astype(o_ref.dtype)


def _sel_map_t(S):
    n_cmp = (S - A_CMP_LEN) // A_CMP_STRIDE + 1
    n_seg = S // A_CMP_STRIDE
    n_sel = S // A_SEL_LEN
    tok = np.arange(n_cmp)[:, None] * A_CMP_STRIDE + np.arange(A_CMP_LEN)[None, :]
    sm = np.zeros((n_seg, n_sel), np.float32)
    np.add.at(sm, (np.repeat(np.arange(n_cmp), A_CMP_LEN), (tok // A_SEL_LEN).reshape(-1)), 1.0 / A_CMP_LEN)
    return jnp.asarray(sm.T, BF16)


def _nsa(aqt, k_cmp, v_cmp_t, ks_aug, vst, kw, vwt, gt, B, S):
    G, R, dh, QC = A_KV_GROUPS, A_GROUP_HEADS, HEAD_DIM, A_Q_CHUNK
    T = B * S
    nq = S // QC
    n_seg = S // A_CMP_STRIDE
    n_sel = S // A_SEL_LEN
    VR = dh + V_AUG
    assert S >= A_WINDOW + QC
    smt = _sel_map_t(S)
    qspec = pl.BlockSpec((G * R * dh, QC), lambda b, c: (0, b * nq + c))
    return pl.pallas_call(
        _nsa_kernel,
        grid=(B, nq),
        in_specs=[qspec,
                  pl.BlockSpec((G, n_seg, dh), lambda b, c: (b, 0, 0)),
                  pl.BlockSpec((G, dh, n_seg), lambda b, c: (b, 0, 0)),
                  pl.BlockSpec((n_sel, n_seg), lambda b, c: (0, 0)),
                  pl.BlockSpec((G, S, dh + K_AUG), lambda b, c: (b, 0, 0)),
                  pl.BlockSpec((G * VR, S), lambda b, c: (0, b)),
                  pl.BlockSpec((G, S, dh), lambda b, c: (b, 0, 0)),
                  pl.BlockSpec((G * VR, S), lambda b, c: (0, b)),
                  pl.BlockSpec((G * _GATE_ROWS, QC), lambda b, c: (0, b * nq + c))],
        out_specs=qspec,
        out_shape=jax.ShapeDtypeStruct((G * R * dh, T), BF16),
        scratch_shapes=[pltpu.VMEM((G, n_sel, QC), F32), pltpu.VMEM((G, dh, R * QC), BF16),
                        pltpu.VMEM((G, KV_TILE // 2, R * QC), F32), pltpu.VMEM((G, KV_TILE // 2, R * QC), F32)],
        compiler_params=_cparams(("arbitrary", "arbitrary")),
        name="nsa_attention",
    )(aqt, k_cmp, v_cmp_t, smt, ks_aug, vst, kw, vwt, gt)


SWA_BLOCKS_PER_STEP = 8


def _swa_kernel(q_ref, kp_ref, kc_ref, vpt_ref, vct_ref, sink_ref, o_ref):
    BLK, G, R, dh, NB = B_BLOCK, B_KV_HEADS, B_HEADS // B_KV_HEADS, HEAD_DIM, SWA_BLOCKS_PER_STEP
    NQ = R * BLK
    VR = dh + V_AUG
    n = pl.program_id(1)
    kj = lax.broadcasted_iota(jnp.int32, (2 * BLK, 1), 0) - BLK
    qi = lax.broadcasted_iota(jnp.int32, (1, NQ), 1) & (BLK - 1)
    band = (kj <= qi) & (kj > qi - B_WINDOW)
    for g in range(G):
        sink = sink_ref[g] * LOG2E
        for j in range(NB):
            cols = slice(j * BLK, (j + 1) * BLK)
            qT = _lane_heads(q_ref[g * R * dh:(g + 1) * R * dh, cols], dh, R)
            if j == 0:
                k = jnp.concatenate([kp_ref[g], kc_ref[g, 0:BLK, :]], axis=0)
                vT = jnp.concatenate([vpt_ref[g * VR:(g + 1) * VR, :], vct_ref[g * VR:(g + 1) * VR, cols]], axis=1)
                ok = band & ((n > 0) | (kj >= 0))
            else:
                k = kc_ref[g, (j - 1) * BLK:(j + 1) * BLK, :]
                vT = vct_ref[g * VR:(g + 1) * VR, (j - 1) * BLK:(j + 1) * BLK]
                ok = band
            s = jnp.where(ok, _dot(k, qT), NEG_INF)
            m = jnp.maximum(jnp.max(s, axis=0, keepdims=True), sink)
            p = jnp.exp2(s - m)
            acc = _dot(vT, p.astype(BF16))
            o = acc[0:dh] * (1.0 / (acc[dh:dh + 1] + jnp.exp2(sink - m)))
            for r in range(R):
                row0 = (g * R + r) * dh
                o_ref[row0:row0 + dh, cols] = o[:, r * BLK:(r + 1) * BLK].astype(o_ref.dtype)


def _swa(bqt, bk, bvt, sinks, B, S):
    G, R, dh, BLK = B_KV_HEADS, B_HEADS // B_KV_HEADS, HEAD_DIM, B_BLOCK
    T = B * S
    nb = S // BLK
    sink_cols = jnp.repeat(sinks.astype(F32).reshape(G, R), BLK, axis=1).reshape(G, 1, R * BLK)
    NB = min(SWA_BLOCKS_PER_STEP, nb)
    ns = nb // NB
    prev = lambda n: jnp.maximum(n * NB - 1, 0)
    VR = dh + V_AUG
    qspec = pl.BlockSpec((G * R * dh, NB * BLK), lambda b, n: (0, b * ns + n))
    return pl.pallas_call(
        _swa_kernel,
        grid=(B, ns),
        in_specs=[qspec,
                  pl.BlockSpec((G, BLK, dh), lambda b, n: (b, prev(n), 0)),
                  pl.BlockSpec((G, NB * BLK, dh), lambda b, n: (b, n, 0)),
                  pl.BlockSpec((G * VR, BLK), lambda b, n: (0, b * nb + prev(n))),
                  pl.BlockSpec((G * VR, NB * BLK), lambda b, n: (0, b * ns + n)),
                  pl.BlockSpec((G, 1, R * BLK), lambda b, n: (0, 0, 0))],
        out_specs=qspec,
        out_shape=jax.ShapeDtypeStruct((G * R * dh, T), BF16),
        compiler_params=_cparams(("arbitrary", "arbitrary")),
        name="swa_sink_attention",
    )(bqt, bk, bk, bvt, bvt, sink_cols)


def _mla_proj_kernel(cq_ref, ckv_ref, kpet_ref, qn_ref, kvn_ref, wqt_ref, wkvt_ref, c_ref, s_ref,
                     qf_ref, kf_ref, vt_ref):
    hq = _rms(cq_ref[...], qn_ref[...]).astype(BF16)
    hkv = _rms(ckv_ref[...], kvn_ref[...]).astype(BF16)
    qt = _dot_nt(wqt_ref[...], hq)
    kvt = _dot_nt(wkvt_ref[...], hkv)
    c = c_ref[...]
    s = s_ref[...]
    kpe = kpet_ref[...]
    scale = (C_NOPE + C_ROPE) ** -0.5 * LOG2E
    half = C_ROPE // 2
    tm = qt.shape[1]
    zpad = jnp.zeros((C_QK_PAD - C_NOPE - C_ROPE, tm), BF16)
    zpad_f32 = jnp.zeros((C_QK_PAD - C_NOPE - C_ROPE, tm), F32)
    ones_row = (lax.broadcasted_iota(jnp.int32, (V_AUG, tm), 0) == 0).astype(BF16)
    for h in range(C_HEADS):
        b = h * C_QK_PAD
        qf_ref[h, 0:C_NOPE, :] = (qt[b:b + C_NOPE] * scale).astype(BF16)
        x1 = qt[b + C_NOPE:b + C_NOPE + half]
        x2 = qt[b + C_NOPE + half:b + C_NOPE + C_ROPE]
        qf_ref[h, C_NOPE:C_NOPE + half, :] = ((x1 * c - x2 * s) * scale).astype(BF16)
        qf_ref[h, C_NOPE + half:C_NOPE + C_ROPE, :] = ((x2 * c + x1 * s) * scale).astype(BF16)
        qf_ref[h, C_NOPE + C_ROPE:C_QK_PAD, :] = zpad
        kft = jnp.concatenate([kvt[b:b + C_NOPE], kpe, zpad_f32], axis=0)
        kf_ref[h] = kft.T.astype(BF16)
        vt_ref[h, 0:C_V, :] = kvt[b + C_NOPE:b + C_NOPE + C_V].astype(BF16)
        vt_ref[h, C_V:C_V + V_AUG, :] = ones_row


def _mla_proj(cq, ckv, kpet, q_norm, w_q_up, kv_norm, w_kv_up, c32t, s32t, tm):
    T = cq.shape[0]
    H = C_HEADS
    wq = w_q_up.reshape(C_Q_RANK, H, C_NOPE + C_ROPE)
    wq = jnp.pad(wq, ((0, 0), (0, 0), (0, C_QK_PAD - C_NOPE - C_ROPE))).reshape(C_Q_RANK, H * C_QK_PAD)
    wqt = wq.T.astype(BF16)
    wkvt = w_kv_up.T.astype(BF16)
    full = lambda a: pl.BlockSpec(a.shape, lambda i: (0,) * a.ndim)
    qn = q_norm.reshape(1, C_Q_RANK)
    kvn = kv_norm.reshape(1, C_KV_RANK)
    return pl.pallas_call(
        _mla_proj_kernel,
        grid=(T // tm,),
        in_specs=[pl.BlockSpec((tm, C_Q_RANK), lambda i: (i, 0)),
                  pl.BlockSpec((tm, C_KV_RANK), lambda i: (i, 0)),
                  pl.BlockSpec((C_ROPE, tm), lambda i: (0, i)),
                  full(qn), full(kvn), full(wqt), full(wkvt),
                  pl.BlockSpec((C_ROPE // 2, tm), lambda i: (0, i)),
                  pl.BlockSpec((C_ROPE // 2, tm), lambda i: (0, i))],
        out_specs=[pl.BlockSpec((H, C_QK_PAD, tm), lambda i: (0, 0, i)),
                   pl.BlockSpec((H, tm, C_QK_PAD), lambda i: (0, i, 0)),
                   pl.BlockSpec((H, C_V + V_AUG, tm), lambda i: (0, 0, i))],
        out_shape=[jax.ShapeDtypeStruct((H, C_QK_PAD, T), BF16),
                   jax.ShapeDtypeStruct((H, T, C_QK_PAD), BF16),
                   jax.ShapeDtypeStruct((H, C_V + V_AUG, T), BF16)],
        compiler_params=_cparams(("arbitrary",)),
        name="mla_projection",
    )(cq, ckv, kpet, qn, kvn, wqt, wkvt, c32t, s32t)


MLA_HEADS_PER_STEP = 2


def _mla_kernel(q_ref, kf_ref, vt_ref, o_ref, sa_ref, sb_ref):
    HB = q_ref.shape[0]
    S = q_ref.shape[2]
    KT, QT = sa_ref.shape[1], sa_ref.shape[2]
    nq = S // QT
    lane = lax.broadcasted_iota(jnp.int32, (1, QT), 1)
    kiota = lax.broadcasted_iota(jnp.int32, (KT, 1), 0)

    def scores(qi, t, dst):
        k0 = pl.multiple_of(t * KT, KT)
        q0 = pl.multiple_of(qi * QT, QT)
        for h in range(HB):
            dst[h] = _dot(kf_ref[h, pl.ds(k0, KT), :], q_ref[h, :, pl.ds(q0, QT)])

    def update(qi, t, src, carry, causal):
        k0 = pl.multiple_of(t * KT, KT)
        out = []
        for h in range(HB):
            m_i, acc = carry[2 * h], carry[2 * h + 1]
            s = src[h]
            if causal:
                s = jnp.where((k0 + kiota) <= qi * QT + lane, s, NEG_INF)
            m_new = jnp.maximum(m_i, jnp.max(s, axis=0, keepdims=True))
            alpha = jnp.exp2(m_i - m_new)
            p = jnp.exp2(s - m_new)
            acc = alpha * acc + _dot(vt_ref[h, :, pl.ds(k0, KT)], p.astype(BF16))
            out.extend((m_new, acc))
        return tuple(out)

    def q_tile(qi, _):
        def step(j, carry):
            scores(qi, 2 * j + 1, sb_ref)
            carry = update(qi, 2 * j, sa_ref, carry, False)
            scores(qi, 2 * j + 2, sa_ref)
            return update(qi, 2 * j + 1, sb_ref, carry, False)

        init = (jnp.full((1, QT), NEG_INF, F32), jnp.zeros((C_V + V_AUG, QT), F32)) * HB
        carry = _tile_loop(qi, step, init)
        scores(qi, 2 * qi + 1, sb_ref)
        carry = update(qi, 2 * qi, sa_ref, carry, True)
        scores(jnp.minimum(qi + 1, nq - 1), 0, sa_ref)
        carry = update(qi, 2 * qi + 1, sb_ref, carry, True)
        q0 = pl.multiple_of(qi * QT, QT)
        for h in range(HB):
            acc = carry[2 * h + 1]
            o_ref[h * C_V:(h + 1) * C_V, pl.ds(q0, QT)] = (acc[0:C_V] * (1.0 / acc[C_V:C_V + 1])).astype(o_ref.dtype)
        return 0

    scores(0, 0, sa_ref)
    lax.fori_loop(0, nq, q_tile, 0)


def _mla(qft, kf, vt, B, S):
    H, HB = C_HEADS, MLA_HEADS_PER_STEP
    T = B * S
    QT = min(KV_TILE, S)
    nq = S // QT
    return pl.pallas_call(
        _mla_kernel,
        grid=(B, H // HB),
        in_specs=[pl.BlockSpec((HB, C_QK_PAD, S), lambda b, h: (h, 0, b)),
                  pl.BlockSpec((HB, S, C_QK_PAD), lambda b, h: (h, b, 0)),
                  pl.BlockSpec((HB, C_V + V_AUG, S), lambda b, h: (h, 0, b))],
        out_specs=pl.BlockSpec((HB * C_V, S), lambda b, h: (h, b)),
        out_shape=jax.ShapeDtypeStruct((H * C_V, T), BF16),
        scratch_shapes=[pltpu.VMEM((HB, QT // 2, QT), F32), pltpu.VMEM((HB, QT // 2, QT), F32)],
        compiler_params=_cparams(("arbitrary", "arbitrary")),
        name="mla_attention",
    )(qft, kf, vt)


def _merge_kernel(x_ref, oa_ref, ob_ref, oc_ref, nrm_ref, wg_ref, wa_ref, wb_ref, wc_ref, wo_ref, o_ref):
    x = x_ref[...]
    h = _rms(x, nrm_ref[...]).astype(BF16)
    g = _sigmoid(_dot(h, wg_ref[...]))
    D = D_MODEL
    m = (g[:, 0:D] * _dot_tn(oa_ref[...], wa_ref[...]) + g[:, D:2 * D] * _dot_tn(ob_ref[...], wb_ref[...])
         + g[:, 2 * D:3 * D] * _dot_tn(oc_ref[...], wc_ref[...]))
    o_ref[...] = x + _dot(m.astype(BF16), wo_ref[...])


def _merge(x, oat, obt, oct, mix_norm, w_gate, w_a, w_b, w_c, w_out, tm):
    T = x.shape[0]
    row = lambda n: pl.BlockSpec((tm, n), lambda i: (i, 0))
    colT = lambda n: pl.BlockSpec((n, tm), lambda i: (0, i))
    full = lambda a: pl.BlockSpec(a.shape, lambda i: (0,) * a.ndim, pipeline_mode=pl.Buffered(1))
    nrm = mix_norm.reshape(1, D_MODEL)
    ws = [w.astype(BF16) for w in (w_gate, w_a, w_b, w_c, w_out)]
    return pl.pallas_call(
        _merge_kernel,
        grid=(T // tm,),
        in_specs=[row(D_MODEL), colT(512), colT(512), colT(512), full(nrm)] + [full(w) for w in ws],
        out_specs=row(D_MODEL),
        out_shape=jax.ShapeDtypeStruct((T, D_MODEL), F32),
        compiler_params=_cparams(("arbitrary",)),
        name="branch_merge",
    )(x, oat, obt, oct, nrm, *ws)


def _ffn_kernel(x_ref, p_ref, fn_ref, wg_ref, wu_ref, wd_ref, pn_ref, wpg_ref, wpe_ref, fin_ref, o_ref,
                *, final):
    x = x_ref[...]
    h = _rms(x, fn_ref[...]).astype(BF16)
    a = _dot(h, wg_ref[...])
    u = _dot(h, wu_ref[...])
    y = (a * _sigmoid(a) * u).astype(BF16)
    x = x + _dot(y, wd_ref[...])
    gate = _sigmoid(_dot(_rms(x, pn_ref[...]).astype(BF16), wpg_ref[...]))
    x = x + gate * _dot(p_ref[...].astype(BF16), wpe_ref[...])
    if final:
        x = _rms(x, fin_ref[...])
    o_ref[...] = x


def _ffn_ple(x, p_i, ffn_norm, w_gate, w_up, w_down, ple_norm, w_ple_gate, w_ple_proj, final_norm, final, tm):
    T = x.shape[0]
    row = lambda n: pl.BlockSpec((tm, n), lambda i: (i, 0))
    full = lambda a: pl.BlockSpec(a.shape, lambda i: (0,) * a.ndim, pipeline_mode=pl.Buffered(1))
    fn = ffn_norm.reshape(1, D_MODEL)
    pn = ple_norm.reshape(1, D_MODEL)
    fin = final_norm.reshape(1, D_MODEL)
    wg, wu, wd, wpg, wpe = [w.astype(BF16) for w in (w_gate, w_up, w_down, w_ple_gate, w_ple_proj)]
    return pl.pallas_call(
        functools.partial(_ffn_kernel, final=final),
        grid=(T // tm,),
        in_specs=[row(D_MODEL), row(PLE_DIM), full(fn), full(wg), full(wu), full(wd), full(pn),
                  full(wpg), full(wpe), full(fin)],
        out_specs=row(D_MODEL),
        out_shape=jax.ShapeDtypeStruct((T, D_MODEL), F32),
        compiler_params=_cparams(("arbitrary",)),
        name="ffn_ple",
    )(x, p_i, fn, wg, wu, wd, pn, wpg, wpe, fin)


def _layer(x, p_i, tables, B, S, final_norm, final, mix_norm, w_in, a_cmp_pos_k, a_cmp_w1_k, a_cmp_w2_k,
           a_cmp_pos_v, a_cmp_w1_v, a_cmp_w2_v, b_sinks, c_q_norm, c_w_q_up, c_kv_norm, c_w_kv_up,
           w_branch_gate, w_branch_a, w_branch_b, w_branch_c, w_out, ffn_norm, w_ffn_gate, w_ffn_up,
           w_ffn_down, ple_norm, w_ple_proj, w_ple_gate):
    T = B * S
    tm = min(1024, T)
    (kc, ks_aug, kw, bk, vc, cq, ckv, aqt, bqt, vst, vwt, bvt, gt, kpet) = _inproj(
        x, mix_norm, w_in, tables, tm, B, S)

    k_cmp, v_cmp_t = _compress(kc, vc, a_cmp_pos_k, a_cmp_w1_k, a_cmp_w2_k,
                               a_cmp_pos_v, a_cmp_w1_v, a_cmp_w2_v, B, S)
    oat = _nsa(aqt, k_cmp, v_cmp_t, ks_aug, vst, kw, vwt, gt, B, S)
    obt = _swa(bqt, bk, bvt, b_sinks, B, S)

    qft, kf, vt = _mla_proj(cq, ckv, kpet, c_q_norm, c_w_q_up, c_kv_norm, c_w_kv_up,
                            tables[2], tables[3], tm)
    oct = _mla(qft, kf, vt, B, S)

    x = _merge(x, oat, obt, oct, mix_norm, w_branch_gate, w_branch_a, w_branch_b, w_branch_c,
               w_out, min(512, T))
    return _ffn_ple(x, p_i, ffn_norm, w_ffn_gate, w_ffn_up, w_ffn_down, ple_norm, w_ple_gate, w_ple_proj,
                    final_norm, final, min(512, T))


def kernel(x, p, positions, mix_norm, w_in, a_cmp_pos_k, a_cmp_w1_k, a_cmp_w2_k, a_cmp_pos_v, a_cmp_w1_v, a_cmp_w2_v, b_sinks, c_q_norm, c_w_q_up, c_kv_norm, c_w_kv_up, w_branch_gate, w_branch_a, w_branch_b, w_branch_c, w_out, ffn_norm, w_ffn_gate, w_ffn_up, w_ffn_down, ple_norm, w_ple_proj, w_ple_gate, final_norm):
    B, S, D = x.shape
    depth = p.shape[0]
    T = B * S
    tables = _rope_tables(positions, min(512, T))
    xf = x.reshape(T, D)
    per_layer = (mix_norm, w_in, a_cmp_pos_k, a_cmp_w1_k, a_cmp_w2_k, a_cmp_pos_v, a_cmp_w1_v, a_cmp_w2_v,
                 b_sinks, c_q_norm, c_w_q_up, c_kv_norm, c_w_kv_up, w_branch_gate, w_branch_a, w_branch_b,
                 w_branch_c, w_out, ffn_norm, w_ffn_gate, w_ffn_up, w_ffn_down, ple_norm, w_ple_proj,
                 w_ple_gate)
    for i in range(depth):
        xf = _layer(xf, p[i].reshape(T, PLE_DIM), tables, B, S, final_norm, i == depth - 1,
                    *[w[i] for w in per_layer])
    return xf.reshape(B, S, D)
```
